```python
import math
import jax, jax.numpy as jnp
from jax import lax
import numpy as np

D_MODEL = 4096
BATCH = 2
SEQ = 4096
DEPTH = 2

HEAD_DIM = 128
A_HEADS = 8
A_Q_LORA = 768
A_KV_LORA = 512
A_NOPE = 128
A_ROPE = 64
A_V = 128
B_PAIRS = ((128, 1), (512, 4), (2048, 16))
B_HEADS_PER_PAIR = 4
B_HEADS = B_HEADS_PER_PAIR * len(B_PAIRS)
C_HEADS = 8
C_KV_HEADS = 2
D_HEADS = 8
D_KV_HEADS = 2
D_WINDOW = 128

GRID_W = 64
ROPE_THETA = 10000.0
N_BUCKETS = 32
REL_MAX_DIST = 1024
BLOCK = 128
EPS = 1e-6
NEG = -1e30

W_A = A_HEADS * A_V
W_B = B_HEADS * HEAD_DIM
W_C = C_HEADS * HEAD_DIM
W_D = D_HEADS * HEAD_DIM
MIX_WIDTH = W_A + W_B + W_C + W_D
IN_SPLITS = (A_Q_LORA, A_KV_LORA, A_ROPE,
             W_B, W_B, W_B,
             W_C, C_KV_HEADS * HEAD_DIM, C_KV_HEADS * HEAD_DIM,
             W_D, D_KV_HEADS * HEAD_DIM, D_KV_HEADS * HEAD_DIM,
             MIX_WIDTH)
IN_WIDTH = sum(IN_SPLITS)

kernel_name = "hymba_style_mla_dilated_axial_sink_encoder"


def rmsnorm(x, g):
    xf = x.astype(jnp.float32)
    y = xf * lax.rsqrt(jnp.mean(xf * xf, axis=-1, keepdims=True) + EPS)
    return (y * g.astype(jnp.float32)).astype(x.dtype)


def rope_angles(pos, dim):
    inv = ROPE_THETA ** (-jnp.arange(0, dim, 2, dtype=jnp.float32) / dim)
    return pos.astype(jnp.float32)[:, None] * inv[None, :]


def apply_rope(x, ang):
    xf = x.astype(jnp.float32)
    half = x.shape[-1] // 2
    x1, x2 = xf[..., :half], xf[..., half:]
    cos = jnp.cos(ang)[:, None, :]
    sin = jnp.sin(ang)[:, None, :]
    return jnp.concatenate([x1 * cos - x2 * sin, x2 * cos + x1 * sin], axis=-1).astype(x.dtype)


def rel_bucket(rel):
    nb = N_BUCKETS // 2
    max_exact = nb // 2
    ret = jnp.where(rel > 0, nb, 0)
    n = jnp.abs(rel)
    nf = jnp.maximum(n, 1).astype(jnp.float32)
    large = max_exact + (jnp.log(nf / max_exact) / math.log(REL_MAX_DIST / max_exact)
                         * (nb - max_exact)).astype(jnp.int32)
    large = jnp.minimum(large, nb - 1)
    return ret + jnp.where(n < max_exact, n, large)


def dense_attention(q, k, v, scale):
    b, s = q.shape[:2]
    nblk = s // BLOCK
    qb = jnp.moveaxis(q.reshape(b, nblk, BLOCK, *q.shape[2:]), 1, 0)

    def one(qblk):
        logits = jnp.einsum('bqhgd,bkhd->bhgqk', qblk, k).astype(jnp.float32) * scale
        p = jax.nn.softmax(logits, axis=-1).astype(v.dtype)
        return jnp.einsum('bhgqk,bkhd->bqhgd', p, v)

    out = lax.map(one, qb)
    return jnp.moveaxis(out, 0, 1).reshape(b, s, *out.shape[3:])


def dilated_group(q, k, v, dil, side, bias_tab):
    b, s, h, d = q.shape
    offs = dil * jnp.arange(-side, side + 1)
    bias = bias_tab[rel_bucket(offs)].T.astype(jnp.float32)
    nblk = s // BLOCK
    qb = jnp.moveaxis(q.reshape(b, nblk, BLOCK, h, d), 1, 0)
    starts = jnp.arange(nblk) * BLOCK
    scale = d ** -0.5

    def one(args):
        qblk, st = args
        idx = (st + jnp.arange(BLOCK))[:, None] + offs[None, :]
        valid = (idx >= 0) & (idx < s)
        idxc = jnp.clip(idx, 0, s - 1)
        kg = k[:, idxc]
        vg = v[:, idxc]
        logits = jnp.einsum('bqhd,bqkhd->bhqk', qblk, kg).astype(jnp.float32) * scale + bias[None, :, None, :]
        logits = jnp.where(valid[None, None], logits, NEG)
        lse = jax.nn.logsumexp(logits, axis=-1)
        p = jnp.exp(logits - lse[..., None]).astype(v.dtype)
        return jnp.einsum('bhqk,bqkhd->bqhd', p, vg), lse

    o, lse = lax.map(one, (qb, starts))
    o = jnp.moveaxis(o, 0, 1).reshape(b, s, h, d)
    lse = jnp.transpose(lse, (1, 0, 3, 2)).reshape(b, s, h)
    return o, lse


def window_attention(q, k, v, sinks, bias_tab):
    b, s, hk, g, d = q.shape
    nblk = s // BLOCK
    pad = ((0, 0), (BLOCK, BLOCK), (0, 0), (0, 0))
    kp = jnp.pad(k, pad).reshape(b, nblk + 2, BLOCK, hk, d)
    vp = jnp.pad(v, pad).reshape(b, nblk + 2, BLOCK, hk, d)
    kband = jnp.concatenate([kp[:, :-2], kp[:, 1:-1], kp[:, 2:]], axis=2)
    vband = jnp.concatenate([vp[:, :-2], vp[:, 1:-1], vp[:, 2:]], axis=2)
    qb = q.reshape(b, nblk, BLOCK, hk, g, d)
    kj = jnp.arange(3 * BLOCK)[None, :] - BLOCK
    rel = kj - jnp.arange(BLOCK)[:, None]
    kpos = jnp.arange(nblk)[:, None] * BLOCK + kj
    valid = (jnp.abs(rel) <= D_WINDOW)[None] & ((kpos >= 0) & (kpos < s))[:, None, :]
    bias = jnp.moveaxis(bias_tab[rel_bucket(rel)], -1, 0).reshape(hk, g, BLOCK, 3 * BLOCK).astype(jnp.float32)
    logits = jnp.einsum('bnqhgd,bnkhd->bnhgqk', qb, kband).astype(jnp.float32) * (d ** -0.5) + bias
    logits = jnp.where(valid[None, :, None, None], logits, NEG)
    sink = jnp.broadcast_to(sinks.astype(jnp.float32).reshape(hk, g)[None, None, :, :, None, None],
                            logits.shape[:-1] + (1,))
    p = jax.nn.softmax(jnp.concatenate([logits, sink], axis=-1), axis=-1)[..., :-1].astype(v.dtype)
    o = jnp.einsum('bnhgqk,bnkhd->bnqhgd', p, vband)
    return o.reshape(b, s, hk, g, d)


def hybrid_layer(x, g_attn, w_in, g_qa, g_kva, w_uq, w_ukv, g_qn, g_kn, sinks, w_out,
                 rel_bias, ang_t, ang_row, ang_col):
    b, s, _ = x.shape
    h = rmsnorm(x, g_attn)
    proj = h @ w_in
    pts = [int(p) for p in np.cumsum(IN_SPLITS)[:-1]]
    (a_cq, a_ckv, a_kpe, b_q, b_k, b_v, c_q, c_k, c_v, d_q, d_k, d_v, gate) = jnp.split(proj, pts, axis=-1)

    qa = (rmsnorm(a_cq, g_qa) @ w_uq).reshape(b, s, A_HEADS, A_NOPE + A_ROPE)
    q_pe = apply_rope(qa[..., A_NOPE:], ang_t)
    kva = (rmsnorm(a_ckv, g_kva) @ w_ukv).reshape(b, s, A_HEADS, A_NOPE + A_V)
    k_pe = apply_rope(a_kpe[:, :, None, :], ang_t)
    q_a = jnp.concatenate([qa[..., :A_NOPE], q_pe], axis=-1)[:, :, :, None, :]
    k_a = jnp.concatenate([kva[..., :A_NOPE], jnp.broadcast_to(k_pe, (b, s, A_HEADS, A_ROPE))], axis=-1)
    o_a = dense_attention(q_a, k_a, kva[..., A_NOPE:], (A_NOPE + A_ROPE) ** -0.5).reshape(b, s, W_A)

    npair = len(B_PAIRS)
    b_q = b_q.reshape(b, s, npair, B_HEADS_PER_PAIR, HEAD_DIM)
    b_k = b_k.reshape(b, s, npair, B_HEADS_PER_PAIR, HEAD_DIM)
    b_v = b_v.reshape(b, s, npair, B_HEADS_PER_PAIR, HEAD_DIM)
    outs, lses = [], []
    for j, (win, dil) in enumerate(B_PAIRS):
        o, lse = dilated_group(b_q[:, :, j], b_k[:, :, j], b_v[:, :, j], dil, win // (2 * dil),
                               rel_bias[:, j * B_HEADS_PER_PAIR:(j + 1) * B_HEADS_PER_PAIR])
        outs.append(o)
        lses.append(lse)
    alpha = jax.nn.softmax(jnp.stack(lses, axis=2), axis=2)
    o_b = (jnp.stack(outs, axis=2) * alpha[..., None].astype(x.dtype)).reshape(b, s, W_B)

    half = HEAD_DIM // 2
    qc = rmsnorm(c_q.reshape(b, s, C_HEADS, HEAD_DIM), g_qn)
    kc = rmsnorm(c_k.reshape(b, s, C_KV_HEADS, HEAD_DIM), g_kn)
    qc = jnp.concatenate([apply_rope(qc[..., :half], ang_row), apply_rope(qc[..., half:], ang_col)], axis=-1)
    kc = jnp.concatenate([apply_rope(kc[..., :half], ang_row), apply_rope(kc[..., half:], ang_col)], axis=-1)
    o_c = dense_attention(qc.reshape(b, s, C_KV_HEADS, C_HEADS // C_KV_HEADS, HEAD_DIM), kc,
                          c_v.reshape(b, s, C_KV_HEADS, HEAD_DIM), HEAD_DIM ** -0.5).reshape(b, s, W_C)

    o_d = window_attention(d_q.reshape(b, s, D_KV_HEADS, D_HEADS // D_KV_HEADS, HEAD_DIM),
                           d_k.reshape(b, s, D_KV_HEADS, HEAD_DIM),
                           d_v.reshape(b, s, D_KV_HEADS, HEAD_DIM),
                           sinks, rel_bias[:, B_HEADS:]).reshape(b, s, W_D)

    mixed = jnp.concatenate([o_a, o_b, o_c, o_d], axis=-1) * jax.nn.silu(gate)
    return x + mixed @ w_out


def setup_inputs(seed: int = 0) -> dict:
    key = jax.random.key(seed)
    ks = jax.random.split(key, 16)
    f32 = jnp.float32
    nrm = lambda k, shape, scale: jax.random.normal(k, shape, f32) * scale
    return {
        "x": nrm(ks[0], (BATCH, SEQ, D_MODEL), 1.0),
        "g_attn": 1.0 + nrm(ks[1], (DEPTH, D_MODEL), 0.02),
        "w_in": nrm(ks[2], (DEPTH, D_MODEL, IN_WIDTH), D_MODEL ** -0.5),
        "g_qa": 1.0 + nrm(ks[3], (DEPTH, A_Q_LORA), 0.02),
        "g_kva": 1.0 + nrm(ks[4], (DEPTH, A_KV_LORA), 0.02),
        "w_uq": nrm(ks[5], (DEPTH, A_Q_LORA, A_HEADS * (A_NOPE + A_ROPE)), A_Q_LORA ** -0.5),
        "w_ukv": nrm(ks[6], (DEPTH, A_KV_LORA, A_HEADS * (A_NOPE + A_V)), A_KV_LORA ** -0.5),
        "g_qn": 1.0 + nrm(ks[7], (DEPTH, HEAD_DIM), 0.02),
        "g_kn": 1.0 + nrm(ks[8], (DEPTH, HEAD_DIM), 0.02),
        "sinks": nrm(ks[9], (DEPTH, D_HEADS), 0.5),
        "w_out": nrm(ks[10], (DEPTH, MIX_WIDTH, D_MODEL), MIX_WIDTH ** -0.5),
        "rel_bias": nrm(ks[11], (N_BUCKETS, B_HEADS + D_HEADS), 0.1),
        "g_final": 1.0 + nrm(ks[12], (D_MODEL,), 0.02),
    }


def reference(x, g_attn, w_in, g_qa, g_kva, w_uq, w_ukv, g_qn, g_kn, sinks, w_out, rel_bias, g_final):
    s = x.shape[1]
    rows = s // GRID_W
    ang_t = rope_angles(jnp.arange(s), A_ROPE)
    ang_row = rope_angles(jnp.repeat(jnp.arange(rows), GRID_W), HEAD_DIM // 2)
    ang_col = rope_angles(jnp.tile(jnp.arange(GRID_W), rows), HEAD_DIM // 2)
    for l in range(DEPTH):
        x = hybrid_layer(x, g_attn[l], w_in[l], g_qa[l], g_kva[l], w_uq[l], w_ukv[l], g_qn[l], g_kn[l],
                         sinks[l], w_out[l], rel_bias, ang_t, ang_row, ang_col)
    return rmsnorm(x, g_final)
```

```python
import functools
import math

import numpy as np
import jax
import jax.numpy as jnp
from jax import lax
from jax.experimental import pallas as pl
from jax.experimental.pallas import tpu as pltpu

D_MODEL = 4096
HEAD_DIM = 128
A_HEADS = 8
A_Q_LORA = 768
A_KV_LORA = 512
A_NOPE = 128
A_ROPE = 64
A_V = 128
B_PAIRS = ((128, 1), (512, 4), (2048, 16))
B_HEADS_PER_PAIR = 4
B_HEADS = B_HEADS_PER_PAIR * len(B_PAIRS)
C_HEADS = 8
C_KV_HEADS = 2
D_HEADS = 8
D_KV_HEADS = 2
D_WINDOW = 128
GRID_W = 64
ROPE_THETA = 10000.0
N_BUCKETS = 32
REL_MAX_DIST = 1024
EPS = 1e-6
NEG = -1e30

W_A = A_HEADS * A_V
W_B = B_HEADS * HEAD_DIM
W_C = C_HEADS * HEAD_DIM
W_D = D_HEADS * HEAD_DIM
W_BG = B_HEADS_PER_PAIR * HEAD_DIM
MIX_WIDTH = W_A + W_B + W_C + W_D

GATE_OFF = 0
BQ_OFF = GATE_OFF + MIX_WIDTH
BK_OFF = BQ_OFF + W_B
BV_OFF = BK_OFF + W_B
CQ_OFF = BV_OFF + W_B
DQ_OFF = CQ_OFF + W_C
CK_OFF = DQ_OFF + W_D
CV_OFF = CK_OFF + C_KV_HEADS * HEAD_DIM
DK_OFF = CV_OFF + C_KV_HEADS * HEAD_DIM
DV_OFF = DK_OFF + D_KV_HEADS * HEAD_DIM
LAT_OFF = DV_OFF + D_KV_HEADS * HEAD_DIM
LAT_W = 1536
PROJ_W = LAT_OFF + LAT_W
A_QK = 256

VMEM_LIMIT = 56 * 1024 * 1024

F32 = jnp.float32
BF16 = jnp.bfloat16


def _params(sem, vmem=VMEM_LIMIT):
    return pltpu.CompilerParams(dimension_semantics=sem, vmem_limit_bytes=vmem)


def _rmsnorm_kernel(x_ref, g_ref, o_ref):
    x = x_ref[...]
    ms = jnp.mean(x * x, axis=-1, keepdims=True)
    o_ref[...] = (x * lax.rsqrt(ms + EPS) * g_ref[...]).astype(o_ref.dtype)


def _rmsnorm(x, g, out_dtype, tm=256):
    m, d = x.shape
    return pl.pallas_call(
        _rmsnorm_kernel,
        out_shape=jax.ShapeDtypeStruct((m, d), out_dtype),
        grid=(m // tm,),
        in_specs=[pl.BlockSpec((tm, d), lambda i: (i, 0)),
                  pl.BlockSpec((1, d), lambda i: (0, 0))],
        out_specs=pl.BlockSpec((tm, d), lambda i: (i, 0)),
        compiler_params=_params(("parallel",)),
        name="rmsnorm",
    )(x, g.reshape(1, d))


def _matmul_kernel(x_ref, w_ref, o_ref):
    o_ref[...] = jnp.dot(x_ref[...], w_ref[...], preferred_element_type=F32).astype(o_ref.dtype)


def _in_proj(h, w, tm=1024, tn=768):
    m, k = h.shape
    n = w.shape[1]
    return pl.pallas_call(
        _matmul_kernel,
        out_shape=jax.ShapeDtypeStruct((m, n), BF16),
        grid=(m // tm, n // tn),
        in_specs=[pl.BlockSpec((tm, k), lambda i, j: (i, 0)),
                  pl.BlockSpec((k, tn), lambda i, j: (0, j))],
        out_specs=pl.BlockSpec((tm, tn), lambda i, j: (i, j)),
        compiler_params=_params(("parallel", "parallel")),
        name="in_proj",
    )(h, w)


def _out_proj_kernel(m_ref, w_ref, x_ref, o_ref):
    o_ref[...] = x_ref[...] + jnp.dot(m_ref[...], w_ref[...], preferred_element_type=F32)


def _out_proj(mixed, w, x, tm=512, tn=1024):
    m, k = mixed.shape
    n = w.shape[1]
    return pl.pallas_call(
        _out_proj_kernel,
        out_shape=jax.ShapeDtypeStruct((m, n), F32),
        grid=(n // tn, m // tm),
        in_specs=[pl.BlockSpec((tm, k), lambda j, i: (i, 0)),
                  pl.BlockSpec((k, tn), lambda j, i: (0, j)),
                  pl.BlockSpec((tm, tn), lambda j, i: (i, j))],
        out_specs=pl.BlockSpec((tm, tn), lambda j, i: (i, j)),
        compiler_params=_params(("parallel", "parallel")),
        name="out_proj",
    )(mixed, w, x)


def _rope_tile(x, cos, sin_signed):
    lane = lax.broadcasted_iota(jnp.int32, x.shape, 1)
    partner = jnp.where((lane % 64) < 32, pltpu.roll(x, 96, 1), pltpu.roll(x, 32, 1))
    return x * cos + partner * sin_signed


def _rope_tables(seq):
    inv = ROPE_THETA ** (-jnp.arange(0, 64, 2, dtype=F32) / 64)
    pos = jnp.arange(seq)

    def ang(p):
        return p.astype(F32)[:, None] * inv[None, :]

    def halves(a):
        c, s = jnp.cos(a), jnp.sin(a)
        return jnp.concatenate([c, c], -1), jnp.concatenate([-s, s], -1)

    ct, st = halves(ang(pos))
    zeros = jnp.zeros_like(ct)
    cos_a = jnp.concatenate([ct, zeros], -1)
    sin_a = jnp.concatenate([st, zeros], -1)
    cr, sr = halves(ang(pos // GRID_W))
    cc, sc = halves(ang(pos % GRID_W))
    cos_c = jnp.concatenate([cr, cc], -1)
    sin_c = jnp.concatenate([sr, sc], -1)
    return cos_a, sin_a, cos_c, sin_c


def _prep_a_kernel(lat_ref, gq_ref, gkv_ref, wuq_ref, wukv_ref, cos_ref, sin_ref, q_ref, k_ref, v_ref):
    lat = lat_ref[...].astype(F32)
    ckv = lat[:, 0:A_KV_LORA]
    cq = lat[:, A_KV_LORA:A_KV_LORA + A_Q_LORA]
    kpe = lat[:, A_KV_LORA + A_Q_LORA:A_KV_LORA + A_Q_LORA + 128]

    def rms(x, g):
        return x * lax.rsqrt(jnp.mean(x * x, axis=-1, keepdims=True) + EPS) * g

    qa = jnp.dot(rms(cq, gq_ref[...]).astype(BF16), wuq_ref[...], preferred_element_type=F32)
    kva = jnp.dot(rms(ckv, gkv_ref[...]).astype(BF16), wukv_ref[...], preferred_element_type=F32)
    cos = cos_ref[...]
    sin = sin_ref[...]
    scale = (A_NOPE + A_ROPE) ** -0.5
    kpe_r = _rope_tile(kpe, cos, sin).astype(BF16)
    for h in range(A_HEADS):
        c0 = h * A_QK
        q_ref[:, c0:c0 + 128] = (qa[:, c0:c0 + 128] * scale).astype(BF16)
        q_ref[:, c0 + 128:c0 + 256] = (_rope_tile(qa[:, c0 + 128:c0 + 256], cos, sin) * scale).astype(BF16)
        k_ref[:, c0:c0 + 128] = kva[:, c0:c0 + 128].astype(BF16)
        k_ref[:, c0 + 128:c0 + 256] = kpe_r
        v_ref[:, h * A_V:(h + 1) * A_V] = kva[:, c0 + 128:c0 + 256].astype(BF16)


def _prep_a(proj, g_qa, g_kva, w_uq_p, w_ukv, cos_a, sin_a, seq, tm=512):
    m = proj.shape[0]
    nsb = seq // tm
    return pl.pallas_call(
        _prep_a_kernel,
        out_shape=(jax.ShapeDtypeStruct((m, A_HEADS * A_QK), BF16),
                   jax.ShapeDtypeStruct((m, A_HEADS * A_QK), BF16),
                   jax.ShapeDtypeStruct((m, W_A), BF16)),
        grid=(m // tm,),
        in_specs=[pl.BlockSpec((tm, LAT_W), lambda i: (i, LAT_OFF // LAT_W)),
                  pl.BlockSpec((1, A_Q_LORA), lambda i: (0, 0)),
                  pl.BlockSpec((1, A_KV_LORA), lambda i: (0, 0)),
                  pl.BlockSpec((A_Q_LORA, A_HEADS * A_QK), lambda i: (0, 0)),
                  pl.BlockSpec((A_KV_LORA, A_HEADS * A_QK), lambda i: (0, 0)),
                  pl.BlockSpec((tm, 128), lambda i: (i % nsb, 0)),
                  pl.BlockSpec((tm, 128), lambda i: (i % nsb, 0))],
        out_specs=(pl.BlockSpec((tm, A_HEADS * A_QK), lambda i: (i, 0)),
                   pl.BlockSpec((tm, A_HEADS * A_QK), lambda i: (i, 0)),
                   pl.BlockSpec((tm, W_A), lambda i: (i, 0))),
        compiler_params=_params(("parallel",)),
        name="prep_a",
    )(proj, g_qa.reshape(1, -1), g_kva.reshape(1, -1), w_uq_p, w_ukv, cos_a, sin_a)


def _prep_c_kernel(q_in_ref, k_in_ref, gq_ref, gk_ref, cos_ref, sin_ref, q_ref, k_ref):
    cos = cos_ref[...]
    sin = sin_ref[...]

    def norm_rope(x, g):
        x = x.astype(F32)
        xn = x * lax.rsqrt(jnp.mean(x * x, axis=-1, keepdims=True) + EPS) * g
        return _rope_tile(xn, cos, sin)

    scale = HEAD_DIM ** -0.5
    for h in range(C_HEADS):
        sl = slice(h * HEAD_DIM, (h + 1) * HEAD_DIM)
        q_ref[:, sl] = (norm_rope(q_in_ref[:, sl], gq_ref[...]) * scale).astype(BF16)
    for h in range(C_KV_HEADS):
        sl = slice(h * HEAD_DIM, (h + 1) * HEAD_DIM)
        k_ref[:, sl] = norm_rope(k_in_ref[:, sl], gk_ref[...]).astype(BF16)


def _prep_c(proj, g_qn, g_kn, cos_c, sin_c, seq, tm=512):
    m = proj.shape[0]
    nsb = seq // tm
    kw = C_KV_HEADS * HEAD_DIM
    return pl.pallas_call(
        _prep_c_kernel,
        out_shape=(jax.ShapeDtypeStruct((m, W_C), BF16),
                   jax.ShapeDtypeStruct((m, kw), BF16)),
        grid=(m // tm,),
        in_specs=[pl.BlockSpec((tm, W_C), lambda i: (i, CQ_OFF // W_C)),
                  pl.BlockSpec((tm, kw), lambda i: (i, CK_OFF // kw)),
                  pl.BlockSpec((1, HEAD_DIM), lambda i: (0, 0)),
                  pl.BlockSpec((1, HEAD_DIM), lambda i: (0, 0)),
                  pl.BlockSpec((tm, 128), lambda i: (i % nsb, 0)),
                  pl.BlockSpec((tm, 128), lambda i: (i % nsb, 0))],
        out_specs=(pl.BlockSpec((tm, W_C), lambda i: (i, 0)),
                   pl.BlockSpec((tm, kw), lambda i: (i, 0))),
        compiler_params=_params(("parallel",)),
        name="prep_c",
    )(proj, proj, g_qn.reshape(1, -1), g_kn.reshape(1, -1), cos_c, sin_c)


def _dense_attn_kernel(q_ref, k_ref, v_ref, o_ref, *, groups, dq, dv):
    k = k_ref[0]
    v = v_ref[0]
    for g in range(groups):
        q = q_ref[0, :, g * dq:(g + 1) * dq]
        s = lax.dot_general(q, k, (((1,), (1,)), ((), ())), preferred_element_type=F32)
        m = jnp.max(s, axis=-1, keepdims=True)
        p = jnp.exp(s - m)
        l = jnp.sum(p, axis=-1, keepdims=True)
        o = jnp.dot(p.astype(BF16), v, preferred_element_type=F32)
        o_ref[0, :, g * dv:(g + 1) * dv] = (o / l).astype(o_ref.dtype)


def _dense_attn(q, k, v, *, kv_heads, groups, dq, dv, k_col0, v_col0, tq, name):
    b, s, _ = q.shape
    kern = functools.partial(_dense_attn_kernel, groups=groups, dq=dq, dv=dv)
    return pl.pallas_call(
        kern,
        out_shape=jax.ShapeDtypeStruct((b, s, kv_heads * groups * dv), BF16),
        grid=(b, kv_heads, s // tq),
        in_specs=[pl.BlockSpec((1, tq, groups * dq), lambda bi, hi, qi: (bi, qi, hi)),
                  pl.BlockSpec((1, s, dq), lambda bi, hi, qi: (bi, 0, k_col0 + hi)),
                  pl.BlockSpec((1, s, dv), lambda bi, hi, qi: (bi, 0, v_col0 + hi))],
        out_specs=pl.BlockSpec((1, tq, groups * dv), lambda bi, hi, qi: (bi, qi, hi)),
        compiler_params=_params(("parallel", "parallel", "parallel")),
        name=name,
    )(q, k, v)


def _rel_bucket_np(rel):
    nb = N_BUCKETS // 2
    max_exact = nb // 2
    ret = np.where(rel > 0, nb, 0)
    n = np.abs(rel)
    nf = np.maximum(n, 1).astype(np.float32)
    large = max_exact + (np.log(nf / max_exact) / math.log(REL_MAX_DIST / max_exact)
                         * (nb - max_exact)).astype(np.int32)
    large = np.minimum(large, nb - 1)
    return ret + np.where(n < max_exact, n, large)


def _band_bias(tab, dil, half, tq):
    tw = tq + 2 * half
    qi = np.arange(tq)[:, None]
    ci = np.arange(tw)[None, :]
    tiles = []
    for delta in (0, -half, -2 * half):
        rel = delta + ci - qi
        inside = np.abs(rel) <= half
        bucket = _rel_bucket_np(dil * rel)
        vals = jnp.moveaxis(tab[bucket], -1, 0).astype(F32)
        tiles.append(jnp.where(inside[None], vals, NEG))
    return jnp.stack(tiles, 0)


def _banded_kernel(*refs, n_q, kv_of, tq, tw, half, length, scale, with_sink, with_lse):
    q_ref, k_ref, v_ref, bias_ref = refs[:4]
    pos = 4
    sink_ref = None
    if with_sink:
        sink_ref = refs[pos]
        pos += 1
    o_ref = refs[pos]
    lse_ref = refs[pos + 1] if with_lse else None

    i0 = pl.program_id(2) * tq
    ks = pl.multiple_of(jnp.clip(i0 - half, 0, length - tw), 64)
    for h in range(n_q):
        kvh = kv_of(h)
        q = q_ref[0, :, h * HEAD_DIM:(h + 1) * HEAD_DIM]
        kw = k_ref[0, pl.ds(ks, tw), kvh * HEAD_DIM:(kvh + 1) * HEAD_DIM]
        vw = v_ref[0, pl.ds(ks, tw), kvh * HEAD_DIM:(kvh + 1) * HEAD_DIM]
        s = lax.dot_general(q, kw, (((1,), (1,)), ((), ())), preferred_element_type=F32) * scale
        s = s + bias_ref[0, h]
        m = jnp.max(s, axis=-1, keepdims=True)
        if with_sink:
            sink = sink_ref[0, h:h + 1, 0:1]
            m = jnp.maximum(m, sink)
        p = jnp.exp(s - m)
        l = jnp.sum(p, axis=-1, keepdims=True)
        if with_sink:
            l = l + jnp.exp(sink - m)
        o = jnp.dot(p.astype(BF16), vw, preferred_element_type=F32)
        o_ref[0, :, h * HEAD_DIM:(h + 1) * HEAD_DIM] = (o / l).astype(o_ref.dtype)
        if with_lse:
            lse_ref[0, :, h * HEAD_DIM:(h + 1) * HEAD_DIM] = jnp.broadcast_to(m + jnp.log(l), (tq, HEAD_DIM))


def _variant(i, nblk):
    return jnp.where(i == 0, 0, jnp.where(i == nblk - 1, 2, 1))


def _dilated_group(proj3, bias, j, dil, tq=128):
    b, s, w = proj3.shape
    length = s // dil
    half = B_PAIRS[j][0] // (2 * dil)
    tw = tq + 2 * half
    nblk = length // tq
    view = proj3.reshape(b, length, dil * w)
    wb = w // W_BG
    qc, kc, vc = BQ_OFF // W_BG + j, BK_OFF // W_BG + j, BV_OFF // W_BG + j
    kern = functools.partial(_banded_kernel, n_q=B_HEADS_PER_PAIR, kv_of=lambda h: h, tq=tq, tw=tw, half=half,
                             length=length, scale=HEAD_DIM ** -0.5, with_sink=False, with_lse=True)
    o, lse = pl.pallas_call(
        kern,
        out_shape=(jax.ShapeDtypeStruct((b, length, dil * W_BG), BF16),
                   jax.ShapeDtypeStruct((b, length, dil * W_BG), F32)),
        grid=(b, dil, nblk),
        in_specs=[pl.BlockSpec((1, tq, W_BG), lambda bi, r, i: (bi, i, r * wb + qc)),
                  pl.BlockSpec((1, length, W_BG), lambda bi, r, i: (bi, 0, r * wb + kc)),
                  pl.BlockSpec((1, length, W_BG), lambda bi, r, i: (bi, 0, r * wb + vc)),
                  pl.BlockSpec((1, B_HEADS_PER_PAIR, tq, tw), lambda bi, r, i: (_variant(i, nblk), 0, 0, 0))],
        out_specs=(pl.BlockSpec((1, tq, W_BG), lambda bi, r, i: (bi, i, r)),
                   pl.BlockSpec((1, tq, W_BG), lambda bi, r, i: (bi, i, r))),
        compiler_params=_params(("parallel", "parallel", "parallel")),
        name=f"dilated_attn_{dil}",
    )(view, view, view, bias)
    return o.reshape(b, s, W_BG), lse.reshape(b, s, W_BG)


def _window_attn(proj3, bias, sink_rows, tq=128):
    b, s, w = proj3.shape
    half = D_WINDOW
    tw = tq + 2 * half
    nblk = s // tq
    g = D_HEADS // D_KV_HEADS
    kern = functools.partial(_banded_kernel, n_q=g, kv_of=lambda h: 0, tq=tq, tw=tw, half=half,
                             length=s, scale=HEAD_DIM ** -0.5, with_sink=True, with_lse=False)
    gw = g * HEAD_DIM
    return pl.pallas_call(
        kern,
        out_shape=jax.ShapeDtypeStruct((b, s, W_D), BF16),
        grid=(b, D_KV_HEADS, nblk),
        in_specs=[pl.BlockSpec((1, tq, gw), lambda bi, hi, i: (bi, i, DQ_OFF // gw + hi)),
                  pl.BlockSpec((1, s, HEAD_DIM), lambda bi, hi, i: (bi, 0, DK_OFF // HEAD_DIM + hi)),
                  pl.BlockSpec((1, s, HEAD_DIM), lambda bi, hi, i: (bi, 0, DV_OFF // HEAD_DIM + hi)),
                  pl.BlockSpec((1, g, tq, tw), lambda bi, hi, i: (_variant(i, nblk), hi, 0, 0)),
                  pl.BlockSpec((1, g, 128), lambda bi, hi, i: (hi, 0, 0))],
        out_specs=pl.BlockSpec((1, tq, gw), lambda bi, hi, i: (bi, i, hi)),
        compiler_params=_params(("parallel", "parallel", "parallel")),
        name="window_attn",
    )(proj3, proj3, proj3, bias, sink_rows)


def _mix_kernel(gate_ref, oa_ref, ob0_ref, ob1_ref, ob2_ref, l0_ref, l1_ref, l2_ref, oc_ref, od_ref, out_ref):
    def silu(col0, width):
        g = gate_ref[:, col0:col0 + width].astype(F32)
        return g * (1.0 / (1.0 + jnp.exp(-g)))

    out_ref[:, 0:W_A] = (oa_ref[...].astype(F32) * silu(0, W_A)).astype(BF16)
    l0, l1, l2 = l0_ref[...], l1_ref[...], l2_ref[...]
    mx = jnp.maximum(jnp.maximum(l0, l1), l2)
    e0, e1, e2 = jnp.exp(l0 - mx), jnp.exp(l1 - mx), jnp.exp(l2 - mx)
    inv = 1.0 / (e0 + e1 + e2)
    for j, (ob_ref, e) in enumerate(((ob0_ref, e0), (ob1_ref, e1), (ob2_ref, e2))):
        c0 = W_A + j * W_BG
        out_ref[:, c0:c0 + W_BG] = (ob_ref[...].astype(F32) * (e * inv) * silu(c0, W_BG)).astype(BF16)
    c0 = W_A + W_B
    out_ref[:, c0:c0 + W_C] = (oc_ref[...].astype(F32) * silu(c0, W_C)).astype(BF16)
    c0 = W_A + W_B + W_C
    out_ref[:, c0:c0 + W_D] = (od_ref[...].astype(F32) * silu(c0, W_D)).astype(BF16)


def _mix(proj, o_a, o_b, lse_b, o_c, o_d, tm=512):
    m = proj.shape[0]
    row = lambda w: pl.BlockSpec((tm, w), lambda i: (i, 0))
    return pl.pallas_call(
        _mix_kernel,
        out_shape=jax.ShapeDtypeStruct((m, MIX_WIDTH), BF16),
        grid=(m // tm,),
        in_specs=[pl.BlockSpec((tm, MIX_WIDTH), lambda i: (i, GATE_OFF // MIX_WIDTH)),
                  row(W_A), row(W_BG), row(W_BG), row(W_BG), row(W_BG), row(W_BG), row(W_BG), row(W_C), row(W_D)],
        out_specs=row(MIX_WIDTH),
        compiler_params=_params(("parallel",)),
        name="gate_mix",
    )(proj, o_a, *o_b, *lse_b, o_c, o_d)


def _split_w_in(w):
    sizes = (A_Q_LORA, A_KV_LORA, A_ROPE, W_B, W_B, W_B, W_C, C_KV_HEADS * HEAD_DIM, C_KV_HEADS * HEAD_DIM,
             W_D, D_KV_HEADS * HEAD_DIM, D_KV_HEADS * HEAD_DIM, MIX_WIDTH)
    pts = np.cumsum(sizes)[:-1]
    return jnp.split(w, [int(p) for p in pts], axis=1)


def _layout_w_in(w):
    (a_cq, a_ckv, a_kpe, b_q, b_k, b_v, c_q, c_k, c_v, d_q, d_k, d_v, gate) = _split_w_in(w)
    pad = jnp.zeros((w.shape[0], LAT_W - A_KV_LORA - A_Q_LORA - A_ROPE), w.dtype)
    out = jnp.concatenate([gate, b_q, b_k, b_v, c_q, d_q, c_k, c_v, d_k, d_v, a_ckv, a_cq, a_kpe, pad], axis=1)
    return out.astype(BF16)


def _layout_w_uq(w):
    w = w.reshape(A_Q_LORA, A_HEADS, A_NOPE + A_ROPE)
    w = jnp.pad(w, ((0, 0), (0, 0), (0, A_QK - A_NOPE - A_ROPE)))
    return w.reshape(A_Q_LORA, A_HEADS * A_QK).astype(BF16)


def _layer(x2, b, s, g_attn, w_in, g_qa, g_kva, w_uq, w_ukv, g_qn, g_kn, sinks, w_out, rel_bias, tables):
    cos_a, sin_a, cos_c, sin_c = tables
    h = _rmsnorm(x2, g_attn, BF16)
    proj = _in_proj(h, _layout_w_in(w_in))
    proj3 = proj.reshape(b, s, PROJ_W)

    q_a, k_a, v_a = _prep_a(proj, g_qa, g_kva, _layout_w_uq(w_uq), w_ukv.astype(BF16), cos_a, sin_a, s)
    o_a = _dense_attn(q_a.reshape(b, s, -1), k_a.reshape(b, s, -1), v_a.reshape(b, s, -1),
                      kv_heads=A_HEADS, groups=1, dq=A_QK, dv=A_V, k_col0=0, v_col0=0, tq=512, name="attn_a")

    o_b, lse_b = [], []
    for j, (win, dil) in enumerate(B_PAIRS):
        tab = rel_bias[:, j * B_HEADS_PER_PAIR:(j + 1) * B_HEADS_PER_PAIR]
        bias = _band_bias(tab, dil, win // (2 * dil), 128)
        o, lse = _dilated_group(proj3, bias, j, dil)
        o_b.append(o.reshape(b * s, W_BG))
        lse_b.append(lse.reshape(b * s, W_BG))

    q_c, k_c = _prep_c(proj, g_qn, g_kn, cos_c, sin_c, s)
    o_c = _dense_attn(q_c.reshape(b, s, -1), k_c.reshape(b, s, -1), proj3,
                      kv_heads=C_KV_HEADS, groups=C_HEADS // C_KV_HEADS, dq=HEAD_DIM, dv=HEAD_DIM,
                      k_col0=0, v_col0=CV_OFF // HEAD_DIM, tq=256, name="attn_c")

    bias_d = _band_bias(rel_bias[:, B_HEADS:], 1, D_WINDOW, 128)
    sink_rows = jnp.broadcast_to(sinks.astype(F32).reshape(D_KV_HEADS, D_HEADS // D_KV_HEADS, 1),
                                 (D_KV_HEADS, D_HEADS // D_KV_HEADS, 128))
    o_d = _window_attn(proj3, bias_d, sink_rows)

    mixed = _mix(proj, o_a.reshape(b * s, W_A), o_b, lse_b, o_c.reshape(b * s, W_C), o_d.reshape(b * s, W_D))
    return _out_proj(mixed, w_out.astype(BF16), x2)


def kernel(x, g_attn, w_in, g_qa, g_kva, w_uq, w_ukv, g_qn, g_kn, sinks, w_out, rel_bias, g_final):
    b, s, d = x.shape
    depth = w_in.shape[0]
    tables = _rope_tables(s)
    x2 = x.reshape(b * s, d)
    for l in range(depth):
        x2 = _layer(x2, b, s, g_attn[l], w_in[l], g_qa[l], g_kva[l], w_uq[l], w_ukv[l], g_qn[l], g_kn[l],
                    sinks[l], w_out[l], rel_bias, tables)
    return _rmsnorm(x2, g_final, F32).reshape(b, s, d)
```

```python
import functools
import math

import numpy as np
import jax
import jax.numpy as jnp
from jax import lax
from jax.experimental import pallas as pl
from jax.experimental.pallas import tpu as pltpu

D_MODEL = 4096
HEAD_DIM = 128
A_HEADS = 8
A_Q_LORA = 768
A_KV_LORA = 512
A_NOPE = 128
A_ROPE = 64
A_V = 128
B_PAIRS = ((128, 1), (512, 4), (2048, 16))
B_HEADS_PER_PAIR = 4
B_HEADS = B_HEADS_PER_PAIR * len(B_PAIRS)
C_HEADS = 8
C_KV_HEADS = 2
D_HEADS = 8
D_KV_HEADS = 2
D_WINDOW = 128
GRID_W = 64
ROPE_THETA = 10000.0
N_BUCKETS = 32
REL_MAX_DIST = 1024
EPS = 1e-6
NEG = -1e30

W_A = A_HEADS * A_V
W_B = B_HEADS * HEAD_DIM
W_C = C_HEADS * HEAD_DIM
W_D = D_HEADS * HEAD_DIM
W_BG = B_HEADS_PER_PAIR * HEAD_DIM
MIX_WIDTH = W_A + W_B + W_C + W_D

GATE_OFF = 0
B_OFF = GATE_OFF + MIX_WIDTH
B_GW = 3 * W_BG
CQ_OFF = B_OFF + 3 * B_GW
DQ_OFF = CQ_OFF + W_C
CK_OFF = DQ_OFF + W_D
CV_OFF = CK_OFF + C_KV_HEADS * HEAD_DIM
DK_OFF = CV_OFF + C_KV_HEADS * HEAD_DIM
DV_OFF = DK_OFF + D_KV_HEADS * HEAD_DIM
LAT_OFF = DV_OFF + D_KV_HEADS * HEAD_DIM
LAT_W = 1536
PROJ_W = LAT_OFF + LAT_W
A_QK = 256
LSE_LANES = 32

VMEM_LIMIT = 56 * 1024 * 1024

F32 = jnp.float32
BF16 = jnp.bfloat16


def _params(sem, vmem=VMEM_LIMIT):
    return pltpu.CompilerParams(dimension_semantics=sem, vmem_limit_bytes=vmem)


def _rmsnorm_kernel(x_ref, g_ref, o_ref):
    x = x_ref[...]
    ms = jnp.mean(x * x, axis=-1, keepdims=True)
    o_ref[...] = (x * lax.rsqrt(ms + EPS) * g_ref[...]).astype(o_ref.dtype)


def _rmsnorm(x, g, out_dtype, tm=256):
    m, d = x.shape
    return pl.pallas_call(
        _rmsnorm_kernel,
        out_shape=jax.ShapeDtypeStruct((m, d), out_dtype),
        grid=(m // tm,),
        in_specs=[pl.BlockSpec((tm, d), lambda i: (i, 0)),
                  pl.BlockSpec((1, d), lambda i: (0, 0))],
        out_specs=pl.BlockSpec((tm, d), lambda i: (i, 0)),
        compiler_params=_params(("parallel",)),
        name="rmsnorm",
    )(x, g.reshape(1, d))


def _matmul_kernel(x_ref, w_ref, o_ref):
    o_ref[...] = jnp.dot(x_ref[...], w_ref[...], preferred_element_type=F32).astype(o_ref.dtype)


def _in_proj(h, w, tm=1024, tn=768):
    m, k = h.shape
    n = w.shape[1]
    return pl.pallas_call(
        _matmul_kernel,
        out_shape=jax.ShapeDtypeStruct((m, n), BF16),
        grid=(m // tm, n // tn),
        in_specs=[pl.BlockSpec((tm, k), lambda i, j: (i, 0)),
                  pl.BlockSpec((k, tn), lambda i, j: (0, j))],
        out_specs=pl.BlockSpec((tm, tn), lambda i, j: (i, j)),
        compiler_params=_params(("parallel", "parallel")),
        name="in_proj",
    )(h, w)


def _out_proj_kernel(m_ref, w_ref, x_ref, o_ref):
    o_ref[...] = x_ref[...] + jnp.dot(m_ref[...], w_ref[...], preferred_element_type=F32)


def _out_proj(mixed, w, x, tm=512, tn=1024):
    m, k = mixed.shape
    n = w.shape[1]
    return pl.pallas_call(
        _out_proj_kernel,
        out_shape=jax.ShapeDtypeStruct((m, n), F32),
        grid=(n // tn, m // tm),
        in_specs=[pl.BlockSpec((tm, k), lambda j, i: (i, 0)),
                  pl.BlockSpec((k, tn), lambda j, i: (0, j)),
                  pl.BlockSpec((tm, tn), lambda j, i: (i, j))],
        out_specs=pl.BlockSpec((tm, tn), lambda j, i: (i, j)),
        compiler_params=_params(("parallel", "parallel")),
        name="out_proj",
    )(mixed, w, x)


def _rope_tile(x, cos, sin_signed):
    lane = lax.broadcasted_iota(jnp.int32, x.shape, 1)
    partner = jnp.where((lane % 64) < 32, pltpu.roll(x, 96, 1), pltpu.roll(x, 32, 1))
    return x * cos + partner * sin_signed


def _rope_tables(seq):
    inv = ROPE_THETA ** (-jnp.arange(0, 64, 2, dtype=F32) / 64)
    pos = jnp.arange(seq)

    def ang(p):
        return p.astype(F32)[:, None] * inv[None, :]

    def halves(a):
        c, s = jnp.cos(a), jnp.sin(a)
        return jnp.concatenate([c, c], -1), jnp.concatenate([-s, s], -1)

    ct, st = halves(ang(pos))
    zeros = jnp.zeros_like(ct)
    cos_a = jnp.concatenate([ct, zeros], -1)
    sin_a = jnp.concatenate([st, zeros], -1)
    cr, sr = halves(ang(pos // GRID_W))
    cc, sc = halves(ang(pos % GRID_W))
    cos_c = jnp.concatenate([cr, cc], -1)
    sin_c = jnp.concatenate([sr, sc], -1)
    return cos_a, sin_a, cos_c, sin_c


def _prep_a_kernel(lat_ref, gq_ref, gkv_ref, wuq_ref, wukv_ref, cos_ref, sin_ref, q_ref, k_ref, v_ref):
    lat = lat_ref[...].astype(F32)
    ckv = lat[:, 0:A_KV_LORA]
    cq = lat[:, A_KV_LORA:A_KV_LORA + A_Q_LORA]
    kpe = lat[:, A_KV_LORA + A_Q_LORA:A_KV_LORA + A_Q_LORA + 128]

    def rms(x, g):
        return x * lax.rsqrt(jnp.mean(x * x, axis=-1, keepdims=True) + EPS) * g

    qa = jnp.dot(rms(cq, gq_ref[...]).astype(BF16), wuq_ref[...], preferred_element_type=F32)
    kva = jnp.dot(rms(ckv, gkv_ref[...]).astype(BF16), wukv_ref[...], preferred_element_type=F32)
    cos = cos_ref[...]
    sin = sin_ref[...]
    scale = (A_NOPE + A_ROPE) ** -0.5
    kpe_r = _rope_tile(kpe, cos, sin).astype(BF16)
    for h in range(A_HEADS):
        c0 = h * A_QK
        q_ref[:, c0:c0 + 128] = (qa[:, c0:c0 + 128] * scale).astype(BF16)
        q_ref[:, c0 + 128:c0 + 256] = (_rope_tile(qa[:, c0 + 128:c0 + 256], cos, sin) * scale).astype(BF16)
        k_ref[:, c0:c0 + 128] = kva[:, c0:c0 + 128].astype(BF16)
        k_ref[:, c0 + 128:c0 + 256] = kpe_r
        v_ref[:, h * A_V:(h + 1) * A_V] = kva[:, c0 + 128:c0 + 256].astype(BF16)


def _prep_a(proj, g_qa, g_kva, w_uq_p, w_ukv, cos_a, sin_a, seq, tm=512):
    m = proj.shape[0]
    nsb = seq // tm
    return pl.pallas_call(
        _prep_a_kernel,
        out_shape=(jax.ShapeDtypeStruct((m, A_HEADS * A_QK), BF16),
                   jax.ShapeDtypeStruct((m, A_HEADS * A_QK), BF16),
                   jax.ShapeDtypeStruct((m, W_A), BF16)),
        grid=(m // tm,),
        in_specs=[pl.BlockSpec((tm, LAT_W), lambda i: (i, LAT_OFF // LAT_W)),
                  pl.BlockSpec((1, A_Q_LORA), lambda i: (0, 0)),
                  pl.BlockSpec((1, A_KV_LORA), lambda i: (0, 0)),
                  pl.BlockSpec((A_Q_LORA, A_HEADS * A_QK), lambda i: (0, 0)),
                  pl.BlockSpec((A_KV_LORA, A_HEADS * A_QK), lambda i: (0, 0)),
                  pl.BlockSpec((tm, 128), lambda i: (i % nsb, 0)),
                  pl.BlockSpec((tm, 128), lambda i: (i % nsb, 0))],
        out_specs=(pl.BlockSpec((tm, A_HEADS * A_QK), lambda i: (i, 0)),
                   pl.BlockSpec((tm, A_HEADS * A_QK), lambda i: (i, 0)),
                   pl.BlockSpec((tm, W_A), lambda i: (i, 0))),
        compiler_params=_params(("parallel",)),
        name="prep_a",
    )(proj, g_qa.reshape(1, -1), g_kva.reshape(1, -1), w_uq_p, w_ukv, cos_a, sin_a)


def _prep_c_kernel(q_in_ref, k_in_ref, gq_ref, gk_ref, cos_ref, sin_ref, q_ref, k_ref):
    cos = cos_ref[...]
    sin = sin_ref[...]

    def norm_rope(x, g):
        x = x.astype(F32)
        xn = x * lax.rsqrt(jnp.mean(x * x, axis=-1, keepdims=True) + EPS) * g
        return _rope_tile(xn, cos, sin)

    scale = HEAD_DIM ** -0.5
    for h in range(C_HEADS):
        sl = slice(h * HEAD_DIM, (h + 1) * HEAD_DIM)
        q_ref[:, sl] = (norm_rope(q_in_ref[:, sl], gq_ref[...]) * scale).astype(BF16)
    for h in range(C_KV_HEADS):
        sl = slice(h * HEAD_DIM, (h + 1) * HEAD_DIM)
        k_ref[:, sl] = norm_rope(k_in_ref[:, sl], gk_ref[...]).astype(BF16)


def _prep_c(proj, g_qn, g_kn, cos_c, sin_c, seq, tm=512):
    m = proj.shape[0]
    nsb = seq // tm
    kw = C_KV_HEADS * HEAD_DIM
    return pl.pallas_call(
        _prep_c_kernel,
        out_shape=(jax.ShapeDtypeStruct((m, W_C), BF16),
                   jax.ShapeDtypeStruct((m, kw), BF16)),
        grid=(m // tm,),
        in_specs=[pl.BlockSpec((tm, W_C), lambda i: (i, CQ_OFF // W_C)),
                  pl.BlockSpec((tm, kw), lambda i: (i, CK_OFF // kw)),
                  pl.BlockSpec((1, HEAD_DIM), lambda i: (0, 0)),
                  pl.BlockSpec((1, HEAD_DIM), lambda i: (0, 0)),
                  pl.BlockSpec((tm, 128), lambda i: (i % nsb, 0)),
                  pl.BlockSpec((tm, 128), lambda i: (i % nsb, 0))],
        out_specs=(pl.BlockSpec((tm, W_C), lambda i: (i, 0)),
                   pl.BlockSpec((tm, kw), lambda i: (i, 0))),
        compiler_params=_params(("parallel",)),
        name="prep_c",
    )(proj, proj, g_qn.reshape(1, -1), g_kn.reshape(1, -1), cos_c, sin_c)


def _dense_attn_kernel(q_ref, k_ref, v_ref, o_ref, *, groups, dq, dv):
    k = k_ref[0]
    v = v_ref[0]
    for g in range(groups):
        q = q_ref[0, :, g * dq:(g + 1) * dq]
        s = lax.dot_general(q, k, (((1,), (1,)), ((), ())), preferred_element_type=F32)
        m = jnp.max(s, axis=-1, keepdims=True)
        p = jnp.exp(s - m)
        l = jnp.sum(p, axis=-1, keepdims=True)
        o = jnp.dot(p.astype(BF16), v, preferred_element_type=F32)
        o_ref[0, :, g * dv:(g + 1) * dv] = (o / l).astype(o_ref.dtype)


def _dense_attn(q, k, v, *, kv_heads, groups, dq, dv, k_col0, v_col0, tq, name):
    b, s, _ = q.shape
    kern = functools.partial(_dense_attn_kernel, groups=groups, dq=dq, dv=dv)
    return pl.pallas_call(
        kern,
        out_shape=jax.ShapeDtypeStruct((b, s, kv_heads * groups * dv), BF16),
        grid=(b, kv_heads, s // tq),
        in_specs=[pl.BlockSpec((1, tq, groups * dq), lambda bi, hi, qi: (bi, qi, hi)),
                  pl.BlockSpec((1, s, dq), lambda bi, hi, qi: (bi, 0, k_col0 + hi)),
                  pl.BlockSpec((1, s, dv), lambda bi, hi, qi: (bi, 0, v_col0 + hi))],
        out_specs=pl.BlockSpec((1, tq, groups * dv), lambda bi, hi, qi: (bi, qi, hi)),
        compiler_params=_params(("parallel", "parallel", "parallel")),
        name=name,
    )(q, k, v)


def _rel_bucket_np(rel):
    nb = N_BUCKETS // 2
    max_exact = nb // 2
    ret = np.where(rel > 0, nb, 0)
    n = np.abs(rel)
    nf = np.maximum(n, 1).astype(np.float32)
    large = max_exact + (np.log(nf / max_exact) / math.log(REL_MAX_DIST / max_exact)
                         * (nb - max_exact)).astype(np.int32)
    large = np.minimum(large, nb - 1)
    return ret + np.where(n < max_exact, n, large)


def _band_bias(tab, dil, half, tq):
    tw = tq + 2 * half
    n = tq + tw - 1
    heads = tab.shape[1]
    tiles = []
    for delta in (0, -half, -2 * half):
        rel = delta + np.arange(n) - (tq - 1)
        inside = np.abs(rel) <= half
        vals = tab[_rel_bucket_np(dil * rel)].astype(F32)
        diag = jnp.where(inside[:, None], vals, NEG).T
        flat = jnp.tile(diag, (1, tq + 1))[:, :tq * (n + 1)]
        tiles.append(flat.reshape(heads, tq, n + 1)[:, ::-1, :tw])
    return jnp.stack(tiles, 0)


def _banded_kernel(*refs, n_q, kv_of, tq, tw, half, length, scale, with_sink, with_lse):
    q_ref, k_ref, v_ref, bias_ref = refs[:4]
    pos = 4
    sink_ref = None
    if with_sink:
        sink_ref = refs[pos]
        pos += 1
    o_ref = refs[pos]
    lse_ref = refs[pos + 1] if with_lse else None

    i0 = pl.program_id(2) * tq
    ks = pl.multiple_of(jnp.clip(i0 - half, 0, length - tw), 64)
    lse_parts = []
    for h in range(n_q):
        kvh = kv_of(h)
        q = q_ref[0, 0, :, h * HEAD_DIM:(h + 1) * HEAD_DIM]
        kw = k_ref[0, 0, pl.ds(ks, tw), kvh * HEAD_DIM:(kvh + 1) * HEAD_DIM]
        vw = v_ref[0, 0, pl.ds(ks, tw), kvh * HEAD_DIM:(kvh + 1) * HEAD_DIM]
        s = lax.dot_general(q, kw, (((1,), (1,)), ((), ())), preferred_element_type=F32) * scale
        s = s + bias_ref[0, h]
        m = jnp.max(s, axis=-1, keepdims=True)
        if with_sink:
            sink = sink_ref[0, h:h + 1, 0:1]
            m = jnp.maximum(m, sink)
        p = jnp.exp(s - m)
        l = jnp.sum(p, axis=-1, keepdims=True)
        if with_sink:
            l = l + jnp.exp(sink - m)
        o = jnp.dot(p.astype(BF16), vw, preferred_element_type=F32)
        o_ref[0, 0, :, h * HEAD_DIM:(h + 1) * HEAD_DIM] = (o / l).astype(o_ref.dtype)
        if with_lse:
            lse_parts.append(jnp.broadcast_to(m + jnp.log(l), (tq, LSE_LANES)))
    if with_lse:
        lse_ref[0, 0] = jnp.concatenate(lse_parts, axis=-1)


def _variant(i, nblk):
    return jnp.where(i == 0, 0, jnp.where(i == nblk - 1, 2, 1))


def _dilated_group(proj3, bias, j, dil, tq=128):
    b, s, w = proj3.shape
    length = s // dil
    half = B_PAIRS[j][0] // (2 * dil)
    tw = tq + 2 * half
    nblk = length // tq
    col0 = B_OFF + j * B_GW
    if dil == 1:
        src = proj3.reshape(b, 1, s, w)
        cb = col0 // W_BG
    else:
        src = proj3[:, :, col0:col0 + B_GW].reshape(b, length, dil, B_GW).transpose(0, 2, 1, 3)
        cb = 0
    kern = functools.partial(_banded_kernel, n_q=B_HEADS_PER_PAIR, kv_of=lambda h: h, tq=tq, tw=tw, half=half,
                             length=length, scale=HEAD_DIM ** -0.5, with_sink=False, with_lse=True)
    o, lse = pl.pallas_call(
        kern,
        out_shape=(jax.ShapeDtypeStruct((b, dil, length, W_BG), BF16),
                   jax.ShapeDtypeStruct((b, dil, length, 128), F32)),
        grid=(b, dil, nblk),
        in_specs=[pl.BlockSpec((1, 1, tq, W_BG), lambda bi, r, i: (bi, r, i, cb)),
                  pl.BlockSpec((1, 1, length, W_BG), lambda bi, r, i: (bi, r, 0, cb + 1)),
                  pl.BlockSpec((1, 1, length, W_BG), lambda bi, r, i: (bi, r, 0, cb + 2)),
                  pl.BlockSpec((1, B_HEADS_PER_PAIR, tq, tw), lambda bi, r, i: (_variant(i, nblk), 0, 0, 0))],
        out_specs=(pl.BlockSpec((1, 1, tq, W_BG), lambda bi, r, i: (bi, r, i, 0)),
                   pl.BlockSpec((1, 1, tq, 128), lambda bi, r, i: (bi, r, i, 0))),
        compiler_params=_params(("parallel", "parallel", "parallel")),
        name=f"dilated_attn_{dil}",
    )(src, src, src, bias)
    if dil > 1:
        o = o.transpose(0, 2, 1, 3)
        lse = lse.transpose(0, 2, 1, 3)
    return o.reshape(b, s, W_BG), lse.reshape(b, s, 128)


def _window_attn(proj3, bias, sink_rows, tq=128):
    b, s, w = proj3.shape
    half = D_WINDOW
    tw = tq + 2 * half
    nblk = s // tq
    g = D_HEADS // D_KV_HEADS
    kern = functools.partial(_banded_kernel, n_q=g, kv_of=lambda h: 0, tq=tq, tw=tw, half=half,
                             length=s, scale=HEAD_DIM ** -0.5, with_sink=True, with_lse=False)
    gw = g * HEAD_DIM
    src = proj3.reshape(b, 1, s, w)
    o = pl.pallas_call(
        kern,
        out_shape=jax.ShapeDtypeStruct((b, 1, s, W_D), BF16),
        grid=(b, D_KV_HEADS, nblk),
        in_specs=[pl.BlockSpec((1, 1, tq, gw), lambda bi, hi, i: (bi, 0, i, DQ_OFF // gw + hi)),
                  pl.BlockSpec((1, 1, s, HEAD_DIM), lambda bi, hi, i: (bi, 0, 0, DK_OFF // HEAD_DIM + hi)),
                  pl.BlockSpec((1, 1, s, HEAD_DIM), lambda bi, hi, i: (bi, 0, 0, DV_OFF // HEAD_DIM + hi)),
                  pl.BlockSpec((1, g, tq, tw), lambda bi, hi, i: (_variant(i, nblk), hi, 0, 0)),
                  pl.BlockSpec((1, g, 128), lambda bi, hi, i: (hi, 0, 0))],
        out_specs=pl.BlockSpec((1, 1, tq, gw), lambda bi, hi, i: (bi, 0, i, hi)),
        compiler_params=_params(("parallel", "parallel", "parallel")),
        name="window_attn",
    )(src, src, src, bias, sink_rows)
    return o.reshape(b, s, W_D)


def _mix_kernel(gate_ref, oa_ref, ob0_ref, ob1_ref, ob2_ref, l0_ref, l1_ref, l2_ref, oc_ref, od_ref, out_ref):
    def silu(col0, width):
        g = gate_ref[:, col0:col0 + width].astype(F32)
        return g * (1.0 / (1.0 + jnp.exp(-g)))

    out_ref[:, 0:W_A] = (oa_ref[...].astype(F32) * silu(0, W_A)).astype(BF16)
    l0, l1, l2 = l0_ref[...], l1_ref[...], l2_ref[...]
    mx = jnp.maximum(jnp.maximum(l0, l1), l2)
    e0, e1, e2 = jnp.exp(l0 - mx), jnp.exp(l1 - mx), jnp.exp(l2 - mx)
    inv = 1.0 / (e0 + e1 + e2)
    for j, (ob_ref, e) in enumerate(((ob0_ref, e0), (ob1_ref, e1), (ob2_ref, e2))):
        alpha = e * inv
        for h in range(B_HEADS_PER_PAIR):
            c0 = W_A + j * W_BG + h * HEAD_DIM
            a_h = alpha[:, h * LSE_LANES:h * LSE_LANES + 1]
            o_h = ob_ref[:, h * HEAD_DIM:(h + 1) * HEAD_DIM].astype(F32)
            out_ref[:, c0:c0 + HEAD_DIM] = (o_h * a_h * silu(c0, HEAD_DIM)).astype(BF16)
    c0 = W_A + W_B
    out_ref[:, c0:c0 + W_C] = (oc_ref[...].astype(F32) * silu(c0, W_C)).astype(BF16)
    c0 = W_A + W_B + W_C
    out_ref[:, c0:c0 + W_D] = (od_ref[...].astype(F32) * silu(c0, W_D)).astype(BF16)


def _mix(proj, o_a, o_b, lse_b, o_c, o_d, tm=512):
    m = proj.shape[0]
    row = lambda w: pl.BlockSpec((tm, w), lambda i: (i, 0))
    return pl.pallas_call(
        _mix_kernel,
        out_shape=jax.ShapeDtypeStruct((m, MIX_WIDTH), BF16),
        grid=(m // tm,),
        in_specs=[pl.BlockSpec((tm, MIX_WIDTH), lambda i: (i, GATE_OFF // MIX_WIDTH)),
                  row(W_A), row(W_BG), row(W_BG), row(W_BG), row(128), row(128), row(128), row(W_C), row(W_D)],
        out_specs=row(MIX_WIDTH),
        compiler_params=_params(("parallel",)),
        name="gate_mix",
    )(proj, o_a, *o_b, *lse_b, o_c, o_d)


def _split_w_in(w):
    sizes = (A_Q_LORA, A_KV_LORA, A_ROPE, W_B, W_B, W_B, W_C, C_KV_HEADS * HEAD_DIM, C_KV_HEADS * HEAD_DIM,
             W_D, D_KV_HEADS * HEAD_DIM, D_KV_HEADS * HEAD_DIM, MIX_WIDTH)
    pts = np.cumsum(sizes)[:-1]
    return jnp.split(w, [int(p) for p in pts], axis=1)


def _layout_w_in(w):
    (a_cq, a_ckv, a_kpe, b_q, b_k, b_v, c_q, c_k, c_v, d_q, d_k, d_v, gate) = _split_w_in(w)
    pad = jnp.zeros((w.shape[0], LAT_W - A_KV_LORA - A_Q_LORA - A_ROPE), w.dtype)
    b_groups = []
    for j in range(len(B_PAIRS)):
        b_groups += [p[:, j * W_BG:(j + 1) * W_BG] for p in (b_q, b_k, b_v)]
    pieces = [gate, *b_groups, c_q, d_q, c_k, c_v, d_k, d_v, a_ckv, a_cq, a_kpe, pad]
    return jnp.concatenate([p.astype(BF16) for p in pieces], axis=1)


def _layout_w_uq(w):
    w = w.reshape(A_Q_LORA, A_HEADS, A_NOPE + A_ROPE)
    w = jnp.pad(w, ((0, 0), (0, 0), (0, A_QK - A_NOPE - A_ROPE)))
    return w.reshape(A_Q_LORA, A_HEADS * A_QK).astype(BF16)


def _layer(x2, b, s, g_attn, w_in, g_qa, g_kva, w_uq, w_ukv, g_qn, g_kn, sinks, w_out, rel_bias, tables):
    cos_a, sin_a, cos_c, sin_c = tables
    h = _rmsnorm(x2, g_attn, BF16)
    proj = _in_proj(h, _layout_w_in(w_in))
    proj3 = proj.reshape(b, s, PROJ_W)

    q_a, k_a, v_a = _prep_a(proj, g_qa, g_kva, _layout_w_uq(w_uq), w_ukv.astype(BF16), cos_a, sin_a, s)
    o_a = _dense_attn(q_a.reshape(b, s, -1), k_a.reshape(b, s, -1), v_a.reshape(b, s, -1),
                      kv_heads=A_HEADS, groups=1, dq=A_QK, dv=A_V, k_col0=0, v_col0=0, tq=512, name="attn_a")

    o_b, lse_b = [], []
    for j, (win, dil) in enumerate(B_PAIRS):
        tab = rel_bias[:, j * B_HEADS_PER_PAIR:(j + 1) * B_HEADS_PER_PAIR]
        bias = _band_bias(tab, dil, win // (2 * dil), 128)
        o, lse = _dilated_group(proj3, bias, j, dil)
        o_b.append(o.reshape(b * s, W_BG))
        lse_b.append(lse.reshape(b * s, 128))

    q_c, k_c = _prep_c(proj, g_qn, g_kn, cos_c, sin_c, s)
    o_c = _dense_attn(q_c.reshape(b, s, -1), k_c.reshape(b, s, -1), proj3,
                      kv_heads=C_KV_HEADS, groups=C_HEADS // C_KV_HEADS, dq=HEAD_DIM, dv=HEAD_DIM,
                      k_col0=0, v_col0=CV_OFF // HEAD_DIM, tq=256, name="attn_c")

    bias_d = _band_bias(rel_bias[:, B_HEADS:], 1, D_WINDOW, 128)
    sink_rows = jnp.broadcast_to(sinks.astype(F32).reshape(D_KV_HEADS, D_HEADS // D_KV_HEADS, 1),
                                 (D_KV_HEADS, D_HEADS // D_KV_HEADS, 128))
    o_d = _window_attn(proj3, bias_d, sink_rows)

    mixed = _mix(proj, o_a.reshape(b * s, W_A), o_b, lse_b, o_c.reshape(b * s, W_C), o_d.reshape(b * s, W_D))
    return _out_proj(mixed, w_out.astype(BF16), x2)


def kernel(x, g_attn, w_in, g_qa, g_kva, w_uq, w_ukv, g_qn, g_kn, sinks, w_out, rel_bias, g_final):
    b, s, d = x.shape
    depth = w_in.shape[0]
    tables = _rope_tables(s)
    x2 = x.reshape(b * s, d)
    for l in range(depth):
        x2 = _layer(x2, b, s, g_attn[l], w_in[l], g_qa[l], g_kva[l], w_uq[l], w_ukv[l], g_qn[l], g_kn[l],
                    sinks[l], w_out[l], rel_bias, tables)
    return _rmsnorm(x2, g_final, F32).reshape(b, s, d)
```

```python
import functools
import math

import numpy as np
import jax
import jax.numpy as jnp
from jax import lax
from jax.experimental import pallas as pl
from jax.experimental.pallas import tpu as pltpu

D_MODEL = 4096
HEAD_DIM = 128
A_HEADS = 8
A_Q_LORA = 768
A_KV_LORA = 512
A_NOPE = 128
A_ROPE = 64
A_V = 128
B_PAIRS = ((128, 1), (512, 4), (2048, 16))
B_HEADS_PER_PAIR = 4
B_HEADS = B_HEADS_PER_PAIR * len(B_PAIRS)
C_HEADS = 8
C_KV_HEADS = 2
D_HEADS = 8
D_KV_HEADS = 2
D_WINDOW = 128
GRID_W = 64
ROPE_THETA = 10000.0
N_BUCKETS = 32
REL_MAX_DIST = 1024
EPS = 1e-6
NEG = -1e30
LOG2E = math.log2(math.e)

W_A = A_HEADS * A_V
W_B = B_HEADS * HEAD_DIM
W_C = C_HEADS * HEAD_DIM
W_D = D_HEADS * HEAD_DIM
W_BG = B_HEADS_PER_PAIR * HEAD_DIM
MIX_WIDTH = W_A + W_B + W_C + W_D

GATE_OFF = 0
B_OFF = GATE_OFF + MIX_WIDTH
B_GW = 3 * W_BG
CQ_OFF = B_OFF + 3 * B_GW
DQ_OFF = CQ_OFF + W_C
CK_OFF = DQ_OFF + W_D
CV_OFF = CK_OFF + C_KV_HEADS * HEAD_DIM
DK_OFF = CV_OFF + C_KV_HEADS * HEAD_DIM
DV_OFF = DK_OFF + D_KV_HEADS * HEAD_DIM
LAT_OFF = DV_OFF + D_KV_HEADS * HEAD_DIM
LAT_W = 1536
PROJ_W = LAT_OFF + LAT_W
A_QK = 256
LSE_LANES = 32

VMEM_LIMIT = 56 * 1024 * 1024

F32 = jnp.float32
BF16 = jnp.bfloat16


def _params(sem, vmem=VMEM_LIMIT):
    return pltpu.CompilerParams(dimension_semantics=sem, vmem_limit_bytes=vmem)


def _rmsnorm_kernel(x_ref, g_ref, o_ref):
    x = x_ref[...]
    ms = jnp.mean(x * x, axis=-1, keepdims=True)
    o_ref[...] = (x * lax.rsqrt(ms + EPS) * g_ref[...]).astype(o_ref.dtype)


def _rmsnorm(x, g, out_dtype, tm=256):
    m, d = x.shape
    return pl.pallas_call(
        _rmsnorm_kernel,
        out_shape=jax.ShapeDtypeStruct((m, d), out_dtype),
        grid=(m // tm,),
        in_specs=[pl.BlockSpec((tm, d), lambda i: (i, 0)),
                  pl.BlockSpec((1, d), lambda i: (0, 0))],
        out_specs=pl.BlockSpec((tm, d), lambda i: (i, 0)),
        compiler_params=_params(("parallel",)),
        name="rmsnorm",
    )(x, g.reshape(1, d))


def _matmul_kernel(x_ref, w_ref, o_ref):
    o_ref[...] = jnp.dot(x_ref[...], w_ref[...], preferred_element_type=F32).astype(o_ref.dtype)


def _in_proj(h, w, tm=1024, tn=768):
    m, k = h.shape
    n = w.shape[1]
    return pl.pallas_call(
        _matmul_kernel,
        out_shape=jax.ShapeDtypeStruct((m, n), BF16),
        grid=(m // tm, n // tn),
        in_specs=[pl.BlockSpec((tm, k), lambda i, j: (i, 0)),
                  pl.BlockSpec((k, tn), lambda i, j: (0, j))],
        out_specs=pl.BlockSpec((tm, tn), lambda i, j: (i, j)),
        compiler_params=_params(("parallel", "parallel")),
        name="in_proj",
    )(h, w)


def _out_proj_kernel(m_ref, w_ref, x_ref, o_ref):
    o_ref[...] = x_ref[...] + jnp.dot(m_ref[...], w_ref[...], preferred_element_type=F32)


def _out_proj(mixed, w, x, tm=512, tn=1024):
    m, k = mixed.shape
    n = w.shape[1]
    return pl.pallas_call(
        _out_proj_kernel,
        out_shape=jax.ShapeDtypeStruct((m, n), F32),
        grid=(n // tn, m // tm),
        in_specs=[pl.BlockSpec((tm, k), lambda j, i: (i, 0)),
                  pl.BlockSpec((k, tn), lambda j, i: (0, j)),
                  pl.BlockSpec((tm, tn), lambda j, i: (i, j))],
        out_specs=pl.BlockSpec((tm, tn), lambda j, i: (i, j)),
        compiler_params=_params(("parallel", "parallel")),
        name="out_proj",
    )(mixed, w, x)


def _rope_tile(x, cos, sin_signed):
    lane = lax.broadcasted_iota(jnp.int32, x.shape, 1)
    partner = jnp.where((lane % 64) < 32, pltpu.roll(x, 96, 1), pltpu.roll(x, 32, 1))
    return x * cos + partner * sin_signed


def _rope_tables(seq):
    inv = ROPE_THETA ** (-jnp.arange(0, 64, 2, dtype=F32) / 64)
    pos = jnp.arange(seq)

    def ang(p):
        return p.astype(F32)[:, None] * inv[None, :]

    def halves(a):
        c, s = jnp.cos(a), jnp.sin(a)
        return jnp.concatenate([c, c], -1), jnp.concatenate([-s, s], -1)

    ct, st = halves(ang(pos))
    zeros = jnp.zeros_like(ct)
    cos_a = jnp.concatenate([ct, zeros], -1)
    sin_a = jnp.concatenate([st, zeros], -1)
    cr, sr = halves(ang(pos // GRID_W))
    cc, sc = halves(ang(pos % GRID_W))
    cos_c = jnp.concatenate([cr, cc], -1)
    sin_c = jnp.concatenate([sr, sc], -1)
    return cos_a, sin_a, cos_c, sin_c


def _prep_a_kernel(lat_ref, gq_ref, gkv_ref, wuq_ref, wukv_ref, cos_ref, sin_ref, q_ref, k_ref, v_ref):
    lat = lat_ref[...].astype(F32)
    ckv = lat[:, 0:A_KV_LORA]
    cq = lat[:, A_KV_LORA:A_KV_LORA + A_Q_LORA]
    kpe = lat[:, A_KV_LORA + A_Q_LORA:A_KV_LORA + A_Q_LORA + 128]

    def rms(x, g):
        return x * lax.rsqrt(jnp.mean(x * x, axis=-1, keepdims=True) + EPS) * g

    qa = jnp.dot(rms(cq, gq_ref[...]).astype(BF16), wuq_ref[...], preferred_element_type=F32)
    kva = jnp.dot(rms(ckv, gkv_ref[...]).astype(BF16), wukv_ref[...], preferred_element_type=F32)
    cos = cos_ref[...]
    sin = sin_ref[...]
    scale = (A_NOPE + A_ROPE) ** -0.5 * LOG2E
    kpe_r = _rope_tile(kpe, cos, sin).astype(BF16)
    for h in range(A_HEADS):
        c0 = h * A_QK
        q_ref[:, c0:c0 + 128] = (qa[:, c0:c0 + 128] * scale).astype(BF16)
        q_ref[:, c0 + 128:c0 + 256] = (_rope_tile(qa[:, c0 + 128:c0 + 256], cos, sin) * scale).astype(BF16)
        k_ref[:, c0:c0 + 128] = kva[:, c0:c0 + 128].astype(BF16)
        k_ref[:, c0 + 128:c0 + 256] = kpe_r
        v_ref[:, h * A_V:(h + 1) * A_V] = kva[:, c0 + 128:c0 + 256].astype(BF16)


def _prep_a(proj, g_qa, g_kva, w_uq_p, w_ukv, cos_a, sin_a, seq, tm=512):
    m = proj.shape[0]
    nsb = seq // tm
    return pl.pallas_call(
        _prep_a_kernel,
        out_shape=(jax.ShapeDtypeStruct((m, A_HEADS * A_QK), BF16),
                   jax.ShapeDtypeStruct((m, A_HEADS * A_QK), BF16),
                   jax.ShapeDtypeStruct((m, W_A), BF16)),
        grid=(m // tm,),
        in_specs=[pl.BlockSpec((tm, LAT_W), lambda i: (i, LAT_OFF // LAT_W)),
                  pl.BlockSpec((1, A_Q_LORA), lambda i: (0, 0)),
                  pl.BlockSpec((1, A_KV_LORA), lambda i: (0, 0)),
                  pl.BlockSpec((A_Q_LORA, A_HEADS * A_QK), lambda i: (0, 0)),
                  pl.BlockSpec((A_KV_LORA, A_HEADS * A_QK), lambda i: (0, 0)),
                  pl.BlockSpec((tm, 128), lambda i: (i % nsb, 0)),
                  pl.BlockSpec((tm, 128), lambda i: (i % nsb, 0))],
        out_specs=(pl.BlockSpec((tm, A_HEADS * A_QK), lambda i: (i, 0)),
                   pl.BlockSpec((tm, A_HEADS * A_QK), lambda i: (i, 0)),
                   pl.BlockSpec((tm, W_A), lambda i: (i, 0))),
        compiler_params=_params(("parallel",)),
        name="prep_a",
    )(proj, g_qa.reshape(1, -1), g_kva.reshape(1, -1), w_uq_p, w_ukv, cos_a, sin_a)


def _prep_c_kernel(q_in_ref, k_in_ref, gq_ref, gk_ref, cos_ref, sin_ref, q_ref, k_ref):
    cos = cos_ref[...]
    sin = sin_ref[...]

    def norm_rope(x, g):
        x = x.astype(F32)
        xn = x * lax.rsqrt(jnp.mean(x * x, axis=-1, keepdims=True) + EPS) * g
        return _rope_tile(xn, cos, sin)

    scale = HEAD_DIM ** -0.5 * LOG2E
    for h in range(C_HEADS):
        sl = slice(h * HEAD_DIM, (h + 1) * HEAD_DIM)
        q_ref[:, sl] = (norm_rope(q_in_ref[:, sl], gq_ref[...]) * scale).astype(BF16)
    for h in range(C_KV_HEADS):
        sl = slice(h * HEAD_DIM, (h + 1) * HEAD_DIM)
        k_ref[:, sl] = norm_rope(k_in_ref[:, sl], gk_ref[...]).astype(BF16)


def _prep_c(proj, g_qn, g_kn, cos_c, sin_c, seq, tm=512):
    m = proj.shape[0]
    nsb = seq // tm
    kw = C_KV_HEADS * HEAD_DIM
    return pl.pallas_call(
        _prep_c_kernel,
        out_shape=(jax.ShapeDtypeStruct((m, W_C), BF16),
                   jax.ShapeDtypeStruct((m, kw), BF16)),
        grid=(m // tm,),
        in_specs=[pl.BlockSpec((tm, W_C), lambda i: (i, CQ_OFF // W_C)),
                  pl.BlockSpec((tm, kw), lambda i: (i, CK_OFF // kw)),
                  pl.BlockSpec((1, HEAD_DIM), lambda i: (0, 0)),
                  pl.BlockSpec((1, HEAD_DIM), lambda i: (0, 0)),
                  pl.BlockSpec((tm, 128), lambda i: (i % nsb, 0)),
                  pl.BlockSpec((tm, 128), lambda i: (i % nsb, 0))],
        out_specs=(pl.BlockSpec((tm, W_C), lambda i: (i, 0)),
                   pl.BlockSpec((tm, kw), lambda i: (i, 0))),
        compiler_params=_params(("parallel",)),
        name="prep_c",
    )(proj, proj, g_qn.reshape(1, -1), g_kn.reshape(1, -1), cos_c, sin_c)


def _dense_attn_kernel(q_ref, k_ref, v_ref, o_ref, *, groups, dq, dv, chunk):
    tq = q_ref.shape[1]
    s_len = k_ref.shape[1]
    n = groups * tq
    if groups == 1:
        q = q_ref[0]
    else:
        q = jnp.concatenate([q_ref[0, :, g * dq:(g + 1) * dq] for g in range(groups)], axis=0)
    m = l = acc = None
    for j in range(s_len // chunk):
        kj = k_ref[0, j * chunk:(j + 1) * chunk, :]
        vj = v_ref[0, j * chunk:(j + 1) * chunk, :]
        st = lax.dot_general(kj, q, (((1,), (1,)), ((), ())), preferred_element_type=F32)
        cmax = jnp.max(st, axis=0, keepdims=True)
        m_new = cmax if m is None else jnp.maximum(m, cmax)
        p = jnp.exp2(st - m_new)
        psum = jnp.sum(p, axis=0, keepdims=True)
        pv = lax.dot_general(vj, p.astype(BF16), (((0,), (0,)), ((), ())), preferred_element_type=F32)
        if m is None:
            l, acc = psum, pv
        else:
            alpha = jnp.exp2(m - m_new)
            l = alpha * l + psum
            acc = alpha * acc + pv
        m = m_new
    o_t = acc / l
    for g in range(groups):
        o_ref[0, :, g * dv:(g + 1) * dv] = o_t[:, g * tq:(g + 1) * tq].T.astype(o_ref.dtype)


def _dense_attn(q, k, v, *, kv_heads, groups, dq, dv, k_col0, v_col0, tq, name, chunk=512):
    b, s, _ = q.shape
    kern = functools.partial(_dense_attn_kernel, groups=groups, dq=dq, dv=dv, chunk=chunk)
    return pl.pallas_call(
        kern,
        out_shape=jax.ShapeDtypeStruct((b, s, kv_heads * groups * dv), BF16),
        grid=(b, kv_heads, s // tq),
        in_specs=[pl.BlockSpec((1, tq, groups * dq), lambda bi, hi, qi: (bi, qi, hi)),
                  pl.BlockSpec((1, s, dq), lambda bi, hi, qi: (bi, 0, k_col0 + hi)),
                  pl.BlockSpec((1, s, dv), lambda bi, hi, qi: (bi, 0, v_col0 + hi))],
        out_specs=pl.BlockSpec((1, tq, groups * dv), lambda bi, hi, qi: (bi, qi, hi)),
        compiler_params=_params(("parallel", "parallel", "parallel")),
        name=name,
    )(q, k, v)


def _rel_bucket_np(rel):
    nb = N_BUCKETS // 2
    max_exact = nb // 2
    ret = np.where(rel > 0, nb, 0)
    n = np.abs(rel)
    nf = np.maximum(n, 1).astype(np.float32)
    large = max_exact + (np.log(nf / max_exact) / math.log(REL_MAX_DIST / max_exact)
                         * (nb - max_exact)).astype(np.int32)
    large = np.minimum(large, nb - 1)
    return ret + np.where(n < max_exact, n, large)


def _band_bias(tab, dil, half, tq):
    tw = tq + 2 * half
    n = tq + tw - 1
    heads = tab.shape[1]
    tiles = []
    for delta in (0, -half, -2 * half):
        rel = delta + np.arange(n) - (tq - 1)
        inside = np.abs(rel) <= half
        vals = tab[_rel_bucket_np(dil * rel)].astype(F32)
        diag = jnp.where(inside[:, None], vals, NEG).T
        flat = jnp.tile(diag, (1, tq + 1))[:, :tq * (n + 1)]
        tiles.append(flat.reshape(heads, tq, n + 1)[:, ::-1, :tw])
    return jnp.stack(tiles, 0)


def _banded_kernel(*refs, n_q, kv_of, tq, tw, half, length, scale, with_sink, with_lse):
    q_ref, k_ref, v_ref, bias_ref = refs[:4]
    pos = 4
    sink_ref = None
    if with_sink:
        sink_ref = refs[pos]
        pos += 1
    o_ref = refs[pos]
    lse_ref = refs[pos + 1] if with_lse else None

    i0 = pl.program_id(2) * tq
    ks = pl.multiple_of(jnp.clip(i0 - half, 0, length - tw), 64)
    lse_parts = []
    for h in range(n_q):
        kvh = kv_of(h)
        q = q_ref[0, 0, :, h * HEAD_DIM:(h + 1) * HEAD_DIM]
        kw = k_ref[0, 0, pl.ds(ks, tw), kvh * HEAD_DIM:(kvh + 1) * HEAD_DIM]
        vw = v_ref[0, 0, pl.ds(ks, tw), kvh * HEAD_DIM:(kvh + 1) * HEAD_DIM]
        s = lax.dot_general(q, kw, (((1,), (1,)), ((), ())), preferred_element_type=F32) * scale
        s = s + bias_ref[0, h]
        m = jnp.max(s, axis=-1, keepdims=True)
        if with_sink:
            sink = sink_ref[0, h:h + 1, 0:1]
            m = jnp.maximum(m, sink)
        p = jnp.exp(s - m)
        l = jnp.sum(p, axis=-1, keepdims=True)
        if with_sink:
            l = l + jnp.exp(sink - m)
        o = jnp.dot(p.astype(BF16), vw, preferred_element_type=F32)
        o_ref[0, 0, :, h * HEAD_DIM:(h + 1) * HEAD_DIM] = (o / l).astype(o_ref.dtype)
        if with_lse:
            lse_parts.append(jnp.broadcast_to(m + jnp.log(l), (tq, LSE_LANES)))
    if with_lse:
        lse_ref[0, 0] = jnp.concatenate(lse_parts, axis=-1)


def _variant(i, nblk):
    return jnp.where(i == 0, 0, jnp.where(i == nblk - 1, 2, 1))


def _dilated_group(proj3, bias, j, dil, tq=128):
    b, s, w = proj3.shape
    length = s // dil
    half = B_PAIRS[j][0] // (2 * dil)
    tw = tq + 2 * half
    nblk = length // tq
    col0 = B_OFF + j * B_GW
    if dil == 1:
        src = proj3.reshape(b, 1, s, w)
        cb = col0 // W_BG
    else:
        src = proj3[:, :, col0:col0 + B_GW].reshape(b, length, dil, B_GW).transpose(0, 2, 1, 3)
        cb = 0
    kern = functools.partial(_banded_kernel, n_q=B_HEADS_PER_PAIR, kv_of=lambda h: h, tq=tq, tw=tw, half=half,
                             length=length, scale=HEAD_DIM ** -0.5, with_sink=False, with_lse=True)
    o, lse = pl.pallas_call(
        kern,
        out_shape=(jax.ShapeDtypeStruct((b, dil, length, W_BG), BF16),
                   jax.ShapeDtypeStruct((b, dil, length, 128), F32)),
        grid=(b, dil, nblk),
        in_specs=[pl.BlockSpec((1, 1, tq, W_BG), lambda bi, r, i: (bi, r, i, cb)),
                  pl.BlockSpec((1, 1, length, W_BG), lambda bi, r, i: (bi, r, 0, cb + 1)),
                  pl.BlockSpec((1, 1, length, W_BG), lambda bi, r, i: (bi, r, 0, cb + 2)),
                  pl.BlockSpec((1, B_HEADS_PER_PAIR, tq, tw), lambda bi, r, i: (_variant(i, nblk), 0, 0, 0))],
        out_specs=(pl.BlockSpec((1, 1, tq, W_BG), lambda bi, r, i: (bi, r, i, 0)),
                   pl.BlockSpec((1, 1, tq, 128), lambda bi, r, i: (bi, r, i, 0))),
        compiler_params=_params(("parallel", "parallel", "parallel")),
        name=f"dilated_attn_{dil}",
    )(src, src, src, bias)
    if dil > 1:
        o = o.transpose(0, 2, 1, 3)
        lse = lse.transpose(0, 2, 1, 3)
    return o.reshape(b, s, W_BG), lse.reshape(b, s, 128)


def _window_attn(proj3, bias, sink_rows, tq=128):
    b, s, w = proj3.shape
    half = D_WINDOW
    tw = tq + 2 * half
    nblk = s // tq
    g = D_HEADS // D_KV_HEADS
    kern = functools.partial(_banded_kernel, n_q=g, kv_of=lambda h: 0, tq=tq, tw=tw, half=half,
                             length=s, scale=HEAD_DIM ** -0.5, with_sink=True, with_lse=False)
    gw = g * HEAD_DIM
    src = proj3.reshape(b, 1, s, w)
    o = pl.pallas_call(
        kern,
        out_shape=jax.ShapeDtypeStruct((b, 1, s, W_D), BF16),
        grid=(b, D_KV_HEADS, nblk),
        in_specs=[pl.BlockSpec((1, 1, tq, gw), lambda bi, hi, i: (bi, 0, i, DQ_OFF // gw + hi)),
                  pl.BlockSpec((1, 1, s, HEAD_DIM), lambda bi, hi, i: (bi, 0, 0, DK_OFF // HEAD_DIM + hi)),
                  pl.BlockSpec((1, 1, s, HEAD_DIM), lambda bi, hi, i: (bi, 0, 0, DV_OFF // HEAD_DIM + hi)),
                  pl.BlockSpec((1, g, tq, tw), lambda bi, hi, i: (_variant(i, nblk), hi, 0, 0)),
                  pl.BlockSpec((1, g, 128), lambda bi, hi, i: (hi, 0, 0))],
        out_specs=pl.BlockSpec((1, 1, tq, gw), lambda bi, hi, i: (bi, 0, i, hi)),
        compiler_params=_params(("parallel", "parallel", "parallel")),
        name="window_attn",
    )(src, src, src, bias, sink_rows)
    return o.reshape(b, s, W_D)


def _mix_kernel(gate_ref, oa_ref, ob0_ref, ob1_ref, ob2_ref, l0_ref, l1_ref, l2_ref, oc_ref, od_ref, out_ref):
    def silu(col0, width):
        g = gate_ref[:, col0:col0 + width].astype(F32)
        return g * (1.0 / (1.0 + jnp.exp(-g)))

    out_ref[:, 0:W_A] = (oa_ref[...].astype(F32) * silu(0, W_A)).astype(BF16)
    l0, l1, l2 = l0_ref[...], l1_ref[...], l2_ref[...]
    mx = jnp.maximum(jnp.maximum(l0, l1), l2)
    e0, e1, e2 = jnp.exp(l0 - mx), jnp.exp(l1 - mx), jnp.exp(l2 - mx)
    inv = 1.0 / (e0 + e1 + e2)
    for j, (ob_ref, e) in enumerate(((ob0_ref, e0), (ob1_ref, e1), (ob2_ref, e2))):
        alpha = e * inv
        for h in range(B_HEADS_PER_PAIR):
            c0 = W_A + j * W_BG + h * HEAD_DIM
            a_h = alpha[:, h * LSE_LANES:h * LSE_LANES + 1]
            o_h = ob_ref[:, h * HEAD_DIM:(h + 1) * HEAD_DIM].astype(F32)
            out_ref[:, c0:c0 + HEAD_DIM] = (o_h * a_h * silu(c0, HEAD_DIM)).astype(BF16)
    c0 = W_A + W_B
    out_ref[:, c0:c0 + W_C] = (oc_ref[...].astype(F32) * silu(c0, W_C)).astype(BF16)
    c0 = W_A + W_B + W_C
    out_ref[:, c0:c0 + W_D] = (od_ref[...].astype(F32) * silu(c0, W_D)).astype(BF16)


def _mix(proj, o_a, o_b, lse_b, o_c, o_d, tm=512):
    m = proj.shape[0]
    row = lambda w: pl.BlockSpec((tm, w), lambda i: (i, 0))
    return pl.pallas_call(
        _mix_kernel,
        out_shape=jax.ShapeDtypeStruct((m, MIX_WIDTH), BF16),
        grid=(m // tm,),
        in_specs=[pl.BlockSpec((tm, MIX_WIDTH), lambda i: (i, GATE_OFF // MIX_WIDTH)),
                  row(W_A), row(W_BG), row(W_BG), row(W_BG), row(128), row(128), row(128), row(W_C), row(W_D)],
        out_specs=row(MIX_WIDTH),
        compiler_params=_params(("parallel",)),
        name="gate_mix",
    )(proj, o_a, *o_b, *lse_b, o_c, o_d)


def _split_w_in(w):
    sizes = (A_Q_LORA, A_KV_LORA, A_ROPE, W_B, W_B, W_B, W_C, C_KV_HEADS * HEAD_DIM, C_KV_HEADS * HEAD_DIM,
             W_D, D_KV_HEADS * HEAD_DIM, D_KV_HEADS * HEAD_DIM, MIX_WIDTH)
    pts = np.cumsum(sizes)[:-1]
    return jnp.split(w, [int(p) for p in pts], axis=1)


def _layout_w_in(w):
    (a_cq, a_ckv, a_kpe, b_q, b_k, b_v, c_q, c_k, c_v, d_q, d_k, d_v, gate) = _split_w_in(w)
    pad = jnp.zeros((w.shape[0], LAT_W - A_KV_LORA - A_Q_LORA - A_ROPE), w.dtype)
    b_groups = []
    for j in range(len(B_PAIRS)):
        b_groups += [p[:, j * W_BG:(j + 1) * W_BG] for p in (b_q, b_k, b_v)]
    pieces = [gate, *b_groups, c_q, d_q, c_k, c_v, d_k, d_v, a_ckv, a_cq, a_kpe, pad]
    return jnp.concatenate([p.astype(BF16) for p in pieces], axis=1)


def _layout_w_uq(w):
    w = w.reshape(A_Q_LORA, A_HEADS, A_NOPE + A_ROPE)
    w = jnp.pad(w, ((0, 0), (0, 0), (0, A_QK - A_NOPE - A_ROPE)))
    return w.reshape(A_Q_LORA, A_HEADS * A_QK).astype(BF16)


def _layer(x2, b, s, g_attn, w_in, g_qa, g_kva, w_uq, w_ukv, g_qn, g_kn, sinks, w_out, rel_bias, tables):
    cos_a, sin_a, cos_c, sin_c = tables
    h = _rmsnorm(x2, g_attn, BF16)
    proj = _in_proj(h, _layout_w_in(w_in))
    proj3 = proj.reshape(b, s, PROJ_W)

    q_a, k_a, v_a = _prep_a(proj, g_qa, g_kva, _layout_w_uq(w_uq), w_ukv.astype(BF16), cos_a, sin_a, s)
    o_a = _dense_attn(q_a.reshape(b, s, -1), k_a.reshape(b, s, -1), v_a.reshape(b, s, -1),
                      kv_heads=A_HEADS, groups=1, dq=A_QK, dv=A_V, k_col0=0, v_col0=0, tq=2048, name="attn_a", chunk=512)

    o_b, lse_b = [], []
    for j, (win, dil) in enumerate(B_PAIRS):
        tab = rel_bias[:, j * B_HEADS_PER_PAIR:(j + 1) * B_HEADS_PER_PAIR]
        bias = _band_bias(tab, dil, win // (2 * dil), 128)
        o, lse = _dilated_group(proj3, bias, j, dil)
        o_b.append(o.reshape(b * s, W_BG))
        lse_b.append(lse.reshape(b * s, 128))

    q_c, k_c = _prep_c(proj, g_qn, g_kn, cos_c, sin_c, s)
    o_c = _dense_attn(q_c.reshape(b, s, -1), k_c.reshape(b, s, -1), proj3,
                      kv_heads=C_KV_HEADS, groups=C_HEADS // C_KV_HEADS, dq=HEAD_DIM, dv=HEAD_DIM,
                      k_col0=0, v_col0=CV_OFF // HEAD_DIM, tq=512, name="attn_c", chunk=512)

    bias_d = _band_bias(rel_bias[:, B_HEADS:], 1, D_WINDOW, 128)
    sink_rows = jnp.broadcast_to(sinks.astype(F32).reshape(D_KV_HEADS, D_HEADS // D_KV_HEADS, 1),
                                 (D_KV_HEADS, D_HEADS // D_KV_HEADS, 128))
    o_d = _window_attn(proj3, bias_d, sink_rows)

    mixed = _mix(proj, o_a.reshape(b * s, W_A), o_b, lse_b, o_c.reshape(b * s, W_C), o_d.reshape(b * s, W_D))
    return _out_proj(mixed, w_out.astype(BF16), x2)


def kernel(x, g_attn, w_in, g_qa, g_kva, w_uq, w_ukv, g_qn, g_kn, sinks, w_out, rel_bias, g_final):
    b, s, d = x.shape
    depth = w_in.shape[0]
    tables = _rope_tables(s)
    x2 = x.reshape(b * s, d)
    for l in range(depth):
        x2 = _layer(x2, b, s, g_attn[l], w_in[l], g_qa[l], g_kva[l], w_uq[l], w_ukv[l], g_qn[l], g_kn[l],
                    sinks[l], w_out[l], rel_bias, tables)
    return _rmsnorm(x2, g_final, F32).reshape(b, s, d)
```

```python
import functools
import math

import numpy as np
import jax
import jax.numpy as jnp
from jax import lax
from jax.experimental import pallas as pl
from jax.experimental.pallas import tpu as pltpu

D_MODEL = 4096
HEAD_DIM = 128
A_HEADS = 8
A_Q_LORA = 768
A_KV_LORA = 512
A_NOPE = 128
A_ROPE = 64
A_V = 128
B_PAIRS = ((128, 1), (512, 4), (2048, 16))
B_HEADS_PER_PAIR = 4
B_HEADS = B_HEADS_PER_PAIR * len(B_PAIRS)
C_HEADS = 8
C_KV_HEADS = 2
D_HEADS = 8
D_KV_HEADS = 2
D_WINDOW = 128
GRID_W = 64
ROPE_THETA = 10000.0
N_BUCKETS = 32
REL_MAX_DIST = 1024
EPS = 1e-6
NEG = -1e30
LOG2E = math.log2(math.e)
LN2 = math.log(2.0)

W_A = A_HEADS * A_V
W_B = B_HEADS * HEAD_DIM
W_C = C_HEADS * HEAD_DIM
W_D = D_HEADS * HEAD_DIM
W_BG = B_HEADS_PER_PAIR * HEAD_DIM
MIX_WIDTH = W_A + W_B + W_C + W_D

LAT_USED = A_Q_LORA + A_KV_LORA + A_ROPE
LAT_W = 1536
BQ_OFF = 0
BK_OFF = BQ_OFF + W_B
BV_OFF = BK_OFF + W_B
CQ_OFF = BV_OFF + W_B
CK_OFF = CQ_OFF + W_C
CV_OFF = CK_OFF + C_KV_HEADS * HEAD_DIM
DQ_OFF = CV_OFF + C_KV_HEADS * HEAD_DIM
DK_OFF = DQ_OFF + W_D
DV_OFF = DK_OFF + D_KV_HEADS * HEAD_DIM
GATE_OFF = DV_OFF + D_KV_HEADS * HEAD_DIM
MAIN_W = GATE_OFF + MIX_WIDTH
GATE_BLK = 1536

A_QK = 256
LSE_LANES = 32
BAND_TQ = 128

VMEM_LIMIT = 56 * 1024 * 1024

F32 = jnp.float32
BF16 = jnp.bfloat16

_NT = (((1,), (1,)), ((), ()))
_TN = (((0,), (0,)), ((), ()))


def _params(sem, vmem=VMEM_LIMIT):
    return pltpu.CompilerParams(dimension_semantics=sem, vmem_limit_bytes=vmem)


def _rmsnorm_kernel(x_ref, g_ref, o_ref):
    x = x_ref[...]
    ms = jnp.mean(x * x, axis=-1, keepdims=True)
    o_ref[...] = (x * lax.rsqrt(ms + EPS) * g_ref[...]).astype(o_ref.dtype)


def _rmsnorm(x, g, out_dtype, tm=256):
    m, d = x.shape
    return pl.pallas_call(
        _rmsnorm_kernel,
        out_shape=jax.ShapeDtypeStruct((m, d), out_dtype),
        grid=(m // tm,),
        in_specs=[pl.BlockSpec((tm, d), lambda i: (i, 0)),
                  pl.BlockSpec((1, d), lambda i: (0, 0))],
        out_specs=pl.BlockSpec((tm, d), lambda i: (i, 0)),
        compiler_params=_params(("parallel",)),
        name="rmsnorm",
    )(x, g.reshape(1, d))


def _matmul_kernel(x_ref, w_ref, o_ref):
    o_ref[...] = jnp.dot(x_ref[...], w_ref[...], preferred_element_type=F32).astype(o_ref.dtype)


def _in_proj(h, w, tm, tn, name):
    m, k = h.shape
    n = w.shape[1]
    return pl.pallas_call(
        _matmul_kernel,
        out_shape=jax.ShapeDtypeStruct((m, n), BF16),
        grid=(m // tm, n // tn),
        in_specs=[pl.BlockSpec((tm, k), lambda i, j: (i, 0)),
                  pl.BlockSpec((k, tn), lambda i, j: (0, j))],
        out_specs=pl.BlockSpec((tm, tn), lambda i, j: (i, j)),
        compiler_params=_params(("parallel", "parallel")),
        name=name,
    )(h, w)


def _out_proj_kernel(m_ref, w_ref, x_ref, o_ref):
    o_ref[...] = x_ref[...] + jnp.dot(m_ref[...], w_ref[...], preferred_element_type=F32)


def _out_proj(mixed, w, x, tm=512, tn=1024):
    m, k = mixed.shape
    n = w.shape[1]
    return pl.pallas_call(
        _out_proj_kernel,
        out_shape=jax.ShapeDtypeStruct((m, n), F32),
        grid=(n // tn, m // tm),
        in_specs=[pl.BlockSpec((tm, k), lambda j, i: (i, 0)),
                  pl.BlockSpec((k, tn), lambda j, i: (0, j)),
                  pl.BlockSpec((tm, tn), lambda j, i: (i, j))],
        out_specs=pl.BlockSpec((tm, tn), lambda j, i: (i, j)),
        compiler_params=_params(("parallel", "parallel")),
        name="out_proj",
    )(mixed, w, x)


def _rope_tile(x, cos, sin_signed):
    lane = lax.broadcasted_iota(jnp.int32, x.shape, 1)
    partner = jnp.where((lane % 64) < 32, pltpu.roll(x, 96, 1), pltpu.roll(x, 32, 1))
    return x * cos + partner * sin_signed


def _rope_tables(seq):
    inv = ROPE_THETA ** (-jnp.arange(0, 64, 2, dtype=F32) / 64)
    pos = jnp.arange(seq)

    def ang(p):
        return p.astype(F32)[:, None] * inv[None, :]

    def halves(a):
        c, s = jnp.cos(a), jnp.sin(a)
        return jnp.concatenate([c, c], -1), jnp.concatenate([-s, s], -1)

    ct, st = halves(ang(pos))
    zeros = jnp.zeros_like(ct)
    cos_a = jnp.concatenate([ct, zeros], -1)
    sin_a = jnp.concatenate([st, zeros], -1)
    cr, sr = halves(ang(pos // GRID_W))
    cc, sc = halves(ang(pos % GRID_W))
    cos_c = jnp.concatenate([cr, cc], -1)
    sin_c = jnp.concatenate([sr, sc], -1)
    return cos_a, sin_a, cos_c, sin_c


def _prep_a_kernel(lat_ref, gq_ref, gkv_ref, wuq_ref, wukv_ref, cos_ref, sin_ref, q_ref, k_ref, v_ref):
    lat = lat_ref[...].astype(F32)
    cq = lat[:, 0:A_Q_LORA]
    ckv = lat[:, A_Q_LORA:A_Q_LORA + A_KV_LORA]
    kpe = lat[:, A_Q_LORA + A_KV_LORA:A_Q_LORA + A_KV_LORA + 128]

    def rms(x, g):
        return x * lax.rsqrt(jnp.mean(x * x, axis=-1, keepdims=True) + EPS) * g

    qa = jnp.dot(rms(cq, gq_ref[...]).astype(BF16), wuq_ref[...], preferred_element_type=F32)
    kva = jnp.dot(rms(ckv, gkv_ref[...]).astype(BF16), wukv_ref[...], preferred_element_type=F32)
    cos = cos_ref[...]
    sin = sin_ref[...]
    scale = (A_NOPE + A_ROPE) ** -0.5 * LOG2E
    kpe_r = _rope_tile(kpe, cos, sin).astype(BF16)
    for h in range(A_HEADS):
        c0 = h * A_QK
        q_ref[:, c0:c0 + 128] = (qa[:, c0:c0 + 128] * scale).astype(BF16)
        q_ref[:, c0 + 128:c0 + 256] = (_rope_tile(qa[:, c0 + 128:c0 + 256], cos, sin) * scale).astype(BF16)
        k_ref[:, c0:c0 + 128] = kva[:, c0:c0 + 128].astype(BF16)
        k_ref[:, c0 + 128:c0 + 256] = kpe_r
        v_ref[:, h * A_V:(h + 1) * A_V] = kva[:, c0 + 128:c0 + 256].astype(BF16)


def _prep_a(lat, g_qa, g_kva, w_uq_p, w_ukv, cos_a, sin_a, seq, tm=512):
    m = lat.shape[0]
    nsb = seq // tm
    return pl.pallas_call(
        _prep_a_kernel,
        out_shape=(jax.ShapeDtypeStruct((m, A_HEADS * A_QK), BF16),
                   jax.ShapeDtypeStruct((m, A_HEADS * A_QK), BF16),
                   jax.ShapeDtypeStruct((m, W_A), BF16)),
        grid=(m // tm,),
        in_specs=[pl.BlockSpec((tm, LAT_W), lambda i: (i, 0)),
                  pl.BlockSpec((1, A_Q_LORA), lambda i: (0, 0)),
                  pl.BlockSpec((1, A_KV_LORA), lambda i: (0, 0)),
                  pl.BlockSpec((A_Q_LORA, A_HEADS * A_QK), lambda i: (0, 0)),
                  pl.BlockSpec((A_KV_LORA, A_HEADS * A_QK), lambda i: (0, 0)),
                  pl.BlockSpec((tm, 128), lambda i: (i % nsb, 0)),
                  pl.BlockSpec((tm, 128), lambda i: (i % nsb, 0))],
        out_specs=(pl.BlockSpec((tm, A_HEADS * A_QK), lambda i: (i, 0)),
                   pl.BlockSpec((tm, A_HEADS * A_QK), lambda i: (i, 0)),
                   pl.BlockSpec((tm, W_A), lambda i: (i, 0))),
        compiler_params=_params(("parallel",)),
        name="prep_a",
    )(lat, g_qa.reshape(1, -1), g_kva.reshape(1, -1), w_uq_p, w_ukv, cos_a, sin_a)


def _prep_c_kernel(q_in_ref, k_in_ref, gq_ref, gk_ref, cos_ref, sin_ref, q_ref, k_ref):
    cos = cos_ref[...]
    sin = sin_ref[...]

    def norm_rope(x, g):
        x = x.astype(F32)
        xn = x * lax.rsqrt(jnp.mean(x * x, axis=-1, keepdims=True) + EPS) * g
        return _rope_tile(xn, cos, sin)

    scale = HEAD_DIM ** -0.5 * LOG2E
    for h in range(q_ref.shape[1] // HEAD_DIM):
        sl = slice(h * HEAD_DIM, (h + 1) * HEAD_DIM)
        q_ref[:, sl] = (norm_rope(q_in_ref[:, sl], gq_ref[...]) * scale).astype(BF16)

    @pl.when(pl.program_id(1) == 0)
    def _():
        for h in range(C_KV_HEADS):
            sl = slice(h * HEAD_DIM, (h + 1) * HEAD_DIM)
            k_ref[:, sl] = norm_rope(k_in_ref[:, sl], gk_ref[...]).astype(BF16)


def _prep_c(proj, g_qn, g_kn, cos_c, sin_c, seq, tm=512, qw=512):
    m = proj.shape[0]
    nsb = seq // tm
    kw = C_KV_HEADS * HEAD_DIM
    return pl.pallas_call(
        _prep_c_kernel,
        out_shape=(jax.ShapeDtypeStruct((m, W_C), BF16),
                   jax.ShapeDtypeStruct((m, kw), BF16)),
        grid=(m // tm, W_C // qw),
        in_specs=[pl.BlockSpec((tm, qw), lambda i, c: (i, CQ_OFF // qw + c)),
                  pl.BlockSpec((tm, kw), lambda i, c: (i, CK_OFF // kw)),
                  pl.BlockSpec((1, HEAD_DIM), lambda i, c: (0, 0)),
                  pl.BlockSpec((1, HEAD_DIM), lambda i, c: (0, 0)),
                  pl.BlockSpec((tm, 128), lambda i, c: (i % nsb, 0)),
                  pl.BlockSpec((tm, 128), lambda i, c: (i % nsb, 0))],
        out_specs=(pl.BlockSpec((tm, qw), lambda i, c: (i, c)),
                   pl.BlockSpec((tm, kw), lambda i, c: (i, 0))),
        compiler_params=_params(("parallel", "arbitrary")),
        name="prep_c",
    )(proj, proj, g_qn.reshape(1, -1), g_kn.reshape(1, -1), cos_c, sin_c)


def _dense_attn_kernel(q_ref, k_ref, v_ref, o_ref, *, groups, dq, dv, chunk):
    tq = q_ref.shape[1]
    s_len = k_ref.shape[1]
    if groups == 1:
        q = q_ref[0]
    else:
        q = jnp.concatenate([q_ref[0, :, g * dq:(g + 1) * dq] for g in range(groups)], axis=0)
    m = l = acc = None
    for j in range(s_len // chunk):
        kj = k_ref[0, j * chunk:(j + 1) * chunk, :]
        vj = v_ref[0, j * chunk:(j + 1) * chunk, :]
        st = lax.dot_general(kj, q, _NT, preferred_element_type=F32)
        cmax = jnp.max(st, axis=0, keepdims=True)
        m_new = cmax if m is None else jnp.maximum(m, cmax)
        p = jnp.exp2(st - m_new)
        psum = jnp.sum(p, axis=0, keepdims=True)
        pv = lax.dot_general(vj, p.astype(BF16), _TN, preferred_element_type=F32)
        if m is None:
            l, acc = psum, pv
        else:
            alpha = jnp.exp2(m - m_new)
            l = alpha * l + psum
            acc = alpha * acc + pv
        m = m_new
    o_t = acc / l
    for g in range(groups):
        o_ref[0, :, g * dv:(g + 1) * dv] = o_t[:, g * tq:(g + 1) * tq].T.astype(o_ref.dtype)


def _dense_attn(q, k, v, *, kv_heads, groups, dq, dv, k_col0, v_col0, tq, name, chunk=512):
    b, s, _ = q.shape
    kern = functools.partial(_dense_attn_kernel, groups=groups, dq=dq, dv=dv, chunk=chunk)
    return pl.pallas_call(
        kern,
        out_shape=jax.ShapeDtypeStruct((b, s, kv_heads * groups * dv), BF16),
        grid=(b, kv_heads, s // tq),
        in_specs=[pl.BlockSpec((1, tq, groups * dq), lambda bi, hi, qi: (bi, qi, hi)),
                  pl.BlockSpec((1, s, dq), lambda bi, hi, qi: (bi, 0, k_col0 + hi)),
                  pl.BlockSpec((1, s, dv), lambda bi, hi, qi: (bi, 0, v_col0 + hi))],
        out_specs=pl.BlockSpec((1, tq, groups * dv), lambda bi, hi, qi: (bi, qi, hi)),
        compiler_params=_params(("parallel", "parallel", "parallel")),
        name=name,
    )(q, k, v)


def _rel_bucket_np(rel):
    nb = N_BUCKETS // 2
    max_exact = nb // 2
    ret = np.where(rel > 0, nb, 0)
    n = np.abs(rel)
    nf = np.maximum(n, 1).astype(np.float32)
    large = max_exact + (np.log(nf / max_exact) / math.log(REL_MAX_DIST / max_exact)
                         * (nb - max_exact)).astype(np.int32)
    large = np.minimum(large, nb - 1)
    return ret + np.where(n < max_exact, n, large)


def _band_bias(tab, dil, half, tq):
    tw = tq + 2 * half
    n = tq + tw - 1
    heads = tab.shape[1]
    tiles = []
    for delta in (0, -half, -2 * half):
        rel = delta + np.arange(n) - (tq - 1)
        inside = np.abs(rel) <= half
        vals = tab[_rel_bucket_np(dil * rel)].astype(F32) * LOG2E
        diag = jnp.where(inside[:, None], vals, NEG).T
        flat = jnp.tile(diag, (1, tq + 1))[:, :tq * (n + 1)]
        tiles.append(flat.reshape(heads, tq, n + 1)[:, ::-1, :tw])
    return jnp.swapaxes(jnp.stack(tiles, 0), -1, -2)


def _banded_kernel(*refs, head_groups, n_sub, half, length, with_sink, with_lse):
    q_ref, k_ref, v_ref, bias_ref = refs[:4]
    pos = 4
    sink_ref = None
    if with_sink:
        sink_ref = refs[pos]
        pos += 1
    o_ref = refs[pos]
    lse_ref = refs[pos + 1] if with_lse else None

    tq = BAND_TQ
    tw = tq + 2 * half
    nblk = length // tq
    for sb in range(n_sub):
        blk = pl.program_id(2) * n_sub + sb
        ks = pl.multiple_of(jnp.clip(blk * tq - half, 0, length - tw), 64)
        var = jnp.where(blk == 0, 0, jnp.where(blk == nblk - 1, 2, 1))
        rows = slice(sb * tq, (sb + 1) * tq)
        lse_rows = []
        for kvh, heads in head_groups:
            n = len(heads) * tq
            cols = slice(kvh * HEAD_DIM, (kvh + 1) * HEAD_DIM)
            kw = k_ref[0, 0, pl.ds(ks, tw), cols]
            vw = v_ref[0, 0, pl.ds(ks, tw), cols]
            qs = [q_ref[0, 0, rows, h * HEAD_DIM:(h + 1) * HEAD_DIM] for h in heads]
            q = qs[0] if len(heads) == 1 else jnp.concatenate(qs, axis=0)
            bias = [bias_ref[var, h] for h in heads]
            bias = bias[0] if len(heads) == 1 else jnp.concatenate(bias, axis=1)
            st = lax.dot_general(kw, q, _NT, preferred_element_type=F32) + bias
            m = jnp.max(st, axis=0, keepdims=True)
            if with_sink:
                sink = sink_ref[0]
                m = jnp.maximum(m, sink)
            p = jnp.exp2(st - m)
            l = jnp.sum(p, axis=0, keepdims=True)
            if with_sink:
                l = l + jnp.exp2(sink - m)
            o_t = lax.dot_general(vw, p.astype(BF16), _TN, preferred_element_type=F32) / l
            for gi, h in enumerate(heads):
                o_ref[0, 0, rows, h * HEAD_DIM:(h + 1) * HEAD_DIM] = (
                    o_t[:, gi * tq:(gi + 1) * tq].T.astype(o_ref.dtype))
            if with_lse:
                lse = (m + jnp.log2(l)) * LN2
                for gi in range(len(heads)):
                    lse_rows.append(jnp.broadcast_to(lse[:, gi * tq:(gi + 1) * tq], (LSE_LANES, tq)))
        if with_lse:
            lse_ref[0, 0, rows, :] = jnp.concatenate(lse_rows, axis=0).T


def _dilated_group(proj3, bias, j, dil):
    b, s, w = proj3.shape
    length = s // dil
    half = B_PAIRS[j][0] // (2 * dil)
    n_sub = min(4, length // BAND_TQ)
    tqs = n_sub * BAND_TQ
    cq, ck, cv = (BQ_OFF + j * W_BG, BK_OFF + j * W_BG, BV_OFF + j * W_BG)
    if dil == 1:
        src = proj3.reshape(b, 1, s, w)
        bq, bk, bv = cq // W_BG, ck // W_BG, cv // W_BG
    else:
        src = jnp.concatenate([proj3[:, :, c:c + W_BG] for c in (cq, ck, cv)], axis=-1)
        src = src.reshape(b, length, dil, 3 * W_BG).transpose(0, 2, 1, 3)
        bq, bk, bv = 0, 1, 2
    groups = tuple((h, (h,)) for h in range(B_HEADS_PER_PAIR))
    kern = functools.partial(_banded_kernel, head_groups=groups, n_sub=n_sub, half=half, length=length,
                             with_sink=False, with_lse=True)
    o, lse = pl.pallas_call(
        kern,
        out_shape=(jax.ShapeDtypeStruct((b, dil, length, W_BG), BF16),
                   jax.ShapeDtypeStruct((b, dil, length, 128), F32)),
        grid=(b, dil, length // tqs),
        in_specs=[pl.BlockSpec((1, 1, tqs, W_BG), lambda bi, r, i: (bi, r, i, bq)),
                  pl.BlockSpec((1, 1, length, W_BG), lambda bi, r, i: (bi, r, 0, bk)),
                  pl.BlockSpec((1, 1, length, W_BG), lambda bi, r, i: (bi, r, 0, bv)),
                  pl.BlockSpec(bias.shape, lambda bi, r, i: (0, 0, 0, 0))],
        out_specs=(pl.BlockSpec((1, 1, tqs, W_BG), lambda bi, r, i: (bi, r, i, 0)),
                   pl.BlockSpec((1, 1, tqs, 128), lambda bi, r, i: (bi, r, i, 0))),
        compiler_params=_params(("parallel", "parallel", "parallel")),
        name=f"dilated_attn_{dil}",
    )(src, src, src, bias)
    if dil > 1:
        o = o.transpose(0, 2, 1, 3)
        lse = lse.transpose(0, 2, 1, 3)
    return o.reshape(b, s, W_BG), lse.reshape(b, s, 128)


def _window_attn(proj3, bias, sink_rows, n_sub=4):
    b, s, w = proj3.shape
    g = D_HEADS // D_KV_HEADS
    gw = g * HEAD_DIM
    tqs = n_sub * BAND_TQ
    kern = functools.partial(_banded_kernel, head_groups=((0, tuple(range(g))),), n_sub=n_sub, half=D_WINDOW,
                             length=s, with_sink=True, with_lse=False)
    src = proj3.reshape(b, 1, s, w)
    o = pl.pallas_call(
        kern,
        out_shape=jax.ShapeDtypeStruct((b, 1, s, W_D), BF16),
        grid=(b, D_KV_HEADS, s // tqs),
        in_specs=[pl.BlockSpec((1, 1, tqs, gw), lambda bi, hi, i: (bi, 0, i, DQ_OFF // gw + hi)),
                  pl.BlockSpec((1, 1, s, HEAD_DIM), lambda bi, hi, i: (bi, 0, 0, DK_OFF // HEAD_DIM + hi)),
                  pl.BlockSpec((1, 1, s, HEAD_DIM), lambda bi, hi, i: (bi, 0, 0, DV_OFF // HEAD_DIM + hi)),
                  pl.BlockSpec((3, g) + bias.shape[2:], lambda bi, hi, i: (0, hi, 0, 0)),
                  pl.BlockSpec((1, 1, g * BAND_TQ), lambda bi, hi, i: (hi, 0, 0))],
        out_specs=pl.BlockSpec((1, 1, tqs, gw), lambda bi, hi, i: (bi, 0, i, hi)),
        compiler_params=_params(("parallel", "parallel", "parallel")),
        name="window_attn",
    )(src, src, src, bias, sink_rows)
    return o.reshape(b, s, W_D)


def _mix_kernel(g0_ref, g1_ref, g2_ref, oa_ref, ob0_ref, ob1_ref, ob2_ref, l0_ref, l1_ref, l2_ref,
                oc_ref, od_ref, out_ref):
    gate_refs = (g0_ref, g1_ref, g2_ref)

    def silu(col0, width):
        blk, off = divmod(col0, GATE_BLK)
        assert off + width <= GATE_BLK
        g = gate_refs[blk][:, off:off + width].astype(F32)
        return g * (1.0 / (1.0 + jnp.exp(-g)))

    def plain(o_ref, col0, width, piece=512):
        for c in range(0, width, piece):
            out_ref[:, col0 + c:col0 + c + piece] = (
                o_ref[:, c:c + piece].astype(F32) * silu(col0 + c, piece)).astype(BF16)

    plain(oa_ref, 0, W_A)
    l0, l1, l2 = l0_ref[...], l1_ref[...], l2_ref[...]
    mx = jnp.maximum(jnp.maximum(l0, l1), l2)
    e0, e1, e2 = jnp.exp(l0 - mx), jnp.exp(l1 - mx), jnp.exp(l2 - mx)
    inv = 1.0 / (e0 + e1 + e2)
    for j, (ob_ref, e) in enumerate(((ob0_ref, e0), (ob1_ref, e1), (ob2_ref, e2))):
        alpha = e * inv
        for h in range(B_HEADS_PER_PAIR):
            c0 = W_A + j * W_BG + h * HEAD_DIM
            a_h = alpha[:, h * LSE_LANES:h * LSE_LANES + 1]
            o_h = ob_ref[:, h * HEAD_DIM:(h + 1) * HEAD_DIM].astype(F32)
            out_ref[:, c0:c0 + HEAD_DIM] = (o_h * a_h * silu(c0, HEAD_DIM)).astype(BF16)
    plain(oc_ref, W_A + W_B, W_C)
    plain(od_ref, W_A + W_B + W_C, W_D)


def _mix(proj, o_a, o_b, lse_b, o_c, o_d, tm=512):
    m = proj.shape[0]
    row = lambda w: pl.BlockSpec((tm, w), lambda i: (i, 0))
    gate = lambda c: pl.BlockSpec((tm, GATE_BLK), lambda i: (i, GATE_OFF // GATE_BLK + c))
    return pl.pallas_call(
        _mix_kernel,
        out_shape=jax.ShapeDtypeStruct((m, MIX_WIDTH), BF16),
        grid=(m // tm,),
        in_specs=[gate(0), gate(1), gate(2),
                  row(W_A), row(W_BG), row(W_BG), row(W_BG), row(128), row(128), row(128), row(W_C), row(W_D)],
        out_specs=row(MIX_WIDTH),
        compiler_params=_params(("parallel",)),
        name="gate_mix",
    )(proj, proj, proj, o_a, *o_b, *lse_b, o_c, o_d)


def _layout_w_in(w):
    lat = jnp.pad(w[:, :LAT_USED], ((0, 0), (0, LAT_W - LAT_USED))).astype(BF16)
    qscale = np.ones((MAIN_W,), np.float32)
    qscale[BQ_OFF:BQ_OFF + W_B] = HEAD_DIM ** -0.5 * LOG2E
    qscale[DQ_OFF:DQ_OFF + W_D] = HEAD_DIM ** -0.5 * LOG2E
    main = (w[:, LAT_USED:] * qscale[None, :]).astype(BF16)
    return lat, main


def _layout_w_uq(w):
    w = w.reshape(A_Q_LORA, A_HEADS, A_NOPE + A_ROPE)
    w = jnp.pad(w, ((0, 0), (0, 0), (0, A_QK - A_NOPE - A_ROPE)))
    return w.reshape(A_Q_LORA, A_HEADS * A_QK).astype(BF16)


def _layer(x2, b, s, g_attn, w_in, g_qa, g_kva, w_uq, w_ukv, g_qn, g_kn, sinks, w_out, rel_bias, tables):
    cos_a, sin_a, cos_c, sin_c = tables
    h = _rmsnorm(x2, g_attn, BF16)
    w_lat, w_main = _layout_w_in(w_in)
    proj = _in_proj(h, w_main, 1024, 768, "in_proj")
    lat = _in_proj(h, w_lat, 512, LAT_W, "in_proj_latent")
    proj3 = proj.reshape(b, s, MAIN_W)

    q_a, k_a, v_a = _prep_a(lat, g_qa, g_kva, _layout_w_uq(w_uq), w_ukv.astype(BF16), cos_a, sin_a, s)
    o_a = _dense_attn(q_a.reshape(b, s, -1), k_a.reshape(b, s, -1), v_a.reshape(b, s, -1),
                      kv_heads=A_HEADS, groups=1, dq=A_QK, dv=A_V, k_col0=0, v_col0=0, tq=2048, name="attn_a")

    o_b, lse_b = [], []
    for j, (win, dil) in enumerate(B_PAIRS):
        tab = rel_bias[:, j * B_HEADS_PER_PAIR:(j + 1) * B_HEADS_PER_PAIR]
        bias = _band_bias(tab, dil, win // (2 * dil), BAND_TQ)
        o, lse = _dilated_group(proj3, bias, j, dil)
        o_b.append(o.reshape(b * s, W_BG))
        lse_b.append(lse.reshape(b * s, 128))

    q_c, k_c = _prep_c(proj, g_qn, g_kn, cos_c, sin_c, s)
    o_c = _dense_attn(q_c.reshape(b, s, -1), k_c.reshape(b, s, -1), proj3,
                      kv_heads=C_KV_HEADS, groups=C_HEADS // C_KV_HEADS, dq=HEAD_DIM, dv=HEAD_DIM,
                      k_col0=0, v_col0=CV_OFF // HEAD_DIM, tq=512, name="attn_c")

    g = D_HEADS // D_KV_HEADS
    bias_d = _band_bias(rel_bias[:, B_HEADS:], 1, D_WINDOW, BAND_TQ)
    sink_rows = jnp.repeat(sinks.astype(F32) * LOG2E, BAND_TQ).reshape(D_KV_HEADS, 1, g * BAND_TQ)
    o_d = _window_attn(proj3, bias_d, sink_rows)

    mixed = _mix(proj, o_a.reshape(b * s, W_A), o_b, lse_b, o_c.reshape(b * s, W_C), o_d.reshape(b * s, W_D))
    return _out_proj(mixed, w_out.astype(BF16), x2)


def kernel(x, g_attn, w_in, g_qa, g_kva, w_uq, w_ukv, g_qn, g_kn, sinks, w_out, rel_bias, g_final):
    b, s, d = x.shape
    depth = w_in.shape[0]
    tables = _rope_tables(s)
    x2 = x.reshape(b * s, d)
    for l in range(depth):
        x2 = _layer(x2, b, s, g_attn[l], w_in[l], g_qa[l], g_kva[l], w_uq[l], w_ukv[l], g_qn[l], g_kn[l],
                    sinks[l], w_out[l], rel_bias, tables)
    return _rmsnorm(x2, g_final, F32).reshape(b, s, d)
```

```python
import functools
import math

import numpy as np
import jax
import jax.numpy as jnp
from jax import lax
from jax.experimental import pallas as pl
from jax.experimental.pallas import tpu as pltpu

D_MODEL = 4096
HEAD_DIM = 128
A_HEADS = 8
A_Q_LORA = 768
A_KV_LORA = 512
A_NOPE = 128
A_ROPE = 64
A_V = 128
B_PAIRS = ((128, 1), (512, 4), (2048, 16))
B_HEADS_PER_PAIR = 4
B_HEADS = B_HEADS_PER_PAIR * len(B_PAIRS)
C_HEADS = 8
C_KV_HEADS = 2
D_HEADS = 8
D_KV_HEADS = 2
D_WINDOW = 128
GRID_W = 64
ROPE_THETA = 10000.0
N_BUCKETS = 32
REL_MAX_DIST = 1024
EPS = 1e-6
NEG = -1e30
LOG2E = math.log2(math.e)
LN2 = math.log(2.0)

W_A = A_HEADS * A_V
W_B = B_HEADS * HEAD_DIM
W_C = C_HEADS * HEAD_DIM
W_D = D_HEADS * HEAD_DIM
W_BG = B_HEADS_PER_PAIR * HEAD_DIM
MIX_WIDTH = W_A + W_B + W_C + W_D

LAT_USED = A_Q_LORA + A_KV_LORA + A_ROPE
LAT_W = 1536
BQ_OFF = 0
BK_OFF = BQ_OFF + W_B
BV_OFF = BK_OFF + W_B
CQ_OFF = BV_OFF + W_B
CK_OFF = CQ_OFF + W_C
CV_OFF = CK_OFF + C_KV_HEADS * HEAD_DIM
DQ_OFF = CV_OFF + C_KV_HEADS * HEAD_DIM
DK_OFF = DQ_OFF + W_D
DV_OFF = DK_OFF + D_KV_HEADS * HEAD_DIM
GATE_OFF = DV_OFF + D_KV_HEADS * HEAD_DIM
MAIN_W = GATE_OFF + MIX_WIDTH
GATE_BLK = 1536

A_QK = 256
LSE_LANES = 32
BAND_TQ = 128

VMEM_LIMIT = 56 * 1024 * 1024

F32 = jnp.float32
BF16 = jnp.bfloat16

_NT = (((1,), (1,)), ((), ()))
_TN = (((0,), (0,)), ((), ()))


def _params(sem, vmem=VMEM_LIMIT):
    return pltpu.CompilerParams(dimension_semantics=sem, vmem_limit_bytes=vmem)


def _rmsnorm_kernel(x_ref, g_ref, o_ref):
    x = x_ref[...]
    ms = jnp.mean(x * x, axis=-1, keepdims=True)
    o_ref[...] = (x * lax.rsqrt(ms + EPS) * g_ref[...]).astype(o_ref.dtype)


def _rmsnorm(x, g, out_dtype, tm=256):
    m, d = x.shape
    return pl.pallas_call(
        _rmsnorm_kernel,
        out_shape=jax.ShapeDtypeStruct((m, d), out_dtype),
        grid=(m // tm,),
        in_specs=[pl.BlockSpec((tm, d), lambda i: (i, 0)),
                  pl.BlockSpec((1, d), lambda i: (0, 0))],
        out_specs=pl.BlockSpec((tm, d), lambda i: (i, 0)),
        compiler_params=_params(("parallel",)),
        name="rmsnorm",
    )(x, g.reshape(1, d))


def _matmul_kernel(x_ref, w_ref, o_ref):
    o_ref[...] = jnp.dot(x_ref[...], w_ref[...], preferred_element_type=F32).astype(o_ref.dtype)


def _in_proj(h, w, tm, tn, name):
    m, k = h.shape
    n = w.shape[1]
    return pl.pallas_call(
        _matmul_kernel,
        out_shape=jax.ShapeDtypeStruct((m, n), BF16),
        grid=(m // tm, n // tn),
        in_specs=[pl.BlockSpec((tm, k), lambda i, j: (i, 0)),
                  pl.BlockSpec((k, tn), lambda i, j: (0, j))],
        out_specs=pl.BlockSpec((tm, tn), lambda i, j: (i, j)),
        compiler_params=_params(("parallel", "parallel")),
        name=name,
    )(h, w)


def _in_proj_main_kernel(x_ref, w_ref, s_ref, o_ref, wb_ref, *, shift, tn):
    @pl.when(pl.program_id(1) == 0)
    def _():
        wb_ref[...] = (w_ref[:, shift:shift + tn] * s_ref[...]).astype(BF16)

    o_ref[...] = jnp.dot(x_ref[...], wb_ref[...], preferred_element_type=F32).astype(o_ref.dtype)


def _in_proj_main(h, w_in, layer, col_scale, tm=512, tn=768):
    m, k = h.shape
    shift = LAT_USED % 128
    win0 = LAT_USED - shift
    over = win0 + MAIN_W + 128 - w_in.shape[2]
    kern = functools.partial(_in_proj_main_kernel, shift=shift, tn=tn)
    return pl.pallas_call(
        kern,
        out_shape=jax.ShapeDtypeStruct((m, MAIN_W), BF16),
        grid=(MAIN_W // tn, m // tm),
        in_specs=[pl.BlockSpec((tm, k), lambda j, i: (i, 0)),
                  pl.BlockSpec((pl.Squeezed(), pl.Element(k), pl.Element(tn + 128, (0, max(over, 0)))),
                               lambda j, i: (layer, 0, pl.multiple_of(win0 + j * tn, 128))),
                  pl.BlockSpec((1, tn), lambda j, i: (0, j))],
        out_specs=pl.BlockSpec((tm, tn), lambda j, i: (i, j)),
        scratch_shapes=[pltpu.VMEM((k, tn), BF16)],
        compiler_params=_params(("parallel", "arbitrary")),
        name="in_proj",
    )(h, w_in, col_scale)


def _out_proj_kernel(m_ref, w_ref, x_ref, o_ref):
    o_ref[...] = x_ref[...] + jnp.dot(m_ref[...], w_ref[...], preferred_element_type=F32)


def _out_proj(mixed, w, x, tm=512, tn=1024):
    m, k = mixed.shape
    n = w.shape[1]
    return pl.pallas_call(
        _out_proj_kernel,
        out_shape=jax.ShapeDtypeStruct((m, n), F32),
        grid=(n // tn, m // tm),
        in_specs=[pl.BlockSpec((tm, k), lambda j, i: (i, 0)),
                  pl.BlockSpec((k, tn), lambda j, i: (0, j)),
                  pl.BlockSpec((tm, tn), lambda j, i: (i, j))],
        out_specs=pl.BlockSpec((tm, tn), lambda j, i: (i, j)),
        compiler_params=_params(("parallel", "parallel")),
        name="out_proj",
    )(mixed, w, x)


def _rope_tile(x, cos, sin_signed):
    lane = lax.broadcasted_iota(jnp.int32, x.shape, 1)
    partner = jnp.where((lane % 64) < 32, pltpu.roll(x, 96, 1), pltpu.roll(x, 32, 1))
    return x * cos + partner * sin_signed


def _rope_tables(seq):
    inv = ROPE_THETA ** (-jnp.arange(0, 64, 2, dtype=F32) / 64)
    pos = jnp.arange(seq)

    def ang(p):
        return p.astype(F32)[:, None] * inv[None, :]

    def halves(a):
        c, s = jnp.cos(a), jnp.sin(a)
        return jnp.concatenate([c, c], -1), jnp.concatenate([-s, s], -1)

    ct, st = halves(ang(pos))
    zeros = jnp.zeros_like(ct)
    cos_a = jnp.concatenate([ct, zeros], -1)
    sin_a = jnp.concatenate([st, zeros], -1)
    cr, sr = halves(ang(pos // GRID_W))
    cc, sc = halves(ang(pos % GRID_W))
    cos_c = jnp.concatenate([cr, cc], -1)
    sin_c = jnp.concatenate([sr, sc], -1)
    return cos_a, sin_a, cos_c, sin_c


def _prep_a_kernel(lat_ref, gq_ref, gkv_ref, wuq_ref, wukv_ref, cos_ref, sin_ref, q_ref, k_ref, v_ref):
    lat = lat_ref[...].astype(F32)
    cq = lat[:, 0:A_Q_LORA]
    ckv = lat[:, A_Q_LORA:A_Q_LORA + A_KV_LORA]
    kpe = lat[:, A_Q_LORA + A_KV_LORA:A_Q_LORA + A_KV_LORA + 128]

    def rms(x, g):
        return x * lax.rsqrt(jnp.mean(x * x, axis=-1, keepdims=True) + EPS) * g

    qa = jnp.dot(rms(cq, gq_ref[...]).astype(BF16), wuq_ref[...], preferred_element_type=F32)
    kva = jnp.dot(rms(ckv, gkv_ref[...]).astype(BF16), wukv_ref[...], preferred_element_type=F32)
    cos = cos_ref[...]
    sin = sin_ref[...]
    scale = (A_NOPE + A_ROPE) ** -0.5 * LOG2E
    kpe_r = _rope_tile(kpe, cos, sin).astype(BF16)
    for h in range(A_HEADS):
        c0 = h * A_QK
        q_ref[:, c0:c0 + 128] = (qa[:, c0:c0 + 128] * scale).astype(BF16)
        q_ref[:, c0 + 128:c0 + 256] = (_rope_tile(qa[:, c0 + 128:c0 + 256], cos, sin) * scale).astype(BF16)
        k_ref[:, c0:c0 + 128] = kva[:, c0:c0 + 128].astype(BF16)
        k_ref[:, c0 + 128:c0 + 256] = kpe_r
        v_ref[:, h * A_V:(h + 1) * A_V] = kva[:, c0 + 128:c0 + 256].astype(BF16)


def _prep_a(lat, g_qa, g_kva, w_uq_p, w_ukv, cos_a, sin_a, seq, tm=512):
    m = lat.shape[0]
    nsb = seq // tm
    return pl.pallas_call(
        _prep_a_kernel,
        out_shape=(jax.ShapeDtypeStruct((m, A_HEADS * A_QK), BF16),
                   jax.ShapeDtypeStruct((m, A_HEADS * A_QK), BF16),
                   jax.ShapeDtypeStruct((m, W_A), BF16)),
        grid=(m // tm,),
        in_specs=[pl.BlockSpec((tm, LAT_W), lambda i: (i, 0)),
                  pl.BlockSpec((1, A_Q_LORA), lambda i: (0, 0)),
                  pl.BlockSpec((1, A_KV_LORA), lambda i: (0, 0)),
                  pl.BlockSpec((A_Q_LORA, A_HEADS * A_QK), lambda i: (0, 0)),
                  pl.BlockSpec((A_KV_LORA, A_HEADS * A_QK), lambda i: (0, 0)),
                  pl.BlockSpec((tm, 128), lambda i: (i % nsb, 0)),
                  pl.BlockSpec((tm, 128), lambda i: (i % nsb, 0))],
        out_specs=(pl.BlockSpec((tm, A_HEADS * A_QK), lambda i: (i, 0)),
                   pl.BlockSpec((tm, A_HEADS * A_QK), lambda i: (i, 0)),
                   pl.BlockSpec((tm, W_A), lambda i: (i, 0))),
        compiler_params=_params(("parallel",)),
        name="prep_a",
    )(lat, g_qa.reshape(1, -1), g_kva.reshape(1, -1), w_uq_p, w_ukv, cos_a, sin_a)


def _prep_c_kernel(q_in_ref, k_in_ref, gq_ref, gk_ref, cos_ref, sin_ref, q_ref, k_ref):
    cos = cos_ref[...]
    sin = sin_ref[...]

    def norm_rope(x, g):
        x = x.astype(F32)
        xn = x * lax.rsqrt(jnp.mean(x * x, axis=-1, keepdims=True) + EPS) * g
        return _rope_tile(xn, cos, sin)

    scale = HEAD_DIM ** -0.5 * LOG2E
    for h in range(q_ref.shape[1] // HEAD_DIM):
        sl = slice(h * HEAD_DIM, (h + 1) * HEAD_DIM)
        q_ref[:, sl] = (norm_rope(q_in_ref[:, sl], gq_ref[...]) * scale).astype(BF16)

    @pl.when(pl.program_id(1) == 0)
    def _():
        for h in range(C_KV_HEADS):
            sl = slice(h * HEAD_DIM, (h + 1) * HEAD_DIM)
            k_ref[:, sl] = norm_rope(k_in_ref[:, sl], gk_ref[...]).astype(BF16)


def _prep_c(proj, g_qn, g_kn, cos_c, sin_c, seq, tm=512, qw=512):
    m = proj.shape[0]
    nsb = seq // tm
    kw = C_KV_HEADS * HEAD_DIM
    return pl.pallas_call(
        _prep_c_kernel,
        out_shape=(jax.ShapeDtypeStruct((m, W_C), BF16),
                   jax.ShapeDtypeStruct((m, kw), BF16)),
        grid=(m // tm, W_C // qw),
        in_specs=[pl.BlockSpec((tm, qw), lambda i, c: (i, CQ_OFF // qw + c)),
                  pl.BlockSpec((tm, kw), lambda i, c: (i, CK_OFF // kw)),
                  pl.BlockSpec((1, HEAD_DIM), lambda i, c: (0, 0)),
                  pl.BlockSpec((1, HEAD_DIM), lambda i, c: (0, 0)),
                  pl.BlockSpec((tm, 128), lambda i, c: (i % nsb, 0)),
                  pl.BlockSpec((tm, 128), lambda i, c: (i % nsb, 0))],
        out_specs=(pl.BlockSpec((tm, qw), lambda i, c: (i, c)),
                   pl.BlockSpec((tm, kw), lambda i, c: (i, 0))),
        compiler_params=_params(("parallel", "arbitrary")),
        name="prep_c",
    )(proj, proj, g_qn.reshape(1, -1), g_kn.reshape(1, -1), cos_c, sin_c)


def _dense_attn_kernel(q_ref, k_ref, v_ref, o_ref, *, groups, dq, dv, chunk):
    tq = q_ref.shape[1]
    s_len = k_ref.shape[1]
    if groups == 1:
        q = q_ref[0]
    else:
        q = jnp.concatenate([q_ref[0, :, g * dq:(g + 1) * dq] for g in range(groups)], axis=0)
    m = l = acc = None
    for j in range(s_len // chunk):
        kj = k_ref[0, j * chunk:(j + 1) * chunk, :]
        vj = v_ref[0, j * chunk:(j + 1) * chunk, :]
        st = lax.dot_general(kj, q, _NT, preferred_element_type=F32)
        cmax = jnp.max(st, axis=0, keepdims=True)
        m_new = cmax if m is None else jnp.maximum(m, cmax)
        p = jnp.exp2(st - m_new)
        psum = jnp.sum(p, axis=0, keepdims=True)
        pv = lax.dot_general(vj, p.astype(BF16), _TN, preferred_element_type=F32)
        if m is None:
            l, acc = psum, pv
        else:
            alpha = jnp.exp2(m - m_new)
            l = alpha * l + psum
            acc = alpha * acc + pv
        m = m_new
    o_t = acc / l
    for g in range(groups):
        o_ref[0, :, g * dv:(g + 1) * dv] = o_t[:, g * tq:(g + 1) * tq].T.astype(o_ref.dtype)


def _dense_attn(q, k, v, *, kv_heads, groups, dq, dv, k_col0, v_col0, tq, name, chunk=512):
    b, s, _ = q.shape
    kern = functools.partial(_dense_attn_kernel, groups=groups, dq=dq, dv=dv, chunk=chunk)
    return pl.pallas_call(
        kern,
        out_shape=jax.ShapeDtypeStruct((b, s, kv_heads * groups * dv), BF16),
        grid=(b, kv_heads, s // tq),
        in_specs=[pl.BlockSpec((1, tq, groups * dq), lambda bi, hi, qi: (bi, qi, hi)),
                  pl.BlockSpec((1, s, dq), lambda bi, hi, qi: (bi, 0, k_col0 + hi)),
                  pl.BlockSpec((1, s, dv), lambda bi, hi, qi: (bi, 0, v_col0 + hi))],
        out_specs=pl.BlockSpec((1, tq, groups * dv), lambda bi, hi, qi: (bi, qi, hi)),
        compiler_params=_params(("parallel", "parallel", "parallel")),
        name=name,
    )(q, k, v)


def _rel_bucket_np(rel):
    nb = N_BUCKETS // 2
    max_exact = nb // 2
    ret = np.where(rel > 0, nb, 0)
    n = np.abs(rel)
    nf = np.maximum(n, 1).astype(np.float32)
    large = max_exact + (np.log(nf / max_exact) / math.log(REL_MAX_DIST / max_exact)
                         * (nb - max_exact)).astype(np.int32)
    large = np.minimum(large, nb - 1)
    return ret + np.where(n < max_exact, n, large)


def _band_bias(tab, dil, half, tq):
    tw = tq + 2 * half
    n = tq + tw - 1
    heads = tab.shape[1]
    tiles = []
    for delta in (0, -half, -2 * half):
        rel = delta + np.arange(n) - (tq - 1)
        inside = np.abs(rel) <= half
        vals = tab[_rel_bucket_np(dil * rel)].astype(F32) * LOG2E
        diag = jnp.where(inside[:, None], vals, NEG).T
        flat = jnp.tile(diag, (1, tq + 1))[:, :tq * (n + 1)]
        tiles.append(flat.reshape(heads, tq, n + 1)[:, ::-1, :tw])
    return jnp.swapaxes(jnp.stack(tiles, 0), -1, -2)


def _banded_kernel(*refs, head_groups, n_sub, half, length, with_sink, with_lse):
    q_ref, k_ref, v_ref, bias_ref = refs[:4]
    pos = 4
    sink_ref = None
    if with_sink:
        sink_ref = refs[pos]
        pos += 1
    o_ref = refs[pos]
    lse_ref = refs[pos + 1] if with_lse else None

    tq = BAND_TQ
    tw = tq + 2 * half
    nblk = length // tq
    for sb in range(n_sub):
        blk = pl.program_id(2) * n_sub + sb
        ks = pl.multiple_of(jnp.clip(blk * tq - half, 0, length - tw), 64)
        var = jnp.where(blk == 0, 0, jnp.where(blk == nblk - 1, 2, 1))
        rows = slice(sb * tq, (sb + 1) * tq)
        lse_rows = []
        for kvh, heads in head_groups:
            n = len(heads) * tq
            cols = slice(kvh * HEAD_DIM, (kvh + 1) * HEAD_DIM)
            kw = k_ref[0, 0, pl.ds(ks, tw), cols]
            vw = v_ref[0, 0, pl.ds(ks, tw), cols]
            qs = [q_ref[0, 0, rows, h * HEAD_DIM:(h + 1) * HEAD_DIM] for h in heads]
            q = qs[0] if len(heads) == 1 else jnp.concatenate(qs, axis=0)
            bias = [bias_ref[var, h] for h in heads]
            bias = bias[0] if len(heads) == 1 else jnp.concatenate(bias, axis=1)
            st = lax.dot_general(kw, q, _NT, preferred_element_type=F32) + bias
            m = jnp.max(st, axis=0, keepdims=True)
            if with_sink:
                sink = sink_ref[0]
                m = jnp.maximum(m, sink)
            p = jnp.exp2(st - m)
            l = jnp.sum(p, axis=0, keepdims=True)
            if with_sink:
                l = l + jnp.exp2(sink - m)
            o_t = lax.dot_general(vw, p.astype(BF16), _TN, preferred_element_type=F32) / l
            for gi, h in enumerate(heads):
                o_ref[0, 0, rows, h * HEAD_DIM:(h + 1) * HEAD_DIM] = (
                    o_t[:, gi * tq:(gi + 1) * tq].T.astype(o_ref.dtype))
            if with_lse:
                lse = (m + jnp.log2(l)) * LN2
                for gi in range(len(heads)):
                    lse_rows.append(jnp.broadcast_to(lse[:, gi * tq:(gi + 1) * tq], (LSE_LANES, tq)))
        if with_lse:
            lse_ref[0, 0, rows, :] = jnp.concatenate(lse_rows, axis=0).T


def _dilated_group(proj3, bias, j, dil):
    b, s, w = proj3.shape
    length = s // dil
    half = B_PAIRS[j][0] // (2 * dil)
    n_sub = min(4, length // BAND_TQ)
    tqs = n_sub * BAND_TQ
    cq, ck, cv = (BQ_OFF + j * W_BG, BK_OFF + j * W_BG, BV_OFF + j * W_BG)
    if dil == 1:
        src = proj3.reshape(b, 1, s, w)
        bq, bk, bv = cq // W_BG, ck // W_BG, cv // W_BG
    else:
        src = jnp.concatenate([proj3[:, :, c:c + W_BG] for c in (cq, ck, cv)], axis=-1)
        src = src.reshape(b, length, dil, 3 * W_BG).transpose(0, 2, 1, 3)
        bq, bk, bv = 0, 1, 2
    groups = tuple((h, (h,)) for h in range(B_HEADS_PER_PAIR))
    kern = functools.partial(_banded_kernel, head_groups=groups, n_sub=n_sub, half=half, length=length,
                             with_sink=False, with_lse=True)
    o, lse = pl.pallas_call(
        kern,
        out_shape=(jax.ShapeDtypeStruct((b, dil, length, W_BG), BF16),
                   jax.ShapeDtypeStruct((b, dil, length, 128), F32)),
        grid=(b, dil, length // tqs),
        in_specs=[pl.BlockSpec((1, 1, tqs, W_BG), lambda bi, r, i: (bi, r, i, bq)),
                  pl.BlockSpec((1, 1, length, W_BG), lambda bi, r, i: (bi, r, 0, bk)),
                  pl.BlockSpec((1, 1, length, W_BG), lambda bi, r, i: (bi, r, 0, bv)),
                  pl.BlockSpec(bias.shape, lambda bi, r, i: (0, 0, 0, 0))],
        out_specs=(pl.BlockSpec((1, 1, tqs, W_BG), lambda bi, r, i: (bi, r, i, 0)),
                   pl.BlockSpec((1, 1, tqs, 128), lambda bi, r, i: (bi, r, i, 0))),
        compiler_params=_params(("parallel", "parallel", "parallel")),
        name=f"dilated_attn_{dil}",
    )(src, src, src, bias)
    if dil > 1:
        o = o.transpose(0, 2, 1, 3)
        lse = lse.transpose(0, 2, 1, 3)
    return o.reshape(b, s, W_BG), lse.reshape(b, s, 128)


def _window_attn(proj3, bias, sink_rows, n_sub=4):
    b, s, w = proj3.shape
    g = D_HEADS // D_KV_HEADS
    gw = g * HEAD_DIM
    tqs = n_sub * BAND_TQ
    kern = functools.partial(_banded_kernel, head_groups=((0, tuple(range(g))),), n_sub=n_sub, half=D_WINDOW,
                             length=s, with_sink=True, with_lse=False)
    src = proj3.reshape(b, 1, s, w)
    o = pl.pallas_call(
        kern,
        out_shape=jax.ShapeDtypeStruct((b, 1, s, W_D), BF16),
        grid=(b, D_KV_HEADS, s // tqs),
        in_specs=[pl.BlockSpec((1, 1, tqs, gw), lambda bi, hi, i: (bi, 0, i, DQ_OFF // gw + hi)),
                  pl.BlockSpec((1, 1, s, HEAD_DIM), lambda bi, hi, i: (bi, 0, 0, DK_OFF // HEAD_DIM + hi)),
                  pl.BlockSpec((1, 1, s, HEAD_DIM), lambda bi, hi, i: (bi, 0, 0, DV_OFF // HEAD_DIM + hi)),
                  pl.BlockSpec((3, g) + bias.shape[2:], lambda bi, hi, i: (0, hi, 0, 0)),
                  pl.BlockSpec((1, 1, g * BAND_TQ), lambda bi, hi, i: (hi, 0, 0))],
        out_specs=pl.BlockSpec((1, 1, tqs, gw), lambda bi, hi, i: (bi, 0, i, hi)),
        compiler_params=_params(("parallel", "parallel", "parallel")),
        name="window_attn",
    )(src, src, src, bias, sink_rows)
    return o.reshape(b, s, W_D)


def _mix_kernel(g0_ref, g1_ref, g2_ref, oa_ref, ob0_ref, ob1_ref, ob2_ref, l0_ref, l1_ref, l2_ref,
                oc_ref, od_ref, out_ref):
    gate_refs = (g0_ref, g1_ref, g2_ref)

    def silu(col0, width):
        blk, off = divmod(col0, GATE_BLK)
        assert off + width <= GATE_BLK
        g = gate_refs[blk][:, off:off + width].astype(F32)
        return g * (1.0 / (1.0 + jnp.exp(-g)))

    def plain(o_ref, col0, width, piece=512):
        for c in range(0, width, piece):
            out_ref[:, col0 + c:col0 + c + piece] = (
                o_ref[:, c:c + piece].astype(F32) * silu(col0 + c, piece)).astype(BF16)

    plain(oa_ref, 0, W_A)
    l0, l1, l2 = l0_ref[...], l1_ref[...], l2_ref[...]
    mx = jnp.maximum(jnp.maximum(l0, l1), l2)
    e0, e1, e2 = jnp.exp(l0 - mx), jnp.exp(l1 - mx), jnp.exp(l2 - mx)
    inv = 1.0 / (e0 + e1 + e2)
    for j, (ob_ref, e) in enumerate(((ob0_ref, e0), (ob1_ref, e1), (ob2_ref, e2))):
        alpha = e * inv
        for h in range(B_HEADS_PER_PAIR):
            c0 = W_A + j * W_BG + h * HEAD_DIM
            a_h = alpha[:, h * LSE_LANES:h * LSE_LANES + 1]
            o_h = ob_ref[:, h * HEAD_DIM:(h + 1) * HEAD_DIM].astype(F32)
            out_ref[:, c0:c0 + HEAD_DIM] = (o_h * a_h * silu(c0, HEAD_DIM)).astype(BF16)
    plain(oc_ref, W_A + W_B, W_C)
    plain(od_ref, W_A + W_B + W_C, W_D)


def _mix(proj, o_a, o_b, lse_b, o_c, o_d, tm=512):
    m = proj.shape[0]
    row = lambda w: pl.BlockSpec((tm, w), lambda i: (i, 0))
    gate = lambda c: pl.BlockSpec((tm, GATE_BLK), lambda i: (i, GATE_OFF // GATE_BLK + c))
    return pl.pallas_call(
        _mix_kernel,
        out_shape=jax.ShapeDtypeStruct((m, MIX_WIDTH), BF16),
        grid=(m // tm,),
        in_specs=[gate(0), gate(1), gate(2),
                  row(W_A), row(W_BG), row(W_BG), row(W_BG), row(128), row(128), row(128), row(W_C), row(W_D)],
        out_specs=row(MIX_WIDTH),
        compiler_params=_params(("parallel",)),
        name="gate_mix",
    )(proj, proj, proj, o_a, *o_b, *lse_b, o_c, o_d)


def _layout_w_lat(w):
    return jnp.pad(w[:, :LAT_USED], ((0, 0), (0, LAT_W - LAT_USED))).astype(BF16)


def _main_col_scale():
    qscale = np.ones((1, MAIN_W), np.float32)
    qscale[:, BQ_OFF:BQ_OFF + W_B] = HEAD_DIM ** -0.5 * LOG2E
    qscale[:, DQ_OFF:DQ_OFF + W_D] = HEAD_DIM ** -0.5 * LOG2E
    return jnp.asarray(qscale)


def _layout_w_uq(w):
    w = w.reshape(A_Q_LORA, A_HEADS, A_NOPE + A_ROPE)
    w = jnp.pad(w, ((0, 0), (0, 0), (0, A_QK - A_NOPE - A_ROPE)))
    return w.reshape(A_Q_LORA, A_HEADS * A_QK).astype(BF16)


def _layer(x2, b, s, layer, g_attn, w_in, g_qa, g_kva, w_uq, w_ukv, g_qn, g_kn, sinks, w_out, rel_bias, tables):
    cos_a, sin_a, cos_c, sin_c = tables
    h = _rmsnorm(x2, g_attn, BF16)
    proj = _in_proj_main(h, w_in, layer, _main_col_scale())
    lat = _in_proj(h, _layout_w_lat(w_in[layer]), 512, LAT_W, "in_proj_latent")
    proj3 = proj.reshape(b, s, MAIN_W)

    q_a, k_a, v_a = _prep_a(lat, g_qa, g_kva, _layout_w_uq(w_uq), w_ukv.astype(BF16), cos_a, sin_a, s)
    o_a = _dense_attn(q_a.reshape(b, s, -1), k_a.reshape(b, s, -1), v_a.reshape(b, s, -1),
                      kv_heads=A_HEADS, groups=1, dq=A_QK, dv=A_V, k_col0=0, v_col0=0, tq=2048, name="attn_a")

    o_b, lse_b = [], []
    for j, (win, dil) in enumerate(B_PAIRS):
        tab = rel_bias[:, j * B_HEADS_PER_PAIR:(j + 1) * B_HEADS_PER_PAIR]
        bias = _band_bias(tab, dil, win // (2 * dil), BAND_TQ)
        o, lse = _dilated_group(proj3, bias, j, dil)
        o_b.append(o.reshape(b * s, W_BG))
        lse_b.append(lse.reshape(b * s, 128))

    q_c, k_c = _prep_c(proj, g_qn, g_kn, cos_c, sin_c, s)
    o_c = _dense_attn(q_c.reshape(b, s, -1), k_c.reshape(b, s, -1), proj3,
                      kv_heads=C_KV_HEADS, groups=C_HEADS // C_KV_HEADS, dq=HEAD_DIM, dv=HEAD_DIM,
                      k_col0=0, v_col0=CV_OFF // HEAD_DIM, tq=512, name="attn_c")

    g = D_HEADS // D_KV_HEADS
    bias_d = _band_bias(rel_bias[:, B_HEADS:], 1, D_WINDOW, BAND_TQ)
    sink_rows = jnp.repeat(sinks.astype(F32) * LOG2E, BAND_TQ).reshape(D_KV_HEADS, 1, g * BAND_TQ)
    o_d = _window_attn(proj3, bias_d, sink_rows)

    mixed = _mix(proj, o_a.reshape(b * s, W_A), o_b, lse_b, o_c.reshape(b * s, W_C), o_d.reshape(b * s, W_D))
    return _out_proj(mixed, w_out.astype(BF16), x2)


def kernel(x, g_attn, w_in, g_qa, g_kva, w_uq, w_ukv, g_qn, g_kn, sinks, w_out, rel_bias, g_final):
    b, s, d = x.shape
    depth = w_in.shape[0]
    tables = _rope_tables(s)
    x2 = x.reshape(b * s, d)
    for l in range(depth):
        x2 = _layer(x2, b, s, l, g_attn[l], w_in, g_qa[l], g_kva[l], w_uq[l], w_ukv[l], g_qn[l], g_kn[l],
                    sinks[l], w_out[l], rel_bias, tables)
    return _rmsnorm(x2, g_final, F32).reshape(b, s, d)
```

```python
import functools
import math

import numpy as np
import jax
import jax.numpy as jnp
from jax import lax
from jax.experimental import pallas as pl
from jax.experimental.pallas import tpu as pltpu

D_MODEL = 4096
HEAD_DIM = 128
A_HEADS = 8
A_Q_LORA = 768
A_KV_LORA = 512
A_NOPE = 128
A_ROPE = 64
A_V = 128
B_PAIRS = ((128, 1), (512, 4), (2048, 16))
B_HEADS_PER_PAIR = 4
B_HEADS = B_HEADS_PER_PAIR * len(B_PAIRS)
C_HEADS = 8
C_KV_HEADS = 2
D_HEADS = 8
D_KV_HEADS = 2
D_WINDOW = 128
GRID_W = 64
ROPE_THETA = 10000.0
N_BUCKETS = 32
REL_MAX_DIST = 1024
EPS = 1e-6
NEG = -1e30
LOG2E = math.log2(math.e)
LN2 = math.log(2.0)

W_A = A_HEADS * A_V
W_B = B_HEADS * HEAD_DIM
W_C = C_HEADS * HEAD_DIM
W_D = D_HEADS * HEAD_DIM
W_BG = B_HEADS_PER_PAIR * HEAD_DIM
MIX_WIDTH = W_A + W_B + W_C + W_D

LAT_USED = A_Q_LORA + A_KV_LORA + A_ROPE
LAT_W = 1536
BQ_OFF = 0
BK_OFF = BQ_OFF + W_B
BV_OFF = BK_OFF + W_B
CQ_OFF = BV_OFF + W_B
CK_OFF = CQ_OFF + W_C
CV_OFF = CK_OFF + C_KV_HEADS * HEAD_DIM
DQ_OFF = CV_OFF + C_KV_HEADS * HEAD_DIM
DK_OFF = DQ_OFF + W_D
DV_OFF = DK_OFF + D_KV_HEADS * HEAD_DIM
GATE_OFF = DV_OFF + D_KV_HEADS * HEAD_DIM
MAIN_W = GATE_OFF + MIX_WIDTH
GATE_BLK = 1536

A_QK = 256
LSE_LANES = 32
BAND_TQ = 128

VMEM_LIMIT = 56 * 1024 * 1024

F32 = jnp.float32
BF16 = jnp.bfloat16

_NT = (((1,), (1,)), ((), ()))
_TN = (((0,), (0,)), ((), ()))


def _params(sem, vmem=VMEM_LIMIT):
    return pltpu.CompilerParams(dimension_semantics=sem, vmem_limit_bytes=vmem)


def _rmsnorm_kernel(x_ref, g_ref, o_ref):
    x = x_ref[...]
    ms = jnp.mean(x * x, axis=-1, keepdims=True)
    o_ref[...] = (x * lax.rsqrt(ms + EPS) * g_ref[...]).astype(o_ref.dtype)


def _rmsnorm(x, g, out_dtype, tm=256):
    m, d = x.shape
    return pl.pallas_call(
        _rmsnorm_kernel,
        out_shape=jax.ShapeDtypeStruct((m, d), out_dtype),
        grid=(m // tm,),
        in_specs=[pl.BlockSpec((tm, d), lambda i: (i, 0)),
                  pl.BlockSpec((1, d), lambda i: (0, 0))],
        out_specs=pl.BlockSpec((tm, d), lambda i: (i, 0)),
        compiler_params=_params(("parallel",)),
        name="rmsnorm",
    )(x, g.reshape(1, d))


def _matmul_kernel(x_ref, w_ref, o_ref):
    o_ref[...] = jnp.dot(x_ref[...], w_ref[...], preferred_element_type=F32).astype(o_ref.dtype)


def _in_proj(h, w, tm, tn, name):
    m, k = h.shape
    n = w.shape[1]
    return pl.pallas_call(
        _matmul_kernel,
        out_shape=jax.ShapeDtypeStruct((m, n), BF16),
        grid=(m // tm, n // tn),
        in_specs=[pl.BlockSpec((tm, k), lambda i, j: (i, 0)),
                  pl.BlockSpec((k, tn), lambda i, j: (0, j))],
        out_specs=pl.BlockSpec((tm, tn), lambda i, j: (i, j)),
        compiler_params=_params(("parallel", "parallel")),
        name=name,
    )(h, w)


def _w_main_kernel(a_ref, b_ref, c_ref, s_ref, o_ref, *, shift):
    win = jnp.concatenate([a_ref[0], b_ref[0], c_ref[0]], axis=1)
    o_ref[...] = (win[:, shift:shift + o_ref.shape[1]] * s_ref[...]).astype(BF16)


def _w_main(w_in, layer, col_scale, tr=2048, tn=512):
    k = w_in.shape[1]
    shift = LAT_USED % 128
    win0 = LAT_USED - shift
    assert win0 % 256 == 0 and tn == 512
    a0, c0 = win0 // 256, (win0 + tn) // 128
    kern = functools.partial(_w_main_kernel, shift=shift)
    return pl.pallas_call(
        kern,
        out_shape=jax.ShapeDtypeStruct((k, MAIN_W), BF16),
        grid=(MAIN_W // tn, k // tr),
        in_specs=[pl.BlockSpec((1, tr, 256), lambda j, i: (layer, i, a0 + 2 * j)),
                  pl.BlockSpec((1, tr, 256), lambda j, i: (layer, i, a0 + 2 * j + 1)),
                  pl.BlockSpec((1, tr, 128), lambda j, i: (layer, i, c0 + 4 * j)),
                  pl.BlockSpec((1, tn), lambda j, i: (0, j))],
        out_specs=pl.BlockSpec((tr, tn), lambda j, i: (i, j)),
        compiler_params=_params(("parallel", "parallel")),
        name="w_main_layout",
    )(w_in, w_in, w_in, col_scale)


def _out_proj_kernel(m_ref, w_ref, x_ref, o_ref):
    o_ref[...] = x_ref[...] + jnp.dot(m_ref[...], w_ref[...], preferred_element_type=F32)


def _out_proj(mixed, w, x, tm=512, tn=1024):
    m, k = mixed.shape
    n = w.shape[1]
    return pl.pallas_call(
        _out_proj_kernel,
        out_shape=jax.ShapeDtypeStruct((m, n), F32),
        grid=(n // tn, m // tm),
        in_specs=[pl.BlockSpec((tm, k), lambda j, i: (i, 0)),
                  pl.BlockSpec((k, tn), lambda j, i: (0, j)),
                  pl.BlockSpec((tm, tn), lambda j, i: (i, j))],
        out_specs=pl.BlockSpec((tm, tn), lambda j, i: (i, j)),
        compiler_params=_params(("parallel", "parallel")),
        name="out_proj",
    )(mixed, w, x)


def _rope_tile(x, cos, sin_signed):
    lane = lax.broadcasted_iota(jnp.int32, x.shape, 1)
    partner = jnp.where((lane % 64) < 32, pltpu.roll(x, 96, 1), pltpu.roll(x, 32, 1))
    return x * cos + partner * sin_signed


def _rope_tables(seq):
    inv = ROPE_THETA ** (-jnp.arange(0, 64, 2, dtype=F32) / 64)
    pos = jnp.arange(seq)

    def ang(p):
        return p.astype(F32)[:, None] * inv[None, :]

    def halves(a):
        c, s = jnp.cos(a), jnp.sin(a)
        return jnp.concatenate([c, c], -1), jnp.concatenate([-s, s], -1)

    ct, st = halves(ang(pos))
    zeros = jnp.zeros_like(ct)
    cos_a = jnp.concatenate([ct, zeros], -1)
    sin_a = jnp.concatenate([st, zeros], -1)
    cr, sr = halves(ang(pos // GRID_W))
    cc, sc = halves(ang(pos % GRID_W))
    cos_c = jnp.concatenate([cr, cc], -1)
    sin_c = jnp.concatenate([sr, sc], -1)
    return cos_a, sin_a, cos_c, sin_c


def _prep_a_kernel(lat_ref, gq_ref, gkv_ref, wuq_ref, wukv_ref, cos_ref, sin_ref, q_ref, k_ref, v_ref):
    lat = lat_ref[...].astype(F32)
    cq = lat[:, 0:A_Q_LORA]
    ckv = lat[:, A_Q_LORA:A_Q_LORA + A_KV_LORA]
    kpe = lat[:, A_Q_LORA + A_KV_LORA:A_Q_LORA + A_KV_LORA + 128]

    def rms(x, g):
        return x * lax.rsqrt(jnp.mean(x * x, axis=-1, keepdims=True) + EPS) * g

    qa = jnp.dot(rms(cq, gq_ref[...]).astype(BF16), wuq_ref[...], preferred_element_type=F32)
    kva = jnp.dot(rms(ckv, gkv_ref[...]).astype(BF16), wukv_ref[...], preferred_element_type=F32)
    cos = cos_ref[...]
    sin = sin_ref[...]
    scale = (A_NOPE + A_ROPE) ** -0.5 * LOG2E
    kpe_r = _rope_tile(kpe, cos, sin).astype(BF16)
    for h in range(A_HEADS):
        c0 = h * A_QK
        q_ref[:, c0:c0 + 128] = (qa[:, c0:c0 + 128] * scale).astype(BF16)
        q_ref[:, c0 + 128:c0 + 256] = (_rope_tile(qa[:, c0 + 128:c0 + 256], cos, sin) * scale).astype(BF16)
        k_ref[:, c0:c0 + 128] = kva[:, c0:c0 + 128].astype(BF16)
        k_ref[:, c0 + 128:c0 + 256] = kpe_r
        v_ref[:, h * A_V:(h + 1) * A_V] = kva[:, c0 + 128:c0 + 256].astype(BF16)


def _prep_a(lat, g_qa, g_kva, w_uq_p, w_ukv, cos_a, sin_a, seq, tm=512):
    m = lat.shape[0]
    nsb = seq // tm
    return pl.pallas_call(
        _prep_a_kernel,
        out_shape=(jax.ShapeDtypeStruct((m, A_HEADS * A_QK), BF16),
                   jax.ShapeDtypeStruct((m, A_HEADS * A_QK), BF16),
                   jax.ShapeDtypeStruct((m, W_A), BF16)),
        grid=(m // tm,),
        in_specs=[pl.BlockSpec((tm, LAT_W), lambda i: (i, 0)),
                  pl.BlockSpec((1, A_Q_LORA), lambda i: (0, 0)),
                  pl.BlockSpec((1, A_KV_LORA), lambda i: (0, 0)),
                  pl.BlockSpec((A_Q_LORA, A_HEADS * A_QK), lambda i: (0, 0)),
                  pl.BlockSpec((A_KV_LORA, A_HEADS * A_QK), lambda i: (0, 0)),
                  pl.BlockSpec((tm, 128), lambda i: (i % nsb, 0)),
                  pl.BlockSpec((tm, 128), lambda i: (i % nsb, 0))],
        out_specs=(pl.BlockSpec((tm, A_HEADS * A_QK), lambda i: (i, 0)),
                   pl.BlockSpec((tm, A_HEADS * A_QK), lambda i: (i, 0)),
                   pl.BlockSpec((tm, W_A), lambda i: (i, 0))),
        compiler_params=_params(("parallel",)),
        name="prep_a",
    )(lat, g_qa.reshape(1, -1), g_kva.reshape(1, -1), w_uq_p, w_ukv, cos_a, sin_a)


def _prep_c_kernel(q_in_ref, k_in_ref, gq_ref, gk_ref, cos_ref, sin_ref, q_ref, k_ref):
    cos = cos_ref[...]
    sin = sin_ref[...]

    def norm_rope(x, g):
        x = x.astype(F32)
        xn = x * lax.rsqrt(jnp.mean(x * x, axis=-1, keepdims=True) + EPS) * g
        return _rope_tile(xn, cos, sin)

    scale = HEAD_DIM ** -0.5 * LOG2E
    for h in range(q_ref.shape[1] // HEAD_DIM):
        sl = slice(h * HEAD_DIM, (h + 1) * HEAD_DIM)
        q_ref[:, sl] = (norm_rope(q_in_ref[:, sl], gq_ref[...]) * scale).astype(BF16)

    @pl.when(pl.program_id(1) == 0)
    def _():
        for h in range(C_KV_HEADS):
            sl = slice(h * HEAD_DIM, (h + 1) * HEAD_DIM)
            k_ref[:, sl] = norm_rope(k_in_ref[:, sl], gk_ref[...]).astype(BF16)


def _prep_c(proj, g_qn, g_kn, cos_c, sin_c, seq, tm=512, qw=512):
    m = proj.shape[0]
    nsb = seq // tm
    kw = C_KV_HEADS * HEAD_DIM
    return pl.pallas_call(
        _prep_c_kernel,
        out_shape=(jax.ShapeDtypeStruct((m, W_C), BF16),
                   jax.ShapeDtypeStruct((m, kw), BF16)),
        grid=(m // tm, W_C // qw),
        in_specs=[pl.BlockSpec((tm, qw), lambda i, c: (i, CQ_OFF // qw + c)),
                  pl.BlockSpec((tm, kw), lambda i, c: (i, CK_OFF // kw)),
                  pl.BlockSpec((1, HEAD_DIM), lambda i, c: (0, 0)),
                  pl.BlockSpec((1, HEAD_DIM), lambda i, c: (0, 0)),
                  pl.BlockSpec((tm, 128), lambda i, c: (i % nsb, 0)),
                  pl.BlockSpec((tm, 128), lambda i, c: (i % nsb, 0))],
        out_specs=(pl.BlockSpec((tm, qw), lambda i, c: (i, c)),
                   pl.BlockSpec((tm, kw), lambda i, c: (i, 0))),
        compiler_params=_params(("parallel", "arbitrary")),
        name="prep_c",
    )(proj, proj, g_qn.reshape(1, -1), g_kn.reshape(1, -1), cos_c, sin_c)


def _dense_attn_kernel(q_ref, k_ref, v_ref, o_ref, *, groups, dq, dv, chunk):
    tq = q_ref.shape[1]
    s_len = k_ref.shape[1]
    if groups == 1:
        q = q_ref[0]
    else:
        q = jnp.concatenate([q_ref[0, :, g * dq:(g + 1) * dq] for g in range(groups)], axis=0)
    m = l = acc = None
    for j in range(s_len // chunk):
        kj = k_ref[0, j * chunk:(j + 1) * chunk, :]
        vj = v_ref[0, j * chunk:(j + 1) * chunk, :]
        st = lax.dot_general(kj, q, _NT, preferred_element_type=F32)
        cmax = jnp.max(st, axis=0, keepdims=True)
        m_new = cmax if m is None else jnp.maximum(m, cmax)
        p = jnp.exp2(st - m_new)
        psum = jnp.sum(p, axis=0, keepdims=True)
        pv = lax.dot_general(vj, p.astype(BF16), _TN, preferred_element_type=F32)
        if m is None:
            l, acc = psum, pv
        else:
            alpha = jnp.exp2(m - m_new)
            l = alpha * l + psum
            acc = alpha * acc + pv
        m = m_new
    o_t = acc / l
    for g in range(groups):
        o_ref[0, :, g * dv:(g + 1) * dv] = o_t[:, g * tq:(g + 1) * tq].T.astype(o_ref.dtype)


def _dense_attn(q, k, v, *, kv_heads, groups, dq, dv, k_col0, v_col0, tq, name, chunk=512):
    b, s, _ = q.shape
    kern = functools.partial(_dense_attn_kernel, groups=groups, dq=dq, dv=dv, chunk=chunk)
    return pl.pallas_call(
        kern,
        out_shape=jax.ShapeDtypeStruct((b, s, kv_heads * groups * dv), BF16),
        grid=(b, kv_heads, s // tq),
        in_specs=[pl.BlockSpec((1, tq, groups * dq), lambda bi, hi, qi: (bi, qi, hi)),
                  pl.BlockSpec((1, s, dq), lambda bi, hi, qi: (bi, 0, k_col0 + hi)),
                  pl.BlockSpec((1, s, dv), lambda bi, hi, qi: (bi, 0, v_col0 + hi))],
        out_specs=pl.BlockSpec((1, tq, groups * dv), lambda bi, hi, qi: (bi, qi, hi)),
        compiler_params=_params(("parallel", "parallel", "parallel")),
        name=name,
    )(q, k, v)


def _rel_bucket_np(rel):
    nb = N_BUCKETS // 2
    max_exact = nb // 2
    ret = np.where(rel > 0, nb, 0)
    n = np.abs(rel)
    nf = np.maximum(n, 1).astype(np.float32)
    large = max_exact + (np.log(nf / max_exact) / math.log(REL_MAX_DIST / max_exact)
                         * (nb - max_exact)).astype(np.int32)
    large = np.minimum(large, nb - 1)
    return ret + np.where(n < max_exact, n, large)


def _band_bias(tab, dil, half, tq):
    tw = tq + 2 * half
    n = tq + tw - 1
    heads = tab.shape[1]
    tiles = []
    for delta in (0, -half, -2 * half):
        rel = delta + np.arange(n) - (tq - 1)
        inside = np.abs(rel) <= half
        vals = tab[_rel_bucket_np(dil * rel)].astype(F32) * LOG2E
        diag = jnp.where(inside[:, None], vals, NEG).T
        flat = jnp.tile(diag, (1, tq + 1))[:, :tq * (n + 1)]
        tiles.append(flat.reshape(heads, tq, n + 1)[:, ::-1, :tw])
    return jnp.swapaxes(jnp.stack(tiles, 0), -1, -2)


def _banded_kernel(*refs, head_groups, n_sub, half, length, with_sink, with_lse):
    q_ref, k_ref, v_ref, bias_ref = refs[:4]
    pos = 4
    sink_ref = None
    if with_sink:
        sink_ref = refs[pos]
        pos += 1
    o_ref = refs[pos]
    lse_ref = refs[pos + 1] if with_lse else None

    tq = BAND_TQ
    tw = tq + 2 * half
    nblk = length // tq
    for sb in range(n_sub):
        blk = pl.program_id(2) * n_sub + sb
        ks = pl.multiple_of(jnp.clip(blk * tq - half, 0, length - tw), 64)
        var = jnp.where(blk == 0, 0, jnp.where(blk == nblk - 1, 2, 1))
        rows = slice(sb * tq, (sb + 1) * tq)
        lse_rows = []
        for kvh, heads in head_groups:
            n = len(heads) * tq
            cols = slice(kvh * HEAD_DIM, (kvh + 1) * HEAD_DIM)
            kw = k_ref[0, 0, pl.ds(ks, tw), cols]
            vw = v_ref[0, 0, pl.ds(ks, tw), cols]
            qs = [q_ref[0, 0, rows, h * HEAD_DIM:(h + 1) * HEAD_DIM] for h in heads]
            q = qs[0] if len(heads) == 1 else jnp.concatenate(qs, axis=0)
            bias = [bias_ref[var, h] for h in heads]
            bias = bias[0] if len(heads) == 1 else jnp.concatenate(bias, axis=1)
            st = lax.dot_general(kw, q, _NT, preferred_element_type=F32) + bias
            m = jnp.max(st, axis=0, keepdims=True)
            if with_sink:
                sink = sink_ref[0]
                m = jnp.maximum(m, sink)
            p = jnp.exp2(st - m)
            l = jnp.sum(p, axis=0, keepdims=True)
            if with_sink:
                l = l + jnp.exp2(sink - m)
            o_t = lax.dot_general(vw, p.astype(BF16), _TN, preferred_element_type=F32) / l
            for gi, h in enumerate(heads):
                o_ref[0, 0, rows, h * HEAD_DIM:(h + 1) * HEAD_DIM] = (
                    o_t[:, gi * tq:(gi + 1) * tq].T.astype(o_ref.dtype))
            if with_lse:
                lse = (m + jnp.log2(l)) * LN2
                for gi in range(len(heads)):
                    lse_rows.append(jnp.broadcast_to(lse[:, gi * tq:(gi + 1) * tq], (LSE_LANES, tq)))
        if with_lse:
            lse_ref[0, 0, rows, :] = jnp.concatenate(lse_rows, axis=0).T


def _deinterleave_kernel(x_ref, o_ref, scr_ref, *, dil):
    n = scr_ref.shape[1] // dil
    for c in range(scr_ref.shape[0]):
        lanes = slice(c * 128, (c + 1) * 128)
        scr_ref[c] = x_ref[0, :, lanes].astype(F32)
        for r in range(dil):
            o_ref[0, r, :, lanes] = scr_ref[c, pl.ds(r, n, stride=dil), :].astype(o_ref.dtype)


def _deinterleave_qkv(proj3, j, dil, tm=1024):
    b, s, _ = proj3.shape
    col0 = BQ_OFF // W_BG + j
    step = (BK_OFF - BQ_OFF) // W_BG
    return pl.pallas_call(
        functools.partial(_deinterleave_kernel, dil=dil),
        out_shape=jax.ShapeDtypeStruct((b, dil, s // dil, 3 * W_BG), BF16),
        grid=(b, s // tm, 3),
        in_specs=[pl.BlockSpec((1, tm, W_BG), lambda bi, i, c: (bi, i, col0 + step * c))],
        out_specs=pl.BlockSpec((1, dil, tm // dil, W_BG), lambda bi, i, c: (bi, 0, i, c)),
        scratch_shapes=[pltpu.VMEM((W_BG // 128, tm, 128), F32)],
        compiler_params=_params(("parallel", "parallel", "parallel")),
        name=f"deinterleave_{dil}",
    )(proj3)


def _dilated_group(proj3, bias, j, dil):
    b, s, w = proj3.shape
    length = s // dil
    half = B_PAIRS[j][0] // (2 * dil)
    n_sub = min(4, length // BAND_TQ)
    tqs = n_sub * BAND_TQ
    cq, ck, cv = (BQ_OFF + j * W_BG, BK_OFF + j * W_BG, BV_OFF + j * W_BG)
    if dil == 1:
        src = proj3.reshape(b, 1, s, w)
        bq, bk, bv = cq // W_BG, ck // W_BG, cv // W_BG
    else:
        src = _deinterleave_qkv(proj3, j, dil)
        bq, bk, bv = 0, 1, 2
    groups = tuple((h, (h,)) for h in range(B_HEADS_PER_PAIR))
    kern = functools.partial(_banded_kernel, head_groups=groups, n_sub=n_sub, half=half, length=length,
                             with_sink=False, with_lse=True)
    o, lse = pl.pallas_call(
        kern,
        out_shape=(jax.ShapeDtypeStruct((b, dil, length, W_BG), BF16),
                   jax.ShapeDtypeStruct((b, dil, length, 128), F32)),
        grid=(b, dil, length // tqs),
        in_specs=[pl.BlockSpec((1, 1, tqs, W_BG), lambda bi, r, i: (bi, r, i, bq)),
                  pl.BlockSpec((1, 1, length, W_BG), lambda bi, r, i: (bi, r, 0, bk)),
                  pl.BlockSpec((1, 1, length, W_BG), lambda bi, r, i: (bi, r, 0, bv)),
                  pl.BlockSpec(bias.shape, lambda bi, r, i: (0, 0, 0, 0))],
        out_specs=(pl.BlockSpec((1, 1, tqs, W_BG), lambda bi, r, i: (bi, r, i, 0)),
                   pl.BlockSpec((1, 1, tqs, 128), lambda bi, r, i: (bi, r, i, 0))),
        compiler_params=_params(("parallel", "parallel", "parallel")),
        name=f"dilated_attn_{dil}",
    )(src, src, src, bias)
    return o, lse


def _window_attn(proj3, bias, sink_rows, n_sub=4):
    b, s, w = proj3.shape
    g = D_HEADS // D_KV_HEADS
    gw = g * HEAD_DIM
    tqs = n_sub * BAND_TQ
    kern = functools.partial(_banded_kernel, head_groups=((0, tuple(range(g))),), n_sub=n_sub, half=D_WINDOW,
                             length=s, with_sink=True, with_lse=False)
    src = proj3.reshape(b, 1, s, w)
    o = pl.pallas_call(
        kern,
        out_shape=jax.ShapeDtypeStruct((b, 1, s, W_D), BF16),
        grid=(b, D_KV_HEADS, s // tqs),
        in_specs=[pl.BlockSpec((1, 1, tqs, gw), lambda bi, hi, i: (bi, 0, i, DQ_OFF // gw + hi)),
                  pl.BlockSpec((1, 1, s, HEAD_DIM), lambda bi, hi, i: (bi, 0, 0, DK_OFF // HEAD_DIM + hi)),
                  pl.BlockSpec((1, 1, s, HEAD_DIM), lambda bi, hi, i: (bi, 0, 0, DV_OFF // HEAD_DIM + hi)),
                  pl.BlockSpec((3, g) + bias.shape[2:], lambda bi, hi, i: (0, hi, 0, 0)),
                  pl.BlockSpec((1, 1, g * BAND_TQ), lambda bi, hi, i: (hi, 0, 0))],
        out_specs=pl.BlockSpec((1, 1, tqs, gw), lambda bi, hi, i: (bi, 0, i, hi)),
        compiler_params=_params(("parallel", "parallel", "parallel")),
        name="window_attn",
    )(src, src, src, bias, sink_rows)
    return o.reshape(b, s, W_D)


def _mix_kernel(g0_ref, g1_ref, g2_ref, oa_ref, ob0_ref, ob1_ref, ob2_ref, l0_ref, l1_ref, l2_ref,
                oc_ref, od_ref, out_ref, ob_scr, lse_scr):
    gate_refs = (g0_ref, g1_ref, g2_ref)

    def interleave(src_ref, scr_ref):
        dil, n, w = src_ref.shape[1:]
        tiles = []
        for c in range(w // 128):
            lanes = slice(c * 128, (c + 1) * 128)
            if dil == 1:
                tiles.append(src_ref[0, 0, :, lanes].astype(F32))
                continue
            for r in range(dil):
                scr_ref[c, pl.ds(r, n, stride=dil), :] = src_ref[0, r, :, lanes].astype(F32)
            tiles.append(scr_ref[c])
        return tiles

    def silu(col0, width):
        blk, off = divmod(col0, GATE_BLK)
        assert off + width <= GATE_BLK
        g = gate_refs[blk][:, off:off + width].astype(F32)
        return g * (1.0 / (1.0 + jnp.exp(-g)))

    def plain(o_ref, col0, width, piece=512):
        for c in range(0, width, piece):
            out_ref[:, col0 + c:col0 + c + piece] = (
                o_ref[:, c:c + piece].astype(F32) * silu(col0 + c, piece)).astype(BF16)

    plain(oa_ref, 0, W_A)
    l0, l1, l2 = (interleave(l_ref, lse_scr)[0] for l_ref in (l0_ref, l1_ref, l2_ref))
    mx = jnp.maximum(jnp.maximum(l0, l1), l2)
    e0, e1, e2 = jnp.exp(l0 - mx), jnp.exp(l1 - mx), jnp.exp(l2 - mx)
    inv = 1.0 / (e0 + e1 + e2)
    for j, (ob_ref, e) in enumerate(((ob0_ref, e0), (ob1_ref, e1), (ob2_ref, e2))):
        alpha = e * inv
        ob = interleave(ob_ref, ob_scr)
        for h in range(B_HEADS_PER_PAIR):
            c0 = W_A + j * W_BG + h * HEAD_DIM
            a_h = alpha[:, h * LSE_LANES:h * LSE_LANES + 1]
            out_ref[:, c0:c0 + HEAD_DIM] = (ob[h] * a_h * silu(c0, HEAD_DIM)).astype(BF16)
    plain(oc_ref, W_A + W_B, W_C)
    plain(od_ref, W_A + W_B + W_C, W_D)


def _mix(proj, o_a, o_b, lse_b, o_c, o_d, seq, tm=512):
    m = proj.shape[0]
    nsb = seq // tm
    row = lambda w: pl.BlockSpec((tm, w), lambda i: (i, 0))
    gate = lambda c: pl.BlockSpec((tm, GATE_BLK), lambda i: (i, GATE_OFF // GATE_BLK + c))

    def classes(a):
        dil, w = a.shape[1], a.shape[3]
        return pl.BlockSpec((1, dil, tm // dil, w), lambda i: (i // nsb, 0, i % nsb, 0))

    return pl.pallas_call(
        _mix_kernel,
        out_shape=jax.ShapeDtypeStruct((m, MIX_WIDTH), BF16),
        grid=(m // tm,),
        in_specs=[gate(0), gate(1), gate(2), row(W_A), *[classes(a) for a in o_b], *[classes(a) for a in lse_b],
                  row(W_C), row(W_D)],
        out_specs=row(MIX_WIDTH),
        scratch_shapes=[pltpu.VMEM((W_BG // 128, tm, 128), F32), pltpu.VMEM((1, tm, 128), F32)],
        compiler_params=_params(("parallel",)),
        name="gate_mix",
    )(proj, proj, proj, o_a, *o_b, *lse_b, o_c, o_d)


def _w_lat_kernel(w_ref, o_ref):
    col = lax.broadcasted_iota(jnp.int32, o_ref.shape, 1)
    o_ref[...] = jnp.where(col < LAT_USED, w_ref[0], 0.0).astype(BF16)


def _w_lat(w_in, layer, tr=1024):
    k = w_in.shape[1]
    return pl.pallas_call(
        _w_lat_kernel,
        out_shape=jax.ShapeDtypeStruct((k, LAT_W), BF16),
        grid=(k // tr,),
        in_specs=[pl.BlockSpec((1, tr, LAT_W), lambda i: (layer, i, 0))],
        out_specs=pl.BlockSpec((tr, LAT_W), lambda i: (i, 0)),
        compiler_params=_params(("parallel",)),
        name="w_lat_layout",
    )(w_in)


def _main_col_scale():
    qscale = np.ones((1, MAIN_W), np.float32)
    qscale[:, BQ_OFF:BQ_OFF + W_B] = HEAD_DIM ** -0.5 * LOG2E
    qscale[:, DQ_OFF:DQ_OFF + W_D] = HEAD_DIM ** -0.5 * LOG2E
    return jnp.asarray(qscale)


def _layout_w_uq(w):
    w = w.reshape(A_Q_LORA, A_HEADS, A_NOPE + A_ROPE)
    w = jnp.pad(w, ((0, 0), (0, 0), (0, A_QK - A_NOPE - A_ROPE)))
    return w.reshape(A_Q_LORA, A_HEADS * A_QK).astype(BF16)


def _layer(x2, b, s, layer, g_attn, w_in, g_qa, g_kva, w_uq, w_ukv, g_qn, g_kn, sinks, w_out, rel_bias, tables):
    cos_a, sin_a, cos_c, sin_c = tables
    h = _rmsnorm(x2, g_attn, BF16)
    proj = _in_proj(h, _w_main(w_in, layer, _main_col_scale()), 1024, 768, "in_proj")
    lat = _in_proj(h, _w_lat(w_in, layer), 512, LAT_W, "in_proj_latent")
    proj3 = proj.reshape(b, s, MAIN_W)

    q_a, k_a, v_a = _prep_a(lat, g_qa, g_kva, _layout_w_uq(w_uq), w_ukv.astype(BF16), cos_a, sin_a, s)
    o_a = _dense_attn(q_a.reshape(b, s, -1), k_a.reshape(b, s, -1), v_a.reshape(b, s, -1),
                      kv_heads=A_HEADS, groups=1, dq=A_QK, dv=A_V, k_col0=0, v_col0=0, tq=2048, name="attn_a")

    o_b, lse_b = [], []
    for j, (win, dil) in enumerate(B_PAIRS):
        tab = rel_bias[:, j * B_HEADS_PER_PAIR:(j + 1) * B_HEADS_PER_PAIR]
        bias = _band_bias(tab, dil, win // (2 * dil), BAND_TQ)
        o, lse = _dilated_group(proj3, bias, j, dil)
        o_b.append(o)
        lse_b.append(lse)

    q_c, k_c = _prep_c(proj, g_qn, g_kn, cos_c, sin_c, s)
    o_c = _dense_attn(q_c.reshape(b, s, -1), k_c.reshape(b, s, -1), proj3,
                      kv_heads=C_KV_HEADS, groups=C_HEADS // C_KV_HEADS, dq=HEAD_DIM, dv=HEAD_DIM,
                      k_col0=0, v_col0=CV_OFF // HEAD_DIM, tq=512, name="attn_c")

    g = D_HEADS // D_KV_HEADS
    bias_d = _band_bias(rel_bias[:, B_HEADS:], 1, D_WINDOW, BAND_TQ)
    sink_rows = jnp.repeat(sinks.astype(F32) * LOG2E, BAND_TQ).reshape(D_KV_HEADS, 1, g * BAND_TQ)
    o_d = _window_attn(proj3, bias_d, sink_rows)

    mixed = _mix(proj, o_a.reshape(b * s, W_A), o_b, lse_b, o_c.reshape(b * s, W_C), o_d.reshape(b * s, W_D), s)
    return _out_proj(mixed, w_out.astype(BF16), x2)


def kernel(x, g_attn, w_in, g_qa, g_kva, w_uq, w_ukv, g_qn, g_kn, sinks, w_out, rel_bias, g_final):
    b, s, d = x.shape
    depth = w_in.shape[0]
    tables = _rope_tables(s)
    x2 = x.reshape(b * s, d)
    for l in range(depth):
        x2 = _layer(x2, b, s, l, g_attn[l], w_in, g_qa[l], g_kva[l], w_uq[l], w_ukv[l], g_qn[l], g_kn[l],
                    sinks[l], w_out[l], rel_bias, tables)
    return _rmsnorm(x2, g_final, F32).reshape(b, s, d)
```

```python
import functools
import math

import numpy as np
import jax
import jax.numpy as jnp
from jax import lax
from jax.experimental import pallas as pl
from jax.experimental.pallas import tpu as pltpu

D_MODEL = 4096
HEAD_DIM = 128
A_HEADS = 8
A_Q_LORA = 768
A_KV_LORA = 512
A_NOPE = 128
A_ROPE = 64
A_V = 128
B_PAIRS = ((128, 1), (512, 4), (2048, 16))
B_HEADS_PER_PAIR = 4
B_HEADS = B_HEADS_PER_PAIR * len(B_PAIRS)
C_HEADS = 8
C_KV_HEADS = 2
D_HEADS = 8
D_KV_HEADS = 2
D_WINDOW = 128
GRID_W = 64
ROPE_THETA = 10000.0
N_BUCKETS = 32
REL_MAX_DIST = 1024
EPS = 1e-6
NEG = -1e30
LOG2E = math.log2(math.e)
LN2 = math.log(2.0)

W_A = A_HEADS * A_V
W_B = B_HEADS * HEAD_DIM
W_C = C_HEADS * HEAD_DIM
W_D = D_HEADS * HEAD_DIM
W_BG = B_HEADS_PER_PAIR * HEAD_DIM
MIX_WIDTH = W_A + W_B + W_C + W_D

LAT_USED = A_Q_LORA + A_KV_LORA + A_ROPE
LAT_W = 1536
BQ_OFF = 0
BK_OFF = BQ_OFF + W_B
BV_OFF = BK_OFF + W_B
CQ_OFF = BV_OFF + W_B
CK_OFF = CQ_OFF + W_C
CV_OFF = CK_OFF + C_KV_HEADS * HEAD_DIM
DQ_OFF = CV_OFF + C_KV_HEADS * HEAD_DIM
DK_OFF = DQ_OFF + W_D
DV_OFF = DK_OFF + D_KV_HEADS * HEAD_DIM
GATE_OFF = DV_OFF + D_KV_HEADS * HEAD_DIM
MAIN_W = GATE_OFF + MIX_WIDTH
GATE_BLK = 1536

A_QK = 256
LSE_LANES = 32
BAND_TQ = 128

VMEM_LIMIT = 56 * 1024 * 1024

F32 = jnp.float32
BF16 = jnp.bfloat16

_NT = (((1,), (1,)), ((), ()))
_TN = (((0,), (0,)), ((), ()))


def _params(sem, vmem=VMEM_LIMIT):
    return pltpu.CompilerParams(dimension_semantics=sem, vmem_limit_bytes=vmem)


def _rmsnorm_kernel(x_ref, g_ref, o_ref):
    x = x_ref[...]
    ms = jnp.mean(x * x, axis=-1, keepdims=True)
    o_ref[...] = (x * lax.rsqrt(ms + EPS) * g_ref[...]).astype(o_ref.dtype)


def _rmsnorm(x, g, out_dtype, tm=256):
    m, d = x.shape
    return pl.pallas_call(
        _rmsnorm_kernel,
        out_shape=jax.ShapeDtypeStruct((m, d), out_dtype),
        grid=(m // tm,),
        in_specs=[pl.BlockSpec((tm, d), lambda i: (i, 0)),
                  pl.BlockSpec((1, d), lambda i: (0, 0))],
        out_specs=pl.BlockSpec((tm, d), lambda i: (i, 0)),
        compiler_params=_params(("parallel",)),
        name="rmsnorm",
    )(x, g.reshape(1, d))


W_ROWS = 192


def _w_rows_kernel(w_ref, o_ref, *, n_valid, scaled):
    row = pl.program_id(0) * W_ROWS + lax.broadcasted_iota(jnp.int32, o_ref.shape, 0)
    w = w_ref[0]
    for start, stop, factor in scaled:
        w = jnp.where((row >= start) & (row < stop), w * factor, w)
    if n_valid is not None:
        w = jnp.where(row < n_valid, w, 0.0)
    o_ref[...] = w.astype(BF16)


def _w_rows(w_t, layer, row0, n_rows, n_valid=None, scaled=(), name="w_layout"):
    assert row0 % W_ROWS == 0 and n_rows % W_ROWS == 0 and row0 + n_rows <= w_t.shape[1]
    k = w_t.shape[2]
    kern = functools.partial(_w_rows_kernel, n_valid=n_valid, scaled=scaled)
    return pl.pallas_call(
        kern,
        out_shape=jax.ShapeDtypeStruct((n_rows, k), BF16),
        grid=(n_rows // W_ROWS,),
        in_specs=[pl.BlockSpec((1, W_ROWS, k), lambda j: (layer, row0 // W_ROWS + j, 0))],
        out_specs=pl.BlockSpec((W_ROWS, k), lambda j: (j, 0)),
        compiler_params=_params(("parallel",)),
        name=name,
    )(w_t)


def _matmul_nt_kernel(x_ref, w_ref, o_ref):
    o_ref[...] = lax.dot_general(x_ref[...], w_ref[...], _NT, preferred_element_type=F32).astype(o_ref.dtype)


def _in_proj_nt(h, w_rows, tm, tn, name):
    m, k = h.shape
    n = w_rows.shape[0]
    return pl.pallas_call(
        _matmul_nt_kernel,
        out_shape=jax.ShapeDtypeStruct((m, n), BF16),
        grid=(m // tm, n // tn),
        in_specs=[pl.BlockSpec((tm, k), lambda i, j: (i, 0)),
                  pl.BlockSpec((tn, k), lambda i, j: (j, 0))],
        out_specs=pl.BlockSpec((tm, tn), lambda i, j: (i, j)),
        compiler_params=_params(("parallel", "parallel")),
        name=name,
    )(h, w_rows)


def _out_proj_kernel(m_ref, w_ref, x_ref, o_ref):
    o_ref[...] = x_ref[...] + jnp.dot(m_ref[...], w_ref[0], preferred_element_type=F32)


def _out_proj(mixed, w, layer, x, tm=512, tn=1024):
    m, k = mixed.shape
    n = w.shape[2]
    return pl.pallas_call(
        _out_proj_kernel,
        out_shape=jax.ShapeDtypeStruct((m, n), F32),
        grid=(n // tn, m // tm),
        in_specs=[pl.BlockSpec((tm, k), lambda j, i: (i, 0)),
                  pl.BlockSpec((1, k, tn), lambda j, i: (layer, 0, j)),
                  pl.BlockSpec((tm, tn), lambda j, i: (i, j))],
        out_specs=pl.BlockSpec((tm, tn), lambda j, i: (i, j)),
        compiler_params=_params(("parallel", "parallel")),
        name="out_proj",
    )(mixed, w, x)


def _rope_tile(x, cos, sin_signed):
    lane = lax.broadcasted_iota(jnp.int32, x.shape, 1)
    partner = jnp.where((lane % 64) < 32, pltpu.roll(x, 96, 1), pltpu.roll(x, 32, 1))
    return x * cos + partner * sin_signed


def _rope_tables(seq):
    inv = ROPE_THETA ** (-jnp.arange(0, 64, 2, dtype=F32) / 64)
    pos = jnp.arange(seq)

    def ang(p):
        return p.astype(F32)[:, None] * inv[None, :]

    def halves(a):
        c, s = jnp.cos(a), jnp.sin(a)
        return jnp.concatenate([c, c], -1), jnp.concatenate([-s, s], -1)

    ct, st = halves(ang(pos))
    zeros = jnp.zeros_like(ct)
    cos_a = jnp.concatenate([ct, zeros], -1)
    sin_a = jnp.concatenate([st, zeros], -1)
    cr, sr = halves(ang(pos // GRID_W))
    cc, sc = halves(ang(pos % GRID_W))
    cos_c = jnp.concatenate([cr, cc], -1)
    sin_c = jnp.concatenate([sr, sc], -1)
    return cos_a, sin_a, cos_c, sin_c


def _prep_a_kernel(lat_ref, gq_ref, gkv_ref, wuq_ref, wukv_ref, cos_ref, sin_ref, q_ref, k_ref, v_ref):
    lat = lat_ref[...].astype(F32)
    cq = lat[:, 0:A_Q_LORA]
    ckv = lat[:, A_Q_LORA:A_Q_LORA + A_KV_LORA]
    kpe = lat[:, A_Q_LORA + A_KV_LORA:A_Q_LORA + A_KV_LORA + 128]

    def rms(x, g):
        return x * lax.rsqrt(jnp.mean(x * x, axis=-1, keepdims=True) + EPS) * g

    qa = jnp.dot(rms(cq, gq_ref[...]).astype(BF16), wuq_ref[...], preferred_element_type=F32)
    kva = jnp.dot(rms(ckv, gkv_ref[...]).astype(BF16), wukv_ref[...], preferred_element_type=F32)
    cos = cos_ref[...]
    sin = sin_ref[...]
    scale = (A_NOPE + A_ROPE) ** -0.5 * LOG2E
    kpe_r = _rope_tile(kpe, cos, sin).astype(BF16)
    for h in range(A_HEADS):
        c0 = h * A_QK
        q_ref[:, c0:c0 + 128] = (qa[:, c0:c0 + 128] * scale).astype(BF16)
        q_ref[:, c0 + 128:c0 + 256] = (_rope_tile(qa[:, c0 + 128:c0 + 256], cos, sin) * scale).astype(BF16)
        k_ref[:, c0:c0 + 128] = kva[:, c0:c0 + 128].astype(BF16)
        k_ref[:, c0 + 128:c0 + 256] = kpe_r
        v_ref[:, h * A_V:(h + 1) * A_V] = kva[:, c0 + 128:c0 + 256].astype(BF16)


def _prep_a(lat, g_qa, g_kva, w_uq_p, w_ukv, cos_a, sin_a, seq, tm=512):
    m = lat.shape[0]
    nsb = seq // tm
    return pl.pallas_call(
        _prep_a_kernel,
        out_shape=(jax.ShapeDtypeStruct((m, A_HEADS * A_QK), BF16),
                   jax.ShapeDtypeStruct((m, A_HEADS * A_QK), BF16),
                   jax.ShapeDtypeStruct((m, W_A), BF16)),
        grid=(m // tm,),
        in_specs=[pl.BlockSpec((tm, LAT_W), lambda i: (i, 0)),
                  pl.BlockSpec((1, A_Q_LORA), lambda i: (0, 0)),
                  pl.BlockSpec((1, A_KV_LORA), lambda i: (0, 0)),
                  pl.BlockSpec((A_Q_LORA, A_HEADS * A_QK), lambda i: (0, 0)),
                  pl.BlockSpec((A_KV_LORA, A_HEADS * A_QK), lambda i: (0, 0)),
                  pl.BlockSpec((tm, 128), lambda i: (i % nsb, 0)),
                  pl.BlockSpec((tm, 128), lambda i: (i % nsb, 0))],
        out_specs=(pl.BlockSpec((tm, A_HEADS * A_QK), lambda i: (i, 0)),
                   pl.BlockSpec((tm, A_HEADS * A_QK), lambda i: (i, 0)),
                   pl.BlockSpec((tm, W_A), lambda i: (i, 0))),
        compiler_params=_params(("parallel",)),
        name="prep_a",
    )(lat, g_qa.reshape(1, -1), g_kva.reshape(1, -1), w_uq_p, w_ukv, cos_a, sin_a)


def _prep_c_kernel(q_in_ref, k_in_ref, gq_ref, gk_ref, cos_ref, sin_ref, q_ref, k_ref):
    cos = cos_ref[...]
    sin = sin_ref[...]

    def norm_rope(x, g):
        x = x.astype(F32)
        xn = x * lax.rsqrt(jnp.mean(x * x, axis=-1, keepdims=True) + EPS) * g
        return _rope_tile(xn, cos, sin)

    scale = HEAD_DIM ** -0.5 * LOG2E
    for h in range(q_ref.shape[1] // HEAD_DIM):
        sl = slice(h * HEAD_DIM, (h + 1) * HEAD_DIM)
        q_ref[:, sl] = (norm_rope(q_in_ref[:, sl], gq_ref[...]) * scale).astype(BF16)

    @pl.when(pl.program_id(1) == 0)
    def _():
        for h in range(C_KV_HEADS):
            sl = slice(h * HEAD_DIM, (h + 1) * HEAD_DIM)
            k_ref[:, sl] = norm_rope(k_in_ref[:, sl], gk_ref[...]).astype(BF16)


def _prep_c(proj, g_qn, g_kn, cos_c, sin_c, seq, tm=512, qw=512):
    m = proj.shape[0]
    nsb = seq // tm
    kw = C_KV_HEADS * HEAD_DIM
    return pl.pallas_call(
        _prep_c_kernel,
        out_shape=(jax.ShapeDtypeStruct((m, W_C), BF16),
                   jax.ShapeDtypeStruct((m, kw), BF16)),
        grid=(m // tm, W_C // qw),
        in_specs=[pl.BlockSpec((tm, qw), lambda i, c: (i, CQ_OFF // qw + c)),
                  pl.BlockSpec((tm, kw), lambda i, c: (i, CK_OFF // kw)),
                  pl.BlockSpec((1, HEAD_DIM), lambda i, c: (0, 0)),
                  pl.BlockSpec((1, HEAD_DIM), lambda i, c: (0, 0)),
                  pl.BlockSpec((tm, 128), lambda i, c: (i % nsb, 0)),
                  pl.BlockSpec((tm, 128), lambda i, c: (i % nsb, 0))],
        out_specs=(pl.BlockSpec((tm, qw), lambda i, c: (i, c)),
                   pl.BlockSpec((tm, kw), lambda i, c: (i, 0))),
        compiler_params=_params(("parallel", "arbitrary")),
        name="prep_c",
    )(proj, proj, g_qn.reshape(1, -1), g_kn.reshape(1, -1), cos_c, sin_c)


def _dense_attn_kernel(q_ref, k_ref, v_ref, o_ref, *, groups, dq, dv, chunk):
    tq = q_ref.shape[1]
    s_len = k_ref.shape[1]
    if groups == 1:
        q = q_ref[0]
    else:
        q = jnp.concatenate([q_ref[0, :, g * dq:(g + 1) * dq] for g in range(groups)], axis=0)
    m = l = acc = None
    for j in range(s_len // chunk):
        kj = k_ref[0, j * chunk:(j + 1) * chunk, :]
        vj = v_ref[0, j * chunk:(j + 1) * chunk, :]
        st = lax.dot_general(kj, q, _NT, preferred_element_type=F32)
        cmax = jnp.max(st, axis=0, keepdims=True)
        m_new = cmax if m is None else jnp.maximum(m, cmax)
        p = jnp.exp2(st - m_new)
        psum = jnp.sum(p, axis=0, keepdims=True)
        pv = lax.dot_general(vj, p.astype(BF16), _TN, preferred_element_type=F32)
        if m is None:
            l, acc = psum, pv
        else:
            alpha = jnp.exp2(m - m_new)
            l = alpha * l + psum
            acc = alpha * acc + pv
        m = m_new
    o_t = acc / l
    for g in range(groups):
        o_ref[0, :, g * dv:(g + 1) * dv] = o_t[:, g * tq:(g + 1) * tq].T.astype(o_ref.dtype)


def _dense_attn(q, k, v, *, kv_heads, groups, dq, dv, k_col0, v_col0, tq, name, chunk=512):
    b, s, _ = q.shape
    kern = functools.partial(_dense_attn_kernel, groups=groups, dq=dq, dv=dv, chunk=chunk)
    return pl.pallas_call(
        kern,
        out_shape=jax.ShapeDtypeStruct((b, s, kv_heads * groups * dv), BF16),
        grid=(b, kv_heads, s // tq),
        in_specs=[pl.BlockSpec((1, tq, groups * dq), lambda bi, hi, qi: (bi, qi, hi)),
                  pl.BlockSpec((1, s, dq), lambda bi, hi, qi: (bi, 0, k_col0 + hi)),
                  pl.BlockSpec((1, s, dv), lambda bi, hi, qi: (bi, 0, v_col0 + hi))],
        out_specs=pl.BlockSpec((1, tq, groups * dv), lambda bi, hi, qi: (bi, qi, hi)),
        compiler_params=_params(("parallel", "parallel", "parallel")),
        name=name,
    )(q, k, v)


def _rel_bucket_np(rel):
    nb = N_BUCKETS // 2
    max_exact = nb // 2
    ret = np.where(rel > 0, nb, 0)
    n = np.abs(rel)
    nf = np.maximum(n, 1).astype(np.float32)
    large = max_exact + (np.log(nf / max_exact) / math.log(REL_MAX_DIST / max_exact)
                         * (nb - max_exact)).astype(np.int32)
    large = np.minimum(large, nb - 1)
    return ret + np.where(n < max_exact, n, large)


def _band_bias(tab, dil, half, tq):
    tw = tq + 2 * half
    n = tq + tw - 1
    heads = tab.shape[1]
    tiles = []
    for delta in (0, -half, -2 * half):
        rel = delta + np.arange(n) - (tq - 1)
        inside = np.abs(rel) <= half
        vals = tab[_rel_bucket_np(dil * rel)].astype(F32) * LOG2E
        diag = jnp.where(inside[:, None], vals, NEG).T
        flat = jnp.tile(diag, (1, tq + 1))[:, :tq * (n + 1)]
        tiles.append(flat.reshape(heads, tq, n + 1)[:, ::-1, :tw])
    return jnp.swapaxes(jnp.stack(tiles, 0), -1, -2)


def _banded_kernel(*refs, head_groups, n_sub, half, length, with_sink, with_lse):
    q_ref, k_ref, v_ref, bias_ref = refs[:4]
    pos = 4
    sink_ref = None
    if with_sink:
        sink_ref = refs[pos]
        pos += 1
    o_ref = refs[pos]
    lse_ref = refs[pos + 1] if with_lse else None

    tq = BAND_TQ
    tw = tq + 2 * half
    nblk = length // tq
    for sb in range(n_sub):
        blk = pl.program_id(2) * n_sub + sb
        ks = pl.multiple_of(jnp.clip(blk * tq - half, 0, length - tw), 64)
        var = jnp.where(blk == 0, 0, jnp.where(blk == nblk - 1, 2, 1))
        rows = slice(sb * tq, (sb + 1) * tq)
        lse_rows = []
        for kvh, heads in head_groups:
            n = len(heads) * tq
            cols = slice(kvh * HEAD_DIM, (kvh + 1) * HEAD_DIM)
            kw = k_ref[0, 0, pl.ds(ks, tw), cols]
            vw = v_ref[0, 0, pl.ds(ks, tw), cols]
            qs = [q_ref[0, 0, rows, h * HEAD_DIM:(h + 1) * HEAD_DIM] for h in heads]
            q = qs[0] if len(heads) == 1 else jnp.concatenate(qs, axis=0)
            bias = [bias_ref[var, h] for h in heads]
            bias = bias[0] if len(heads) == 1 else jnp.concatenate(bias, axis=1)
            st = lax.dot_general(kw, q, _NT, preferred_element_type=F32) + bias
            m = jnp.max(st, axis=0, keepdims=True)
            if with_sink:
                sink = sink_ref[0]
                m = jnp.maximum(m, sink)
            p = jnp.exp2(st - m)
            l = jnp.sum(p, axis=0, keepdims=True)
            if with_sink:
                l = l + jnp.exp2(sink - m)
            o_t = lax.dot_general(vw, p.astype(BF16), _TN, preferred_element_type=F32) / l
            for gi, h in enumerate(heads):
                o_ref[0, 0, rows, h * HEAD_DIM:(h + 1) * HEAD_DIM] = (
                    o_t[:, gi * tq:(gi + 1) * tq].T.astype(o_ref.dtype))
            if with_lse:
                lse = (m + jnp.log2(l)) * LN2
                for gi in range(len(heads)):
                    lse_rows.append(jnp.broadcast_to(lse[:, gi * tq:(gi + 1) * tq], (LSE_LANES, tq)))
        if with_lse:
            lse_ref[0, 0, rows, :] = jnp.concatenate(lse_rows, axis=0).T


def _deinterleave_kernel(x_ref, o_ref, scr_ref, *, dil):
    n = scr_ref.shape[1] // dil
    for c in range(scr_ref.shape[0]):
        lanes = slice(c * 128, (c + 1) * 128)
        scr_ref[c] = x_ref[0, :, lanes].astype(F32)
        for r in range(dil):
            o_ref[0, r, :, lanes] = scr_ref[c, pl.ds(r, n, stride=dil), :].astype(o_ref.dtype)


def _deinterleave_qkv(proj3, j, dil, tm=1024):
    b, s, _ = proj3.shape
    col0 = BQ_OFF // W_BG + j
    step = (BK_OFF - BQ_OFF) // W_BG
    return pl.pallas_call(
        functools.partial(_deinterleave_kernel, dil=dil),
        out_shape=jax.ShapeDtypeStruct((b, dil, s // dil, 3 * W_BG), BF16),
        grid=(b, s // tm, 3),
        in_specs=[pl.BlockSpec((1, tm, W_BG), lambda bi, i, c: (bi, i, col0 + step * c))],
        out_specs=pl.BlockSpec((1, dil, tm // dil, W_BG), lambda bi, i, c: (bi, 0, i, c)),
        scratch_shapes=[pltpu.VMEM((W_BG // 128, tm, 128), F32)],
        compiler_params=_params(("parallel", "parallel", "parallel")),
        name=f"deinterleave_{dil}",
    )(proj3)


def _dilated_group(proj3, bias, j, dil):
    b, s, w = proj3.shape
    length = s // dil
    half = B_PAIRS[j][0] // (2 * dil)
    n_sub = min(4, length // BAND_TQ)
    tqs = n_sub * BAND_TQ
    cq, ck, cv = (BQ_OFF + j * W_BG, BK_OFF + j * W_BG, BV_OFF + j * W_BG)
    if dil == 1:
        src = proj3.reshape(b, 1, s, w)
        bq, bk, bv = cq // W_BG, ck // W_BG, cv // W_BG
    else:
        src = _deinterleave_qkv(proj3, j, dil)
        bq, bk, bv = 0, 1, 2
    groups = tuple((h, (h,)) for h in range(B_HEADS_PER_PAIR))
    kern = functools.partial(_banded_kernel, head_groups=groups, n_sub=n_sub, half=half, length=length,
                             with_sink=False, with_lse=True)
    o, lse = pl.pallas_call(
        kern,
        out_shape=(jax.ShapeDtypeStruct((b, dil, length, W_BG), BF16),
                   jax.ShapeDtypeStruct((b, dil, length, 128), F32)),
        grid=(b, dil, length // tqs),
        in_specs=[pl.BlockSpec((1, 1, tqs, W_BG), lambda bi, r, i: (bi, r, i, bq)),
                  pl.BlockSpec((1, 1, length, W_BG), lambda bi, r, i: (bi, r, 0, bk)),
                  pl.BlockSpec((1, 1, length, W_BG), lambda bi, r, i: (bi, r, 0, bv)),
                  pl.BlockSpec(bias.shape, lambda bi, r, i: (0, 0, 0, 0))],
        out_specs=(pl.BlockSpec((1, 1, tqs, W_BG), lambda bi, r, i: (bi, r, i, 0)),
                   pl.BlockSpec((1, 1, tqs, 128), lambda bi, r, i: (bi, r, i, 0))),
        compiler_params=_params(("parallel", "parallel", "parallel")),
        name=f"dilated_attn_{dil}",
    )(src, src, src, bias)
    return o, lse


def _window_attn(proj3, bias, sink_rows, n_sub=4):
    b, s, w = proj3.shape
    g = D_HEADS // D_KV_HEADS
    gw = g * HEAD_DIM
    tqs = n_sub * BAND_TQ
    kern = functools.partial(_banded_kernel, head_groups=((0, tuple(range(g))),), n_sub=n_sub, half=D_WINDOW,
                             length=s, with_sink=True, with_lse=False)
    src = proj3.reshape(b, 1, s, w)
    o = pl.pallas_call(
        kern,
        out_shape=jax.ShapeDtypeStruct((b, 1, s, W_D), BF16),
        grid=(b, D_KV_HEADS, s // tqs),
        in_specs=[pl.BlockSpec((1, 1, tqs, gw), lambda bi, hi, i: (bi, 0, i, DQ_OFF // gw + hi)),
                  pl.BlockSpec((1, 1, s, HEAD_DIM), lambda bi, hi, i: (bi, 0, 0, DK_OFF // HEAD_DIM + hi)),
                  pl.BlockSpec((1, 1, s, HEAD_DIM), lambda bi, hi, i: (bi, 0, 0, DV_OFF // HEAD_DIM + hi)),
                  pl.BlockSpec((3, g) + bias.shape[2:], lambda bi, hi, i: (0, hi, 0, 0)),
                  pl.BlockSpec((1, 1, g * BAND_TQ), lambda bi, hi, i: (hi, 0, 0))],
        out_specs=pl.BlockSpec((1, 1, tqs, gw), lambda bi, hi, i: (bi, 0, i, hi)),
        compiler_params=_params(("parallel", "parallel", "parallel")),
        name="window_attn",
    )(src, src, src, bias, sink_rows)
    return o.reshape(b, s, W_D)


def _mix_kernel(g0_ref, g1_ref, g2_ref, oa_ref, ob0_ref, ob1_ref, ob2_ref, l0_ref, l1_ref, l2_ref,
                oc_ref, od_ref, out_ref, ob_scr, lse_scr):
    gate_refs = (g0_ref, g1_ref, g2_ref)

    def interleave(src_ref, scr_ref):
        dil, n, w = src_ref.shape[1:]
        tiles = []
        for c in range(w // 128):
            lanes = slice(c * 128, (c + 1) * 128)
            if dil == 1:
                tiles.append(src_ref[0, 0, :, lanes].astype(F32))
                continue
            for r in range(dil):
                scr_ref[c, pl.ds(r, n, stride=dil), :] = src_ref[0, r, :, lanes].astype(F32)
            tiles.append(scr_ref[c])
        return tiles

    def silu(col0, width):
        blk, off = divmod(col0, GATE_BLK)
        assert off + width <= GATE_BLK
        g = gate_refs[blk][:, off:off + width].astype(F32)
        return g * (1.0 / (1.0 + jnp.exp(-g)))

    def plain(o_ref, col0, width, piece=512):
        for c in range(0, width, piece):
            out_ref[:, col0 + c:col0 + c + piece] = (
                o_ref[:, c:c + piece].astype(F32) * silu(col0 + c, piece)).astype(BF16)

    plain(oa_ref, 0, W_A)
    l0, l1, l2 = (interleave(l_ref, lse_scr)[0] for l_ref in (l0_ref, l1_ref, l2_ref))
    mx = jnp.maximum(jnp.maximum(l0, l1), l2)
    e0, e1, e2 = jnp.exp(l0 - mx), jnp.exp(l1 - mx), jnp.exp(l2 - mx)
    inv = 1.0 / (e0 + e1 + e2)
    for j, (ob_ref, e) in enumerate(((ob0_ref, e0), (ob1_ref, e1), (ob2_ref, e2))):
        alpha = e * inv
        ob = interleave(ob_ref, ob_scr)
        for h in range(B_HEADS_PER_PAIR):
            c0 = W_A + j * W_BG + h * HEAD_DIM
            a_h = alpha[:, h * LSE_LANES:h * LSE_LANES + 1]
            out_ref[:, c0:c0 + HEAD_DIM] = (ob[h] * a_h * silu(c0, HEAD_DIM)).astype(BF16)
    plain(oc_ref, W_A + W_B, W_C)
    plain(od_ref, W_A + W_B + W_C, W_D)


def _mix(proj, o_a, o_b, lse_b, o_c, o_d, seq, tm=512):
    m = proj.shape[0]
    nsb = seq // tm
    row = lambda w: pl.BlockSpec((tm, w), lambda i: (i, 0))
    gate = lambda c: pl.BlockSpec((tm, GATE_BLK), lambda i: (i, GATE_OFF // GATE_BLK + c))

    def classes(a):
        dil, w = a.shape[1], a.shape[3]
        return pl.BlockSpec((1, dil, tm // dil, w), lambda i: (i // nsb, 0, i % nsb, 0))

    return pl.pallas_call(
        _mix_kernel,
        out_shape=jax.ShapeDtypeStruct((m, MIX_WIDTH), BF16),
        grid=(m // tm,),
        in_specs=[gate(0), gate(1), gate(2), row(W_A), *[classes(a) for a in o_b], *[classes(a) for a in lse_b],
                  row(W_C), row(W_D)],
        out_specs=row(MIX_WIDTH),
        scratch_shapes=[pltpu.VMEM((W_BG // 128, tm, 128), F32), pltpu.VMEM((1, tm, 128), F32)],
        compiler_params=_params(("parallel",)),
        name="gate_mix",
    )(proj, proj, proj, o_a, *o_b, *lse_b, o_c, o_d)


def _w_in_rows(w_t, layer):
    qk = HEAD_DIM ** -0.5 * LOG2E
    lat = _w_rows(w_t, layer, 0, LAT_W, n_valid=LAT_USED, name="w_lat_layout")
    main = _w_rows(w_t, layer, LAT_USED, MAIN_W, name="w_main_layout",
                   scaled=((BQ_OFF, BQ_OFF + W_B, qk), (DQ_OFF, DQ_OFF + W_D, qk)))
    return lat, main


def _layout_w_uq(w):
    w = w.reshape(A_Q_LORA, A_HEADS, A_NOPE + A_ROPE)
    w = jnp.pad(w, ((0, 0), (0, 0), (0, A_QK - A_NOPE - A_ROPE)))
    return w.reshape(A_Q_LORA, A_HEADS * A_QK).astype(BF16)


def _layer(x2, b, s, layer, g_attn, w_in_t, g_qa, g_kva, w_uq, w_ukv, g_qn, g_kn, sinks, w_out_b, rel_bias, tables):
    cos_a, sin_a, cos_c, sin_c = tables
    h = _rmsnorm(x2, g_attn, BF16)
    w_lat, w_main = _w_in_rows(w_in_t, layer)
    proj = _in_proj_nt(h, w_main, 1024, 768, "in_proj")
    lat = _in_proj_nt(h, w_lat, 512, LAT_W, "in_proj_latent")
    proj3 = proj.reshape(b, s, MAIN_W)

    q_a, k_a, v_a = _prep_a(lat, g_qa, g_kva, _layout_w_uq(w_uq), w_ukv.astype(BF16), cos_a, sin_a, s)
    o_a = _dense_attn(q_a.reshape(b, s, -1), k_a.reshape(b, s, -1), v_a.reshape(b, s, -1),
                      kv_heads=A_HEADS, groups=1, dq=A_QK, dv=A_V, k_col0=0, v_col0=0, tq=2048, name="attn_a")

    o_b, lse_b = [], []
    for j, (win, dil) in enumerate(B_PAIRS):
        tab = rel_bias[:, j * B_HEADS_PER_PAIR:(j + 1) * B_HEADS_PER_PAIR]
        bias = _band_bias(tab, dil, win // (2 * dil), BAND_TQ)
        o, lse = _dilated_group(proj3, bias, j, dil)
        o_b.append(o)
        lse_b.append(lse)

    q_c, k_c = _prep_c(proj, g_qn, g_kn, cos_c, sin_c, s)
    o_c = _dense_attn(q_c.reshape(b, s, -1), k_c.reshape(b, s, -1), proj3,
                      kv_heads=C_KV_HEADS, groups=C_HEADS // C_KV_HEADS, dq=HEAD_DIM, dv=HEAD_DIM,
                      k_col0=0, v_col0=CV_OFF // HEAD_DIM, tq=512, name="attn_c")

    g = D_HEADS // D_KV_HEADS
    bias_d = _band_bias(rel_bias[:, B_HEADS:], 1, D_WINDOW, BAND_TQ)
    sink_rows = jnp.repeat(sinks.astype(F32) * LOG2E, BAND_TQ).reshape(D_KV_HEADS, 1, g * BAND_TQ)
    o_d = _window_attn(proj3, bias_d, sink_rows)

    mixed = _mix(proj, o_a.reshape(b * s, W_A), o_b, lse_b, o_c.reshape(b * s, W_C), o_d.reshape(b * s, W_D), s)
    return _out_proj(mixed, w_out_b, layer, x2)


def kernel(x, g_attn, w_in, g_qa, g_kva, w_uq, w_ukv, g_qn, g_kn, sinks, w_out, rel_bias, g_final):
    b, s, d = x.shape
    depth = w_in.shape[0]
    tables = _rope_tables(s)
    x2 = x.reshape(b * s, d)
    w_in_t = jnp.swapaxes(w_in, 1, 2)
    w_out_b = w_out.astype(BF16)
    for l in range(depth):
        x2 = _layer(x2, b, s, l, g_attn[l], w_in_t, g_qa[l], g_kva[l], w_uq[l], w_ukv[l], g_qn[l], g_kn[l],
                    sinks[l], w_out_b, rel_bias, tables)
    return _rmsnorm(x2, g_final, F32).reshape(b, s, d)
```

```python
import functools
import math

import numpy as np
import jax
import jax.numpy as jnp
from jax import lax
from jax.experimental import pallas as pl
from jax.experimental.pallas import tpu as pltpu

D_MODEL = 4096
HEAD_DIM = 128
A_HEADS = 8
A_Q_LORA = 768
A_KV_LORA = 512
A_NOPE = 128
A_ROPE = 64
A_V = 128
B_PAIRS = ((128, 1), (512, 4), (2048, 16))
B_HEADS_PER_PAIR = 4
B_HEADS = B_HEADS_PER_PAIR * len(B_PAIRS)
C_HEADS = 8
C_KV_HEADS = 2
D_HEADS = 8
D_KV_HEADS = 2
D_WINDOW = 128
GRID_W = 64
ROPE_THETA = 10000.0
N_BUCKETS = 32
REL_MAX_DIST = 1024
EPS = 1e-6
NEG = -1e30
LOG2E = math.log2(math.e)
LN2 = math.log(2.0)

W_A = A_HEADS * A_V
W_B = B_HEADS * HEAD_DIM
W_C = C_HEADS * HEAD_DIM
W_D = D_HEADS * HEAD_DIM
W_BG = B_HEADS_PER_PAIR * HEAD_DIM
MIX_WIDTH = W_A + W_B + W_C + W_D

LAT_USED = A_Q_LORA + A_KV_LORA + A_ROPE
LAT_W = 1536
BQ_OFF = 0
BK_OFF = BQ_OFF + W_B
BV_OFF = BK_OFF + W_B
CQ_OFF = BV_OFF + W_B
CK_OFF = CQ_OFF + W_C
CV_OFF = CK_OFF + C_KV_HEADS * HEAD_DIM
DQ_OFF = CV_OFF + C_KV_HEADS * HEAD_DIM
DK_OFF = DQ_OFF + W_D
DV_OFF = DK_OFF + D_KV_HEADS * HEAD_DIM
GATE_OFF = DV_OFF + D_KV_HEADS * HEAD_DIM
MAIN_W = GATE_OFF + MIX_WIDTH
GATE_BLK = 1536

A_QK = 256
LSE_LANES = 32
BAND_TQ = 128

VMEM_LIMIT = 56 * 1024 * 1024

F32 = jnp.float32
BF16 = jnp.bfloat16

_NT = (((1,), (1,)), ((), ()))
_TN = (((0,), (0,)), ((), ()))


def _params(sem, vmem=VMEM_LIMIT):
    return pltpu.CompilerParams(dimension_semantics=sem, vmem_limit_bytes=vmem)


def _rmsnorm_kernel(x_ref, g_ref, o_ref):
    x = x_ref[...]
    ms = jnp.mean(x * x, axis=-1, keepdims=True)
    o_ref[...] = (x * lax.rsqrt(ms + EPS) * g_ref[...]).astype(o_ref.dtype)


def _rmsnorm(x, g, out_dtype, tm=256):
    m, d = x.shape
    return pl.pallas_call(
        _rmsnorm_kernel,
        out_shape=jax.ShapeDtypeStruct((m, d), out_dtype),
        grid=(m // tm,),
        in_specs=[pl.BlockSpec((tm, d), lambda i: (i, 0)),
                  pl.BlockSpec((1, d), lambda i: (0, 0))],
        out_specs=pl.BlockSpec((tm, d), lambda i: (i, 0)),
        compiler_params=_params(("parallel",)),
        name="rmsnorm",
    )(x, g.reshape(1, d))


W_ROWS = 192


def _w_rows_kernel(w_ref, o_ref, *, n_valid, scaled):
    row = pl.program_id(0) * W_ROWS + lax.broadcasted_iota(jnp.int32, o_ref.shape, 0)
    w = w_ref[0]
    for start, stop, factor in scaled:
        w = jnp.where((row >= start) & (row < stop), w * factor, w)
    if n_valid is not None:
        w = jnp.where(row < n_valid, w, 0.0)
    o_ref[...] = w.astype(BF16)


def _w_rows(w_t, layer, row0, n_rows, n_valid=None, scaled=(), name="w_layout"):
    assert row0 % W_ROWS == 0 and n_rows % W_ROWS == 0 and row0 + n_rows <= w_t.shape[1]
    k = w_t.shape[2]
    kern = functools.partial(_w_rows_kernel, n_valid=n_valid, scaled=scaled)
    return pl.pallas_call(
        kern,
        out_shape=jax.ShapeDtypeStruct((n_rows, k), BF16),
        grid=(n_rows // W_ROWS,),
        in_specs=[pl.BlockSpec((1, W_ROWS, k), lambda j: (layer, row0 // W_ROWS + j, 0))],
        out_specs=pl.BlockSpec((W_ROWS, k), lambda j: (j, 0)),
        compiler_params=_params(("parallel",)),
        name=name,
    )(w_t)


def _matmul_nt_kernel(x_ref, w_ref, o_ref):
    o_ref[...] = lax.dot_general(x_ref[...], w_ref[...], _NT, preferred_element_type=F32).astype(o_ref.dtype)


def _in_proj_nt(h, w_rows, tm, tn, name):
    m, k = h.shape
    n = w_rows.shape[0]
    return pl.pallas_call(
        _matmul_nt_kernel,
        out_shape=jax.ShapeDtypeStruct((m, n), BF16),
        grid=(m // tm, n // tn),
        in_specs=[pl.BlockSpec((tm, k), lambda i, j: (i, 0)),
                  pl.BlockSpec((tn, k), lambda i, j: (j, 0))],
        out_specs=pl.BlockSpec((tm, tn), lambda i, j: (i, j)),
        compiler_params=_params(("parallel", "parallel")),
        name=name,
    )(h, w_rows)


def _in_proj_main_kernel(x_ref, w_ref, o_ref, *, scaled_tiles, factor):
    j = pl.program_id(1)
    hit = functools.reduce(jnp.logical_or, [j == t for t in scaled_tiles])
    w = (w_ref[...] * jnp.where(hit, factor, 1.0)).astype(BF16)
    o_ref[...] = lax.dot_general(x_ref[...], w, _NT, preferred_element_type=F32).astype(o_ref.dtype)


def _in_proj_main(h, w_t, layer, tm=1024, tn=512):
    m, k = h.shape
    assert BQ_OFF % tn == 0 and W_B % tn == 0 and DQ_OFF % tn == 0 and W_D % tn == 0 and LAT_USED % 8 == 0
    scaled = tuple(range(BQ_OFF // tn, (BQ_OFF + W_B) // tn)) + tuple(range(DQ_OFF // tn, (DQ_OFF + W_D) // tn))
    kern = functools.partial(_in_proj_main_kernel, scaled_tiles=scaled, factor=HEAD_DIM ** -0.5 * LOG2E)
    return pl.pallas_call(
        kern,
        out_shape=jax.ShapeDtypeStruct((m, MAIN_W), BF16),
        grid=(m // tm, MAIN_W // tn),
        in_specs=[pl.BlockSpec((tm, k), lambda i, j: (i, 0)),
                  pl.BlockSpec((pl.Squeezed(), pl.Element(tn), pl.Element(k)),
                               lambda i, j: (layer, pl.multiple_of(LAT_USED + j * tn, 8), 0))],
        out_specs=pl.BlockSpec((tm, tn), lambda i, j: (i, j)),
        compiler_params=_params(("parallel", "parallel")),
        name="in_proj",
    )(h, w_t)


def _out_proj_kernel(m_ref, w_ref, x_ref, o_ref):
    o_ref[...] = x_ref[...] + jnp.dot(m_ref[...], w_ref[0], preferred_element_type=F32)


def _out_proj(mixed, w, layer, x, tm=512, tn=1024):
    m, k = mixed.shape
    n = w.shape[2]
    return pl.pallas_call(
        _out_proj_kernel,
        out_shape=jax.ShapeDtypeStruct((m, n), F32),
        grid=(n // tn, m // tm),
        in_specs=[pl.BlockSpec((tm, k), lambda j, i: (i, 0)),
                  pl.BlockSpec((1, k, tn), lambda j, i: (layer, 0, j)),
                  pl.BlockSpec((tm, tn), lambda j, i: (i, j))],
        out_specs=pl.BlockSpec((tm, tn), lambda j, i: (i, j)),
        compiler_params=_params(("parallel", "parallel")),
        name="out_proj",
    )(mixed, w, x)


def _rope_tile(x, cos, sin_signed):
    lane = lax.broadcasted_iota(jnp.int32, x.shape, 1)
    partner = jnp.where((lane % 64) < 32, pltpu.roll(x, 96, 1), pltpu.roll(x, 32, 1))
    return x * cos + partner * sin_signed


def _rope_tables(seq):
    inv = ROPE_THETA ** (-jnp.arange(0, 64, 2, dtype=F32) / 64)
    pos = jnp.arange(seq)

    def ang(p):
        return p.astype(F32)[:, None] * inv[None, :]

    def halves(a):
        c, s = jnp.cos(a), jnp.sin(a)
        return jnp.concatenate([c, c], -1), jnp.concatenate([-s, s], -1)

    ct, st = halves(ang(pos))
    zeros = jnp.zeros_like(ct)
    cos_a = jnp.concatenate([ct, zeros], -1)
    sin_a = jnp.concatenate([st, zeros], -1)
    cr, sr = halves(ang(pos // GRID_W))
    cc, sc = halves(ang(pos % GRID_W))
    cos_c = jnp.concatenate([cr, cc], -1)
    sin_c = jnp.concatenate([sr, sc], -1)
    return cos_a, sin_a, cos_c, sin_c


def _prep_a_kernel(lat_ref, gq_ref, gkv_ref, wuq_ref, wukv_ref, cos_ref, sin_ref, q_ref, k_ref, v_ref):
    lat = lat_ref[...].astype(F32)
    cq = lat[:, 0:A_Q_LORA]
    ckv = lat[:, A_Q_LORA:A_Q_LORA + A_KV_LORA]
    kpe = lat[:, A_Q_LORA + A_KV_LORA:A_Q_LORA + A_KV_LORA + 128]

    def rms(x, g):
        return x * lax.rsqrt(jnp.mean(x * x, axis=-1, keepdims=True) + EPS) * g

    qa = jnp.dot(rms(cq, gq_ref[...]).astype(BF16), wuq_ref[...], preferred_element_type=F32)
    kva = jnp.dot(rms(ckv, gkv_ref[...]).astype(BF16), wukv_ref[...], preferred_element_type=F32)
    cos = cos_ref[...]
    sin = sin_ref[...]
    scale = (A_NOPE + A_ROPE) ** -0.5 * LOG2E
    kpe_r = _rope_tile(kpe, cos, sin).astype(BF16)
    for h in range(A_HEADS):
        c0 = h * A_QK
        q_ref[:, c0:c0 + 128] = (qa[:, c0:c0 + 128] * scale).astype(BF16)
        q_ref[:, c0 + 128:c0 + 256] = (_rope_tile(qa[:, c0 + 128:c0 + 256], cos, sin) * scale).astype(BF16)
        k_ref[:, c0:c0 + 128] = kva[:, c0:c0 + 128].astype(BF16)
        k_ref[:, c0 + 128:c0 + 256] = kpe_r
        v_ref[:, h * A_V:(h + 1) * A_V] = kva[:, c0 + 128:c0 + 256].astype(BF16)


def _prep_a(lat, g_qa, g_kva, w_uq_p, w_ukv, cos_a, sin_a, seq, tm=512):
    m = lat.shape[0]
    nsb = seq // tm
    return pl.pallas_call(
        _prep_a_kernel,
        out_shape=(jax.ShapeDtypeStruct((m, A_HEADS * A_QK), BF16),
                   jax.ShapeDtypeStruct((m, A_HEADS * A_QK), BF16),
                   jax.ShapeDtypeStruct((m, W_A), BF16)),
        grid=(m // tm,),
        in_specs=[pl.BlockSpec((tm, LAT_W), lambda i: (i, 0)),
                  pl.BlockSpec((1, A_Q_LORA), lambda i: (0, 0)),
                  pl.BlockSpec((1, A_KV_LORA), lambda i: (0, 0)),
                  pl.BlockSpec((A_Q_LORA, A_HEADS * A_QK), lambda i: (0, 0)),
                  pl.BlockSpec((A_KV_LORA, A_HEADS * A_QK), lambda i: (0, 0)),
                  pl.BlockSpec((tm, 128), lambda i: (i % nsb, 0)),
                  pl.BlockSpec((tm, 128), lambda i: (i % nsb, 0))],
        out_specs=(pl.BlockSpec((tm, A_HEADS * A_QK), lambda i: (i, 0)),
                   pl.BlockSpec((tm, A_HEADS * A_QK), lambda i: (i, 0)),
                   pl.BlockSpec((tm, W_A), lambda i: (i, 0))),
        compiler_params=_params(("parallel",)),
        name="prep_a",
    )(lat, g_qa.reshape(1, -1), g_kva.reshape(1, -1), w_uq_p, w_ukv, cos_a, sin_a)


def _prep_c_kernel(q_in_ref, k_in_ref, gq_ref, gk_ref, cos_ref, sin_ref, q_ref, k_ref):
    cos = cos_ref[...]
    sin = sin_ref[...]

    def norm_rope(x, g):
        x = x.astype(F32)
        xn = x * lax.rsqrt(jnp.mean(x * x, axis=-1, keepdims=True) + EPS) * g
        return _rope_tile(xn, cos, sin)

    scale = HEAD_DIM ** -0.5 * LOG2E
    for h in range(q_ref.shape[1] // HEAD_DIM):
        sl = slice(h * HEAD_DIM, (h + 1) * HEAD_DIM)
        q_ref[:, sl] = (norm_rope(q_in_ref[:, sl], gq_ref[...]) * scale).astype(BF16)

    @pl.when(pl.program_id(1) == 0)
    def _():
        for h in range(C_KV_HEADS):
            sl = slice(h * HEAD_DIM, (h + 1) * HEAD_DIM)
            k_ref[:, sl] = norm_rope(k_in_ref[:, sl], gk_ref[...]).astype(BF16)


def _prep_c(proj, g_qn, g_kn, cos_c, sin_c, seq, tm=512, qw=512):
    m = proj.shape[0]
    nsb = seq // tm
    kw = C_KV_HEADS * HEAD_DIM
    return pl.pallas_call(
        _prep_c_kernel,
        out_shape=(jax.ShapeDtypeStruct((m, W_C), BF16),
                   jax.ShapeDtypeStruct((m, kw), BF16)),
        grid=(m // tm, W_C // qw),
        in_specs=[pl.BlockSpec((tm, qw), lambda i, c: (i, CQ_OFF // qw + c)),
                  pl.BlockSpec((tm, kw), lambda i, c: (i, CK_OFF // kw)),
                  pl.BlockSpec((1, HEAD_DIM), lambda i, c: (0, 0)),
                  pl.BlockSpec((1, HEAD_DIM), lambda i, c: (0, 0)),
                  pl.BlockSpec((tm, 128), lambda i, c: (i % nsb, 0)),
                  pl.BlockSpec((tm, 128), lambda i, c: (i % nsb, 0))],
        out_specs=(pl.BlockSpec((tm, qw), lambda i, c: (i, c)),
                   pl.BlockSpec((tm, kw), lambda i, c: (i, 0))),
        compiler_params=_params(("parallel", "arbitrary")),
        name="prep_c",
    )(proj, proj, g_qn.reshape(1, -1), g_kn.reshape(1, -1), cos_c, sin_c)


def _dense_attn_kernel(q_ref, k_ref, v_ref, o_ref, *, groups, dq, dv, chunk):
    tq = q_ref.shape[1]
    s_len = k_ref.shape[1]
    if groups == 1:
        q = q_ref[0]
    else:
        q = jnp.concatenate([q_ref[0, :, g * dq:(g + 1) * dq] for g in range(groups)], axis=0)
    m = l = acc = None
    for j in range(s_len // chunk):
        kj = k_ref[0, j * chunk:(j + 1) * chunk, :]
        vj = v_ref[0, j * chunk:(j + 1) * chunk, :]
        st = lax.dot_general(kj, q, _NT, preferred_element_type=F32)
        cmax = jnp.max(st, axis=0, keepdims=True)
        m_new = cmax if m is None else jnp.maximum(m, cmax)
        p = jnp.exp2(st - m_new)
        psum = jnp.sum(p, axis=0, keepdims=True)
        pv = lax.dot_general(vj, p.astype(BF16), _TN, preferred_element_type=F32)
        if m is None:
            l, acc = psum, pv
        else:
            alpha = jnp.exp2(m - m_new)
            l = alpha * l + psum
            acc = alpha * acc + pv
        m = m_new
    o_t = acc / l
    for g in range(groups):
        o_ref[0, :, g * dv:(g + 1) * dv] = o_t[:, g * tq:(g + 1) * tq].T.astype(o_ref.dtype)


def _dense_attn(q, k, v, *, kv_heads, groups, dq, dv, k_col0, v_col0, tq, name, chunk=512):
    b, s, _ = q.shape
    kern = functools.partial(_dense_attn_kernel, groups=groups, dq=dq, dv=dv, chunk=chunk)
    return pl.pallas_call(
        kern,
        out_shape=jax.ShapeDtypeStruct((b, s, kv_heads * groups * dv), BF16),
        grid=(b, kv_heads, s // tq),
        in_specs=[pl.BlockSpec((1, tq, groups * dq), lambda bi, hi, qi: (bi, qi, hi)),
                  pl.BlockSpec((1, s, dq), lambda bi, hi, qi: (bi, 0, k_col0 + hi)),
                  pl.BlockSpec((1, s, dv), lambda bi, hi, qi: (bi, 0, v_col0 + hi))],
        out_specs=pl.BlockSpec((1, tq, groups * dv), lambda bi, hi, qi: (bi, qi, hi)),
        compiler_params=_params(("parallel", "parallel", "parallel")),
        name=name,
    )(q, k, v)


def _rel_bucket_np(rel):
    nb = N_BUCKETS // 2
    max_exact = nb // 2
    ret = np.where(rel > 0, nb, 0)
    n = np.abs(rel)
    nf = np.maximum(n, 1).astype(np.float32)
    large = max_exact + (np.log(nf / max_exact) / math.log(REL_MAX_DIST / max_exact)
                         * (nb - max_exact)).astype(np.int32)
    large = np.minimum(large, nb - 1)
    return ret + np.where(n < max_exact, n, large)


def _band_bias(tab, dil, half, tq):
    tw = tq + 2 * half
    n = tq + tw - 1
    heads = tab.shape[1]
    tiles = []
    for delta in (0, -half, -2 * half):
        rel = delta + np.arange(n) - (tq - 1)
        inside = np.abs(rel) <= half
        vals = tab[_rel_bucket_np(dil * rel)].astype(F32) * LOG2E
        diag = jnp.where(inside[:, None], vals, NEG).T
        flat = jnp.tile(diag, (1, tq + 1))[:, :tq * (n + 1)]
        tiles.append(flat.reshape(heads, tq, n + 1)[:, ::-1, :tw])
    return jnp.swapaxes(jnp.stack(tiles, 0), -1, -2)


def _banded_kernel(*refs, head_groups, n_sub, half, length, with_sink, with_lse):
    q_ref, k_ref, v_ref, bias_ref = refs[:4]
    pos = 4
    sink_ref = None
    if with_sink:
        sink_ref = refs[pos]
        pos += 1
    o_ref = refs[pos]
    lse_ref = refs[pos + 1] if with_lse else None

    tq = BAND_TQ
    tw = tq + 2 * half
    nblk = length // tq
    for sb in range(n_sub):
        blk = pl.program_id(2) * n_sub + sb
        ks = pl.multiple_of(jnp.clip(blk * tq - half, 0, length - tw), 64)
        var = jnp.where(blk == 0, 0, jnp.where(blk == nblk - 1, 2, 1))
        rows = slice(sb * tq, (sb + 1) * tq)
        lse_rows = []
        for kvh, heads in head_groups:
            n = len(heads) * tq
            cols = slice(kvh * HEAD_DIM, (kvh + 1) * HEAD_DIM)
            kw = k_ref[0, 0, pl.ds(ks, tw), cols]
            vw = v_ref[0, 0, pl.ds(ks, tw), cols]
            qs = [q_ref[0, 0, rows, h * HEAD_DIM:(h + 1) * HEAD_DIM] for h in heads]
            q = qs[0] if len(heads) == 1 else jnp.concatenate(qs, axis=0)
            bias = [bias_ref[var, h] for h in heads]
            bias = bias[0] if len(heads) == 1 else jnp.concatenate(bias, axis=1)
            st = lax.dot_general(kw, q, _NT, preferred_element_type=F32) + bias
            m = jnp.max(st, axis=0, keepdims=True)
            if with_sink:
                sink = sink_ref[0]
                m = jnp.maximum(m, sink)
            p = jnp.exp2(st - m)
            l = jnp.sum(p, axis=0, keepdims=True)
            if with_sink:
                l = l + jnp.exp2(sink - m)
            o_t = lax.dot_general(vw, p.astype(BF16), _TN, preferred_element_type=F32) / l
            for gi, h in enumerate(heads):
                o_ref[0, 0, rows, h * HEAD_DIM:(h + 1) * HEAD_DIM] = (
                    o_t[:, gi * tq:(gi + 1) * tq].T.astype(o_ref.dtype))
            if with_lse:
                lse = (m + jnp.log2(l)) * LN2
                for gi in range(len(heads)):
                    lse_rows.append(jnp.broadcast_to(lse[:, gi * tq:(gi + 1) * tq], (LSE_LANES, tq)))
        if with_lse:
            lse_ref[0, 0, rows, :] = jnp.concatenate(lse_rows, axis=0).T


def _deinterleave_kernel(x_ref, o_ref, scr_ref, *, dil):
    n = scr_ref.shape[1] // dil
    for c in range(scr_ref.shape[0]):
        lanes = slice(c * 128, (c + 1) * 128)
        scr_ref[c] = x_ref[0, :, lanes].astype(F32)
        for r in range(dil):
            o_ref[0, r, :, lanes] = scr_ref[c, pl.ds(r, n, stride=dil), :].astype(o_ref.dtype)


def _deinterleave_qkv(proj3, j, dil, tm=1024):
    b, s, _ = proj3.shape
    col0 = BQ_OFF // W_BG + j
    step = (BK_OFF - BQ_OFF) // W_BG
    return pl.pallas_call(
        functools.partial(_deinterleave_kernel, dil=dil),
        out_shape=jax.ShapeDtypeStruct((b, dil, s // dil, 3 * W_BG), BF16),
        grid=(b, s // tm, 3),
        in_specs=[pl.BlockSpec((1, tm, W_BG), lambda bi, i, c: (bi, i, col0 + step * c))],
        out_specs=pl.BlockSpec((1, dil, tm // dil, W_BG), lambda bi, i, c: (bi, 0, i, c)),
        scratch_shapes=[pltpu.VMEM((W_BG // 128, tm, 128), F32)],
        compiler_params=_params(("parallel", "parallel", "parallel")),
        name=f"deinterleave_{dil}",
    )(proj3)


def _dilated_group(proj3, bias, j, dil):
    b, s, w = proj3.shape
    length = s // dil
    half = B_PAIRS[j][0] // (2 * dil)
    n_sub = min(4, length // BAND_TQ)
    tqs = n_sub * BAND_TQ
    cq, ck, cv = (BQ_OFF + j * W_BG, BK_OFF + j * W_BG, BV_OFF + j * W_BG)
    if dil == 1:
        src = proj3.reshape(b, 1, s, w)
        bq, bk, bv = cq // W_BG, ck // W_BG, cv // W_BG
    else:
        src = _deinterleave_qkv(proj3, j, dil)
        bq, bk, bv = 0, 1, 2
    groups = tuple((h, (h,)) for h in range(B_HEADS_PER_PAIR))
    kern = functools.partial(_banded_kernel, head_groups=groups, n_sub=n_sub, half=half, length=length,
                             with_sink=False, with_lse=True)
    o, lse = pl.pallas_call(
        kern,
        out_shape=(jax.ShapeDtypeStruct((b, dil, length, W_BG), BF16),
                   jax.ShapeDtypeStruct((b, dil, length, 128), F32)),
        grid=(b, dil, length // tqs),
        in_specs=[pl.BlockSpec((1, 1, tqs, W_BG), lambda bi, r, i: (bi, r, i, bq)),
                  pl.BlockSpec((1, 1, length, W_BG), lambda bi, r, i: (bi, r, 0, bk)),
                  pl.BlockSpec((1, 1, length, W_BG), lambda bi, r, i: (bi, r, 0, bv)),
                  pl.BlockSpec(bias.shape, lambda bi, r, i: (0, 0, 0, 0))],
        out_specs=(pl.BlockSpec((1, 1, tqs, W_BG), lambda bi, r, i: (bi, r, i, 0)),
                   pl.BlockSpec((1, 1, tqs, 128), lambda bi, r, i: (bi, r, i, 0))),
        compiler_params=_params(("parallel", "parallel", "parallel")),
        name=f"dilated_attn_{dil}",
    )(src, src, src, bias)
    return o, lse


def _window_attn(proj3, bias, sink_rows, n_sub=4):
    b, s, w = proj3.shape
    g = D_HEADS // D_KV_HEADS
    gw = g * HEAD_DIM
    tqs = n_sub * BAND_TQ
    kern = functools.partial(_banded_kernel, head_groups=((0, tuple(range(g))),), n_sub=n_sub, half=D_WINDOW,
                             length=s, with_sink=True, with_lse=False)
    src = proj3.reshape(b, 1, s, w)
    o = pl.pallas_call(
        kern,
        out_shape=jax.ShapeDtypeStruct((b, 1, s, W_D), BF16),
        grid=(b, D_KV_HEADS, s // tqs),
        in_specs=[pl.BlockSpec((1, 1, tqs, gw), lambda bi, hi, i: (bi, 0, i, DQ_OFF // gw + hi)),
                  pl.BlockSpec((1, 1, s, HEAD_DIM), lambda bi, hi, i: (bi, 0, 0, DK_OFF // HEAD_DIM + hi)),
                  pl.BlockSpec((1, 1, s, HEAD_DIM), lambda bi, hi, i: (bi, 0, 0, DV_OFF // HEAD_DIM + hi)),
                  pl.BlockSpec((3, g) + bias.shape[2:], lambda bi, hi, i: (0, hi, 0, 0)),
                  pl.BlockSpec((1, 1, g * BAND_TQ), lambda bi, hi, i: (hi, 0, 0))],
        out_specs=pl.BlockSpec((1, 1, tqs, gw), lambda bi, hi, i: (bi, 0, i, hi)),
        compiler_params=_params(("parallel", "parallel", "parallel")),
        name="window_attn",
    )(src, src, src, bias, sink_rows)
    return o.reshape(b, s, W_D)


def _mix_kernel(g0_ref, g1_ref, g2_ref, oa_ref, ob0_ref, ob1_ref, ob2_ref, l0_ref, l1_ref, l2_ref,
                oc_ref, od_ref, out_ref, ob_scr, lse_scr):
    gate_refs = (g0_ref, g1_ref, g2_ref)

    def interleave(src_ref, scr_ref):
        dil, n, w = src_ref.shape[1:]
        tiles = []
        for c in range(w // 128):
            lanes = slice(c * 128, (c + 1) * 128)
            if dil == 1:
                tiles.append(src_ref[0, 0, :, lanes].astype(F32))
                continue
            for r in range(dil):
                scr_ref[c, pl.ds(r, n, stride=dil), :] = src_ref[0, r, :, lanes].astype(F32)
            tiles.append(scr_ref[c])
        return tiles

    def silu(col0, width):
        blk, off = divmod(col0, GATE_BLK)
        assert off + width <= GATE_BLK
        g = gate_refs[blk][:, off:off + width].astype(F32)
        return g * (1.0 / (1.0 + jnp.exp(-g)))

    def plain(o_ref, col0, width, piece=512):
        for c in range(0, width, piece):
            out_ref[:, col0 + c:col0 + c + piece] = (
                o_ref[:, c:c + piece].astype(F32) * silu(col0 + c, piece)).astype(BF16)

    plain(oa_ref, 0, W_A)
    l0, l1, l2 = (interleave(l_ref, lse_scr)[0] for l_ref in (l0_ref, l1_ref, l2_ref))
    mx = jnp.maximum(jnp.maximum(l0, l1), l2)
    e0, e1, e2 = jnp.exp(l0 - mx), jnp.exp(l1 - mx), jnp.exp(l2 - mx)
    inv = 1.0 / (e0 + e1 + e2)
    for j, (ob_ref, e) in enumerate(((ob0_ref, e0), (ob1_ref, e1), (ob2_ref, e2))):
        alpha = e * inv
        ob = interleave(ob_ref, ob_scr)
        for h in range(B_HEADS_PER_PAIR):
            c0 = W_A + j * W_BG + h * HEAD_DIM
            a_h = alpha[:, h * LSE_LANES:h * LSE_LANES + 1]
            out_ref[:, c0:c0 + HEAD_DIM] = (ob[h] * a_h * silu(c0, HEAD_DIM)).astype(BF16)
    plain(oc_ref, W_A + W_B, W_C)
    plain(od_ref, W_A + W_B + W_C, W_D)


def _mix(proj, o_a, o_b, lse_b, o_c, o_d, seq, tm=512):
    m = proj.shape[0]
    nsb = seq // tm
    row = lambda w: pl.BlockSpec((tm, w), lambda i: (i, 0))
    gate = lambda c: pl.BlockSpec((tm, GATE_BLK), lambda i: (i, GATE_OFF // GATE_BLK + c))

    def classes(a):
        dil, w = a.shape[1], a.shape[3]
        return pl.BlockSpec((1, dil, tm // dil, w), lambda i: (i // nsb, 0, i % nsb, 0))

    return pl.pallas_call(
        _mix_kernel,
        out_shape=jax.ShapeDtypeStruct((m, MIX_WIDTH), BF16),
        grid=(m // tm,),
        in_specs=[gate(0), gate(1), gate(2), row(W_A), *[classes(a) for a in o_b], *[classes(a) for a in lse_b],
                  row(W_C), row(W_D)],
        out_specs=row(MIX_WIDTH),
        scratch_shapes=[pltpu.VMEM((W_BG // 128, tm, 128), F32), pltpu.VMEM((1, tm, 128), F32)],
        compiler_params=_params(("parallel",)),
        name="gate_mix",
    )(proj, proj, proj, o_a, *o_b, *lse_b, o_c, o_d)


def _w_in_rows(w_t, layer):
    qk = HEAD_DIM ** -0.5 * LOG2E
    lat = _w_rows(w_t, layer, 0, LAT_W, n_valid=LAT_USED, name="w_lat_layout")
    main = _w_rows(w_t, layer, LAT_USED, MAIN_W, name="w_main_layout",
                   scaled=((BQ_OFF, BQ_OFF + W_B, qk), (DQ_OFF, DQ_OFF + W_D, qk)))
    return lat, main


def _layout_w_uq(w):
    w = w.reshape(A_Q_LORA, A_HEADS, A_NOPE + A_ROPE)
    w = jnp.pad(w, ((0, 0), (0, 0), (0, A_QK - A_NOPE - A_ROPE)))
    return w.reshape(A_Q_LORA, A_HEADS * A_QK).astype(BF16)


def _layer(x2, b, s, layer, g_attn, w_in_t, g_qa, g_kva, w_uq, w_ukv, g_qn, g_kn, sinks, w_out_b, rel_bias, tables):
    cos_a, sin_a, cos_c, sin_c = tables
    h = _rmsnorm(x2, g_attn, BF16)
    proj = _in_proj_main(h, w_in_t, layer)
    w_lat = _w_rows(w_in_t, layer, 0, LAT_W, n_valid=LAT_USED, name="w_lat_layout")
    lat = _in_proj_nt(h, w_lat, 512, LAT_W, "in_proj_latent")
    proj3 = proj.reshape(b, s, MAIN_W)

    q_a, k_a, v_a = _prep_a(lat, g_qa, g_kva, _layout_w_uq(w_uq), w_ukv.astype(BF16), cos_a, sin_a, s)
    o_a = _dense_attn(q_a.reshape(b, s, -1), k_a.reshape(b, s, -1), v_a.reshape(b, s, -1),
                      kv_heads=A_HEADS, groups=1, dq=A_QK, dv=A_V, k_col0=0, v_col0=0, tq=2048, name="attn_a")

    o_b, lse_b = [], []
    for j, (win, dil) in enumerate(B_PAIRS):
        tab = rel_bias[:, j * B_HEADS_PER_PAIR:(j + 1) * B_HEADS_PER_PAIR]
        bias = _band_bias(tab, dil, win // (2 * dil), BAND_TQ)
        o, lse = _dilated_group(proj3, bias, j, dil)
        o_b.append(o)
        lse_b.append(lse)

    q_c, k_c = _prep_c(proj, g_qn, g_kn, cos_c, sin_c, s)
    o_c = _dense_attn(q_c.reshape(b, s, -1), k_c.reshape(b, s, -1), proj3,
                      kv_heads=C_KV_HEADS, groups=C_HEADS // C_KV_HEADS, dq=HEAD_DIM, dv=HEAD_DIM,
                      k_col0=0, v_col0=CV_OFF // HEAD_DIM, tq=512, name="attn_c")

    g = D_HEADS // D_KV_HEADS
    bias_d = _band_bias(rel_bias[:, B_HEADS:], 1, D_WINDOW, BAND_TQ)
    sink_rows = jnp.repeat(sinks.astype(F32) * LOG2E, BAND_TQ).reshape(D_KV_HEADS, 1, g * BAND_TQ)
    o_d = _window_attn(proj3, bias_d, sink_rows)

    mixed = _mix(proj, o_a.reshape(b * s, W_A), o_b, lse_b, o_c.reshape(b * s, W_C), o_d.reshape(b * s, W_D), s)
    return _out_proj(mixed, w_out_b, layer, x2)


def kernel(x, g_attn, w_in, g_qa, g_kva, w_uq, w_ukv, g_qn, g_kn, sinks, w_out, rel_bias, g_final):
    b, s, d = x.shape
    depth = w_in.shape[0]
    tables = _rope_tables(s)
    x2 = x.reshape(b * s, d)
    w_in_t = jnp.swapaxes(w_in, 1, 2)
    w_out_b = w_out.astype(BF16)
    for l in range(depth):
        x2 = _layer(x2, b, s, l, g_attn[l], w_in_t, g_qa[l], g_kva[l], w_uq[l], w_ukv[l], g_qn[l], g_kn[l],
                    sinks[l], w_out_b, rel_bias, tables)
    return _rmsnorm(x2, g_final, F32).reshape(b, s, d)
```

```python
import functools
import math

import numpy as np
import jax
import jax.numpy as jnp
from jax import lax
from jax.experimental import pallas as pl
from jax.experimental.pallas import tpu as pltpu

D_MODEL = 4096
HEAD_DIM = 128
A_HEADS = 8
A_Q_LORA = 768
A_KV_LORA = 512
A_NOPE = 128
A_ROPE = 64
A_V = 128
B_PAIRS = ((128, 1), (512, 4), (2048, 16))
B_HEADS_PER_PAIR = 4
B_HEADS = B_HEADS_PER_PAIR * len(B_PAIRS)
C_HEADS = 8
C_KV_HEADS = 2
D_HEADS = 8
D_KV_HEADS = 2
D_WINDOW = 128
GRID_W = 64
ROPE_THETA = 10000.0
N_BUCKETS = 32
REL_MAX_DIST = 1024
EPS = 1e-6
NEG = -1e30
LOG2E = math.log2(math.e)
LN2 = math.log(2.0)

W_A = A_HEADS * A_V
W_B = B_HEADS * HEAD_DIM
W_C = C_HEADS * HEAD_DIM
W_D = D_HEADS * HEAD_DIM
W_BG = B_HEADS_PER_PAIR * HEAD_DIM
MIX_WIDTH = W_A + W_B + W_C + W_D

LAT_USED = A_Q_LORA + A_KV_LORA + A_ROPE
LAT_W = 1536
BQ_OFF = LAT_W
BK_OFF = BQ_OFF + W_B
BV_OFF = BK_OFF + W_B
CQ_OFF = BV_OFF + W_B
CK_OFF = CQ_OFF + W_C
CV_OFF = CK_OFF + C_KV_HEADS * HEAD_DIM
DQ_OFF = CV_OFF + C_KV_HEADS * HEAD_DIM
DK_OFF = DQ_OFF + W_D
DV_OFF = DK_OFF + D_KV_HEADS * HEAD_DIM
GATE_OFF = DV_OFF + D_KV_HEADS * HEAD_DIM
PROJ_W = GATE_OFF + MIX_WIDTH
GATE_BLK = 1536

A_QK = 256
LSE_LANES = 32
BAND_TQ = 128

VMEM_LIMIT = 56 * 1024 * 1024

F32 = jnp.float32
BF16 = jnp.bfloat16

_NT = (((1,), (1,)), ((), ()))
_TN = (((0,), (0,)), ((), ()))


def _params(sem, vmem=VMEM_LIMIT):
    return pltpu.CompilerParams(dimension_semantics=sem, vmem_limit_bytes=vmem)


def _rmsnorm_kernel(x_ref, g_ref, o_ref):
    x = x_ref[...]
    ms = jnp.mean(x * x, axis=-1, keepdims=True)
    o_ref[...] = (x * lax.rsqrt(ms + EPS) * g_ref[...]).astype(o_ref.dtype)


def _rmsnorm(x, g, out_dtype, tm=256):
    m, d = x.shape
    return pl.pallas_call(
        _rmsnorm_kernel,
        out_shape=jax.ShapeDtypeStruct((m, d), out_dtype),
        grid=(m // tm,),
        in_specs=[pl.BlockSpec((tm, d), lambda i: (i, 0)),
                  pl.BlockSpec((1, d), lambda i: (0, 0))],
        out_specs=pl.BlockSpec((tm, d), lambda i: (i, 0)),
        compiler_params=_params(("parallel",)),
        name="rmsnorm",
    )(x, g.reshape(1, d))


def _in_proj_kernel(x_ref, w_ref, o_ref, *, factors):
    tn = w_ref.shape[0]
    col = pl.program_id(1) * tn + lax.broadcasted_iota(jnp.int32, (tn, 1), 0)
    s = jnp.ones((tn, 1), F32)
    for start, stop, factor in factors:
        s = jnp.where((col >= start) & (col < stop), factor, s)
    w = (w_ref[...] * s).astype(BF16)
    o_ref[...] = lax.dot_general(x_ref[...], w, _NT, preferred_element_type=F32).astype(o_ref.dtype)


def _in_proj(h, w_t, layer, tm=1024, tn=768):
    m, k = h.shape
    pad = LAT_W - LAT_USED
    assert LAT_W % tn == 0 and PROJ_W % tn == 0 and pad % 8 == 0
    qk = HEAD_DIM ** -0.5 * LOG2E
    factors = ((LAT_USED, LAT_W, 0.0), (BQ_OFF, BQ_OFF + W_B, qk), (DQ_OFF, DQ_OFF + W_D, qk))
    kern = functools.partial(_in_proj_kernel, factors=factors)

    def w_index(i, j):
        first = j * tn - jnp.where(j < LAT_W // tn, 0, pad)
        return layer, pl.multiple_of(first, 8), 0

    return pl.pallas_call(
        kern,
        out_shape=jax.ShapeDtypeStruct((m, PROJ_W), BF16),
        grid=(m // tm, PROJ_W // tn),
        in_specs=[pl.BlockSpec((tm, k), lambda i, j: (i, 0)),
                  pl.BlockSpec((pl.Squeezed(), pl.Element(tn), pl.Element(k)), w_index)],
        out_specs=pl.BlockSpec((tm, tn), lambda i, j: (i, j)),
        compiler_params=_params(("parallel", "parallel")),
        name="in_proj",
    )(h, w_t)


def _out_proj_kernel(m_ref, w_ref, x_ref, o_ref):
    o_ref[...] = x_ref[...] + jnp.dot(m_ref[...], w_ref[0], preferred_element_type=F32)


def _out_proj(mixed, w, layer, x, tm=512, tn=1024):
    m, k = mixed.shape
    n = w.shape[2]
    return pl.pallas_call(
        _out_proj_kernel,
        out_shape=jax.ShapeDtypeStruct((m, n), F32),
        grid=(n // tn, m // tm),
        in_specs=[pl.BlockSpec((tm, k), lambda j, i: (i, 0)),
                  pl.BlockSpec((1, k, tn), lambda j, i: (layer, 0, j)),
                  pl.BlockSpec((tm, tn), lambda j, i: (i, j))],
        out_specs=pl.BlockSpec((tm, tn), lambda j, i: (i, j)),
        compiler_params=_params(("parallel", "parallel")),
        name="out_proj",
    )(mixed, w, x)


def _rope_tile(x, cos, sin_signed):
    lane = lax.broadcasted_iota(jnp.int32, x.shape, 1)
    partner = jnp.where((lane % 64) < 32, pltpu.roll(x, 96, 1), pltpu.roll(x, 32, 1))
    return x * cos + partner * sin_signed


def _rope_tables(seq):
    inv = ROPE_THETA ** (-jnp.arange(0, 64, 2, dtype=F32) / 64)
    pos = jnp.arange(seq)

    def ang(p):
        return p.astype(F32)[:, None] * inv[None, :]

    def halves(a):
        c, s = jnp.cos(a), jnp.sin(a)
        return jnp.concatenate([c, c], -1), jnp.concatenate([-s, s], -1)

    ct, st = halves(ang(pos))
    zeros = jnp.zeros_like(ct)
    cos_a = jnp.concatenate([ct, zeros], -1)
    sin_a = jnp.concatenate([st, zeros], -1)
    cr, sr = halves(ang(pos // GRID_W))
    cc, sc = halves(ang(pos % GRID_W))
    cos_c = jnp.concatenate([cr, cc], -1)
    sin_c = jnp.concatenate([sr, sc], -1)
    return cos_a, sin_a, cos_c, sin_c


def _prep_a_kernel(lat_ref, gq_ref, gkv_ref, wuq_ref, wukv_ref, cos_ref, sin_ref, q_ref, k_ref, v_ref):
    lat = lat_ref[...].astype(F32)
    cq = lat[:, 0:A_Q_LORA]
    ckv = lat[:, A_Q_LORA:A_Q_LORA + A_KV_LORA]
    kpe = lat[:, A_Q_LORA + A_KV_LORA:A_Q_LORA + A_KV_LORA + 128]

    def rms(x, g):
        return x * lax.rsqrt(jnp.mean(x * x, axis=-1, keepdims=True) + EPS) * g

    qa = jnp.dot(rms(cq, gq_ref[...]).astype(BF16), wuq_ref[...], preferred_element_type=F32)
    kva = jnp.dot(rms(ckv, gkv_ref[...]).astype(BF16), wukv_ref[...], preferred_element_type=F32)
    cos = cos_ref[...]
    sin = sin_ref[...]
    scale = (A_NOPE + A_ROPE) ** -0.5 * LOG2E
    kpe_r = _rope_tile(kpe, cos, sin).astype(BF16)
    for h in range(A_HEADS):
        c0 = h * A_QK
        q_ref[:, c0:c0 + 128] = (qa[:, c0:c0 + 128] * scale).astype(BF16)
        q_ref[:, c0 + 128:c0 + 256] = (_rope_tile(qa[:, c0 + 128:c0 + 256], cos, sin) * scale).astype(BF16)
        k_ref[:, c0:c0 + 128] = kva[:, c0:c0 + 128].astype(BF16)
        k_ref[:, c0 + 128:c0 + 256] = kpe_r
        v_ref[:, h * A_V:(h + 1) * A_V] = kva[:, c0 + 128:c0 + 256].astype(BF16)


def _prep_a(lat, g_qa, g_kva, w_uq_p, w_ukv, cos_a, sin_a, seq, tm=512):
    m = lat.shape[0]
    nsb = seq // tm
    return pl.pallas_call(
        _prep_a_kernel,
        out_shape=(jax.ShapeDtypeStruct((m, A_HEADS * A_QK), BF16),
                   jax.ShapeDtypeStruct((m, A_HEADS * A_QK), BF16),
                   jax.ShapeDtypeStruct((m, W_A), BF16)),
        grid=(m // tm,),
        in_specs=[pl.BlockSpec((tm, LAT_W), lambda i: (i, 0)),
                  pl.BlockSpec((1, A_Q_LORA), lambda i: (0, 0)),
                  pl.BlockSpec((1, A_KV_LORA), lambda i: (0, 0)),
                  pl.BlockSpec((A_Q_LORA, A_HEADS * A_QK), lambda i: (0, 0)),
                  pl.BlockSpec((A_KV_LORA, A_HEADS * A_QK), lambda i: (0, 0)),
                  pl.BlockSpec((tm, 128), lambda i: (i % nsb, 0)),
                  pl.BlockSpec((tm, 128), lambda i: (i % nsb, 0))],
        out_specs=(pl.BlockSpec((tm, A_HEADS * A_QK), lambda i: (i, 0)),
                   pl.BlockSpec((tm, A_HEADS * A_QK), lambda i: (i, 0)),
                   pl.BlockSpec((tm, W_A), lambda i: (i, 0))),
        compiler_params=_params(("parallel",)),
        name="prep_a",
    )(lat, g_qa.reshape(1, -1), g_kva.reshape(1, -1), w_uq_p, w_ukv, cos_a, sin_a)


def _prep_c_kernel(q_in_ref, k_in_ref, gq_ref, gk_ref, cos_ref, sin_ref, q_ref, k_ref):
    cos = cos_ref[...]
    sin = sin_ref[...]

    def norm_rope(x, g):
        x = x.astype(F32)
        xn = x * lax.rsqrt(jnp.mean(x * x, axis=-1, keepdims=True) + EPS) * g
        return _rope_tile(xn, cos, sin)

    scale = HEAD_DIM ** -0.5 * LOG2E
    for h in range(q_ref.shape[1] // HEAD_DIM):
        sl = slice(h * HEAD_DIM, (h + 1) * HEAD_DIM)
        q_ref[:, sl] = (norm_rope(q_in_ref[:, sl], gq_ref[...]) * scale).astype(BF16)

    @pl.when(pl.program_id(1) == 0)
    def _():
        for h in range(C_KV_HEADS):
            sl = slice(h * HEAD_DIM, (h + 1) * HEAD_DIM)
            k_ref[:, sl] = norm_rope(k_in_ref[:, sl], gk_ref[...]).astype(BF16)


def _prep_c(proj, g_qn, g_kn, cos_c, sin_c, seq, tm=512, qw=512):
    m = proj.shape[0]
    nsb = seq // tm
    kw = C_KV_HEADS * HEAD_DIM
    return pl.pallas_call(
        _prep_c_kernel,
        out_shape=(jax.ShapeDtypeStruct((m, W_C), BF16),
                   jax.ShapeDtypeStruct((m, kw), BF16)),
        grid=(m // tm, W_C // qw),
        in_specs=[pl.BlockSpec((tm, qw), lambda i, c: (i, CQ_OFF // qw + c)),
                  pl.BlockSpec((tm, kw), lambda i, c: (i, CK_OFF // kw)),
                  pl.BlockSpec((1, HEAD_DIM), lambda i, c: (0, 0)),
                  pl.BlockSpec((1, HEAD_DIM), lambda i, c: (0, 0)),
                  pl.BlockSpec((tm, 128), lambda i, c: (i % nsb, 0)),
                  pl.BlockSpec((tm, 128), lambda i, c: (i % nsb, 0))],
        out_specs=(pl.BlockSpec((tm, qw), lambda i, c: (i, c)),
                   pl.BlockSpec((tm, kw), lambda i, c: (i, 0))),
        compiler_params=_params(("parallel", "arbitrary")),
        name="prep_c",
    )(proj, proj, g_qn.reshape(1, -1), g_kn.reshape(1, -1), cos_c, sin_c)


def _dense_attn_kernel(q_ref, k_ref, v_ref, o_ref, *, groups, dq, dv, chunk, strip):
    tq = q_ref.shape[1]
    s_len = k_ref.shape[1]
    if groups == 1:
        q = q_ref[0]
    else:
        q = jnp.concatenate([q_ref[0, :, g * dq:(g + 1) * dq] for g in range(groups)], axis=0)
    n_strip = groups * tq // strip
    qs = [q[c * strip:(c + 1) * strip, :] for c in range(n_strip)]
    state = [None] * n_strip
    for j in range(s_len // chunk):
        kj = k_ref[0, j * chunk:(j + 1) * chunk, :]
        vj = v_ref[0, j * chunk:(j + 1) * chunk, :]
        for c in range(n_strip):
            st = lax.dot_general(kj, qs[c], _NT, preferred_element_type=F32)
            cmax = jnp.max(st, axis=0, keepdims=True)
            m_new = cmax if state[c] is None else jnp.maximum(state[c][0], cmax)
            p = jnp.exp2(st - m_new)
            psum = jnp.sum(p, axis=0, keepdims=True)
            pv = lax.dot_general(vj, p.astype(BF16), _TN, preferred_element_type=F32)
            if state[c] is None:
                state[c] = (m_new, psum, pv)
            else:
                m, l, acc = state[c]
                alpha = jnp.exp2(m - m_new)
                state[c] = (m_new, alpha * l + psum, alpha * acc + pv)
    o_t = jnp.concatenate([acc / l for _, l, acc in state], axis=1)
    for g in range(groups):
        o_ref[0, :, g * dv:(g + 1) * dv] = o_t[:, g * tq:(g + 1) * tq].T.astype(o_ref.dtype)


def _dense_attn(q, k, v, *, kv_heads, groups, dq, dv, k_col0, v_col0, tq, name, chunk=512, strip=None):
    b, s, _ = q.shape
    kern = functools.partial(_dense_attn_kernel, groups=groups, dq=dq, dv=dv, chunk=chunk,
                             strip=strip or groups * tq)
    return pl.pallas_call(
        kern,
        out_shape=jax.ShapeDtypeStruct((b, s, kv_heads * groups * dv), BF16),
        grid=(b, kv_heads, s // tq),
        in_specs=[pl.BlockSpec((1, tq, groups * dq), lambda bi, hi, qi: (bi, qi, hi)),
                  pl.BlockSpec((1, s, dq), lambda bi, hi, qi: (bi, 0, k_col0 + hi)),
                  pl.BlockSpec((1, s, dv), lambda bi, hi, qi: (bi, 0, v_col0 + hi))],
        out_specs=pl.BlockSpec((1, tq, groups * dv), lambda bi, hi, qi: (bi, qi, hi)),
        compiler_params=_params(("parallel", "parallel", "parallel")),
        name=name,
    )(q, k, v)


def _rel_bucket_np(rel):
    nb = N_BUCKETS // 2
    max_exact = nb // 2
    ret = np.where(rel > 0, nb, 0)
    n = np.abs(rel)
    nf = np.maximum(n, 1).astype(np.float32)
    large = max_exact + (np.log(nf / max_exact) / math.log(REL_MAX_DIST / max_exact)
                         * (nb - max_exact)).astype(np.int32)
    large = np.minimum(large, nb - 1)
    return ret + np.where(n < max_exact, n, large)


def _band_bias(tab, dil, half, tq):
    tw = tq + 2 * half
    n = tq + tw - 1
    heads = tab.shape[1]
    tiles = []
    for delta in (0, -half, -2 * half):
        rel = delta + np.arange(n) - (tq - 1)
        inside = np.abs(rel) <= half
        vals = tab[_rel_bucket_np(dil * rel)].astype(F32) * LOG2E
        diag = jnp.where(inside[:, None], vals, NEG).T
        flat = jnp.tile(diag, (1, tq + 1))[:, :tq * (n + 1)]
        tiles.append(flat.reshape(heads, tq, n + 1)[:, ::-1, :tw])
    return jnp.swapaxes(jnp.stack(tiles, 0), -1, -2)


def _banded_kernel(*refs, head_groups, n_sub, half, length, with_sink, with_lse):
    q_ref, k_ref, v_ref, bias_ref = refs[:4]
    pos = 4
    sink_ref = None
    if with_sink:
        sink_ref = refs[pos]
        pos += 1
    o_ref = refs[pos]
    lse_ref = refs[pos + 1] if with_lse else None

    tq = BAND_TQ
    tw = tq + 2 * half
    nblk = length // tq
    for sb in range(n_sub):
        blk = pl.program_id(2) * n_sub + sb
        ks = pl.multiple_of(jnp.clip(blk * tq - half, 0, length - tw), 64)
        var = jnp.where(blk == 0, 0, jnp.where(blk == nblk - 1, 2, 1))
        rows = slice(sb * tq, (sb + 1) * tq)
        lse_rows = []
        for kvh, heads in head_groups:
            n = len(heads) * tq
            cols = slice(kvh * HEAD_DIM, (kvh + 1) * HEAD_DIM)
            kw = k_ref[0, 0, pl.ds(ks, tw), cols]
            vw = v_ref[0, 0, pl.ds(ks, tw), cols]
            qs = [q_ref[0, 0, rows, h * HEAD_DIM:(h + 1) * HEAD_DIM] for h in heads]
            q = qs[0] if len(heads) == 1 else jnp.concatenate(qs, axis=0)
            bias = [bias_ref[var, h] for h in heads]
            bias = bias[0] if len(heads) == 1 else jnp.concatenate(bias, axis=1)
            st = lax.dot_general(kw, q, _NT, preferred_element_type=F32) + bias
            m = jnp.max(st, axis=0, keepdims=True)
            if with_sink:
                sink = sink_ref[0]
                m = jnp.maximum(m, sink)
            p = jnp.exp2(st - m)
            l = jnp.sum(p, axis=0, keepdims=True)
            if with_sink:
                l = l + jnp.exp2(sink - m)
            o_t = lax.dot_general(vw, p.astype(BF16), _TN, preferred_element_type=F32) / l
            for gi, h in enumerate(heads):
                o_ref[0, 0, rows, h * HEAD_DIM:(h + 1) * HEAD_DIM] = (
                    o_t[:, gi * tq:(gi + 1) * tq].T.astype(o_ref.dtype))
            if with_lse:
                lse = (m + jnp.log2(l)) * LN2
                for gi in range(len(heads)):
                    lse_rows.append(jnp.broadcast_to(lse[:, gi * tq:(gi + 1) * tq], (LSE_LANES, tq)))
        if with_lse:
            lse_ref[0, 0, rows, :] = jnp.concatenate(lse_rows, axis=0).T


def _deinterleave_kernel(x_ref, o_ref, scr_ref, *, dil):
    n = scr_ref.shape[1] // dil
    for c in range(scr_ref.shape[0]):
        lanes = slice(c * 128, (c + 1) * 128)
        scr_ref[c] = x_ref[0, :, lanes].astype(F32)
        for r in range(dil):
            o_ref[0, r, :, lanes] = scr_ref[c, pl.ds(r, n, stride=dil), :].astype(o_ref.dtype)


def _deinterleave_qkv(proj3, j, dil, tm=1024):
    b, s, _ = proj3.shape
    col0 = BQ_OFF // W_BG + j
    step = (BK_OFF - BQ_OFF) // W_BG
    return pl.pallas_call(
        functools.partial(_deinterleave_kernel, dil=dil),
        out_shape=jax.ShapeDtypeStruct((b, dil, s // dil, 3 * W_BG), BF16),
        grid=(b, s // tm, 3),
        in_specs=[pl.BlockSpec((1, tm, W_BG), lambda bi, i, c: (bi, i, col0 + step * c))],
        out_specs=pl.BlockSpec((1, dil, tm // dil, W_BG), lambda bi, i, c: (bi, 0, i, c)),
        scratch_shapes=[pltpu.VMEM((W_BG // 128, tm, 128), F32)],
        compiler_params=_params(("parallel", "parallel", "parallel")),
        name=f"deinterleave_{dil}",
    )(proj3)


def _dilated_group(proj3, bias, j, dil):
    b, s, w = proj3.shape
    length = s // dil
    half = B_PAIRS[j][0] // (2 * dil)
    n_sub = min(4, length // BAND_TQ)
    tqs = n_sub * BAND_TQ
    cq, ck, cv = (BQ_OFF + j * W_BG, BK_OFF + j * W_BG, BV_OFF + j * W_BG)
    if dil == 1:
        src = proj3.reshape(b, 1, s, w)
        bq, bk, bv = cq // W_BG, ck // W_BG, cv // W_BG
    else:
        src = _deinterleave_qkv(proj3, j, dil)
        bq, bk, bv = 0, 1, 2
    groups = tuple((h, (h,)) for h in range(B_HEADS_PER_PAIR))
    kern = functools.partial(_banded_kernel, head_groups=groups, n_sub=n_sub, half=half, length=length,
                             with_sink=False, with_lse=True)
    o, lse = pl.pallas_call(
        kern,
        out_shape=(jax.ShapeDtypeStruct((b, dil, length, W_BG), BF16),
                   jax.ShapeDtypeStruct((b, dil, length, 128), F32)),
        grid=(b, dil, length // tqs),
        in_specs=[pl.BlockSpec((1, 1, tqs, W_BG), lambda bi, r, i: (bi, r, i, bq)),
                  pl.BlockSpec((1, 1, length, W_BG), lambda bi, r, i: (bi, r, 0, bk)),
                  pl.BlockSpec((1, 1, length, W_BG), lambda bi, r, i: (bi, r, 0, bv)),
                  pl.BlockSpec(bias.shape, lambda bi, r, i: (0, 0, 0, 0))],
        out_specs=(pl.BlockSpec((1, 1, tqs, W_BG), lambda bi, r, i: (bi, r, i, 0)),
                   pl.BlockSpec((1, 1, tqs, 128), lambda bi, r, i: (bi, r, i, 0))),
        compiler_params=_params(("parallel", "parallel", "parallel")),
        name=f"dilated_attn_{dil}",
    )(src, src, src, bias)
    return o, lse


def _window_attn(proj3, bias, sink_rows, n_sub=4):
    b, s, w = proj3.shape
    g = D_HEADS // D_KV_HEADS
    gw = g * HEAD_DIM
    tqs = n_sub * BAND_TQ
    kern = functools.partial(_banded_kernel, head_groups=((0, tuple(range(g))),), n_sub=n_sub, half=D_WINDOW,
                             length=s, with_sink=True, with_lse=False)
    src = proj3.reshape(b, 1, s, w)
    o = pl.pallas_call(
        kern,
        out_shape=jax.ShapeDtypeStruct((b, 1, s, W_D), BF16),
        grid=(b, D_KV_HEADS, s // tqs),
        in_specs=[pl.BlockSpec((1, 1, tqs, gw), lambda bi, hi, i: (bi, 0, i, DQ_OFF // gw + hi)),
                  pl.BlockSpec((1, 1, s, HEAD_DIM), lambda bi, hi, i: (bi, 0, 0, DK_OFF // HEAD_DIM + hi)),
                  pl.BlockSpec((1, 1, s, HEAD_DIM), lambda bi, hi, i: (bi, 0, 0, DV_OFF // HEAD_DIM + hi)),
                  pl.BlockSpec((3, g) + bias.shape[2:], lambda bi, hi, i: (0, hi, 0, 0)),
                  pl.BlockSpec((1, 1, g * BAND_TQ), lambda bi, hi, i: (hi, 0, 0))],
        out_specs=pl.BlockSpec((1, 1, tqs, gw), lambda bi, hi, i: (bi, 0, i, hi)),
        compiler_params=_params(("parallel", "parallel", "parallel")),
        name="window_attn",
    )(src, src, src, bias, sink_rows)
    return o.reshape(b, s, W_D)


def _mix_kernel(g0_ref, g1_ref, g2_ref, oa_ref, ob0_ref, ob1_ref, ob2_ref, l0_ref, l1_ref, l2_ref,
                oc_ref, od_ref, out_ref, ob_scr, lse_scr):
    gate_refs = (g0_ref, g1_ref, g2_ref)

    def interleave(src_ref, scr_ref):
        dil, n, w = src_ref.shape[1:]
        tiles = []
        for c in range(w // 128):
            lanes = slice(c * 128, (c + 1) * 128)
            if dil == 1:
                tiles.append(src_ref[0, 0, :, lanes].astype(F32))
                continue
            for r in range(dil):
                scr_ref[c, pl.ds(r, n, stride=dil), :] = src_ref[0, r, :, lanes].astype(F32)
            tiles.append(scr_ref[c])
        return tiles

    def silu(col0, width):
        blk, off = divmod(col0, GATE_BLK)
        assert off + width <= GATE_BLK
        g = gate_refs[blk][:, off:off + width].astype(F32)
        return g * (1.0 / (1.0 + jnp.exp(-g)))

    def plain(o_ref, col0, width, piece=512):
        for c in range(0, width, piece):
            out_ref[:, col0 + c:col0 + c + piece] = (
                o_ref[:, c:c + piece].astype(F32) * silu(col0 + c, piece)).astype(BF16)

    plain(oa_ref, 0, W_A)
    l0, l1, l2 = (interleave(l_ref, lse_scr)[0] for l_ref in (l0_ref, l1_ref, l2_ref))
    mx = jnp.maximum(jnp.maximum(l0, l1), l2)
    e0, e1, e2 = jnp.exp(l0 - mx), jnp.exp(l1 - mx), jnp.exp(l2 - mx)
    inv = 1.0 / (e0 + e1 + e2)
    for j, (ob_ref, e) in enumerate(((ob0_ref, e0), (ob1_ref, e1), (ob2_ref, e2))):
        alpha = e * inv
        ob = interleave(ob_ref, ob_scr)
        for h in range(B_HEADS_PER_PAIR):
            c0 = W_A + j * W_BG + h * HEAD_DIM
            a_h = alpha[:, h * LSE_LANES:h * LSE_LANES + 1]
            out_ref[:, c0:c0 + HEAD_DIM] = (ob[h] * a_h * silu(c0, HEAD_DIM)).astype(BF16)
    plain(oc_ref, W_A + W_B, W_C)
    plain(od_ref, W_A + W_B + W_C, W_D)


def _mix(proj, o_a, o_b, lse_b, o_c, o_d, seq, tm=512):
    m = proj.shape[0]
    nsb = seq // tm
    row = lambda w: pl.BlockSpec((tm, w), lambda i: (i, 0))
    gate = lambda c: pl.BlockSpec((tm, GATE_BLK), lambda i: (i, GATE_OFF // GATE_BLK + c))

    def classes(a):
        dil, w = a.shape[1], a.shape[3]
        return pl.BlockSpec((1, dil, tm // dil, w), lambda i: (i // nsb, 0, i % nsb, 0))

    return pl.pallas_call(
        _mix_kernel,
        out_shape=jax.ShapeDtypeStruct((m, MIX_WIDTH), BF16),
        grid=(m // tm,),
        in_specs=[gate(0), gate(1), gate(2), row(W_A), *[classes(a) for a in o_b], *[classes(a) for a in lse_b],
                  row(W_C), row(W_D)],
        out_specs=row(MIX_WIDTH),
        scratch_shapes=[pltpu.VMEM((W_BG // 128, tm, 128), F32), pltpu.VMEM((1, tm, 128), F32)],
        compiler_params=_params(("parallel",)),
        name="gate_mix",
    )(proj, proj, proj, o_a, *o_b, *lse_b, o_c, o_d)


def _layout_w_uq(w):
    w = w.reshape(A_Q_LORA, A_HEADS, A_NOPE + A_ROPE)
    w = jnp.pad(w, ((0, 0), (0, 0), (0, A_QK - A_NOPE - A_ROPE)))
    return w.reshape(A_Q_LORA, A_HEADS * A_QK).astype(BF16)


def _layer(x2, b, s, layer, g_attn, w_in_t, g_qa, g_kva, w_uq, w_ukv, g_qn, g_kn, sinks, w_out_b, rel_bias, tables):
    cos_a, sin_a, cos_c, sin_c = tables
    h = _rmsnorm(x2, g_attn, BF16)
    proj = _in_proj(h, w_in_t, layer)
    proj3 = proj.reshape(b, s, PROJ_W)

    q_a, k_a, v_a = _prep_a(proj, g_qa, g_kva, _layout_w_uq(w_uq), w_ukv.astype(BF16), cos_a, sin_a, s)
    o_a = _dense_attn(q_a.reshape(b, s, -1), k_a.reshape(b, s, -1), v_a.reshape(b, s, -1),
                      kv_heads=A_HEADS, groups=1, dq=A_QK, dv=A_V, k_col0=0, v_col0=0, tq=2048, name="attn_a")

    o_b, lse_b = [], []
    for j, (win, dil) in enumerate(B_PAIRS):
        tab = rel_bias[:, j * B_HEADS_PER_PAIR:(j + 1) * B_HEADS_PER_PAIR]
        bias = _band_bias(tab, dil, win // (2 * dil), BAND_TQ)
        o, lse = _dilated_group(proj3, bias, j, dil)
        o_b.append(o)
        lse_b.append(lse)

    q_c, k_c = _prep_c(proj, g_qn, g_kn, cos_c, sin_c, s)
    o_c = _dense_attn(q_c.reshape(b, s, -1), k_c.reshape(b, s, -1), proj3,
                      kv_heads=C_KV_HEADS, groups=C_HEADS // C_KV_HEADS, dq=HEAD_DIM, dv=HEAD_DIM,
                      k_col0=0, v_col0=CV_OFF // HEAD_DIM, tq=512, name="attn_c")

    g = D_HEADS // D_KV_HEADS
    bias_d = _band_bias(rel_bias[:, B_HEADS:], 1, D_WINDOW, BAND_TQ)
    sink_rows = jnp.repeat(sinks.astype(F32) * LOG2E, BAND_TQ).reshape(D_KV_HEADS, 1, g * BAND_TQ)
    o_d = _window_attn(proj3, bias_d, sink_rows)

    mixed = _mix(proj, o_a.reshape(b * s, W_A), o_b, lse_b, o_c.reshape(b * s, W_C), o_d.reshape(b * s, W_D), s)
    return _out_proj(mixed, w_out_b, layer, x2)


def kernel(x, g_attn, w_in, g_qa, g_kva, w_uq, w_ukv, g_qn, g_kn, sinks, w_out, rel_bias, g_final):
    b, s, d = x.shape
    depth = w_in.shape[0]
    tables = _rope_tables(s)
    x2 = x.reshape(b * s, d)
    w_in_t = jnp.swapaxes(w_in, 1, 2)
    w_out_b = w_out.astype(BF16)
    for l in range(depth):
        x2 = _layer(x2, b, s, l, g_attn[l], w_in_t, g_qa[l], g_kva[l], w_uq[l], w_ukv[l], g_qn[l], g_kn[l],
                    sinks[l], w_out_b, rel_bias, tables)
    return _rmsnorm(x2, g_final, F32).reshape(b, s, d)
```

```python
import functools
import math

import numpy as np
import jax
import jax.numpy as jnp
from jax import lax
from jax.experimental import pallas as pl
from jax.experimental.pallas import tpu as pltpu

D_MODEL = 4096
HEAD_DIM = 128
A_HEADS = 8
A_Q_LORA = 768
A_KV_LORA = 512
A_NOPE = 128
A_ROPE = 64
A_V = 128
B_PAIRS = ((128, 1), (512, 4), (2048, 16))
B_HEADS_PER_PAIR = 4
B_HEADS = B_HEADS_PER_PAIR * len(B_PAIRS)
C_HEADS = 8
C_KV_HEADS = 2
D_HEADS = 8
D_KV_HEADS = 2
D_WINDOW = 128
GRID_W = 64
ROPE_THETA = 10000.0
N_BUCKETS = 32
REL_MAX_DIST = 1024
EPS = 1e-6
NEG = -1e30
LOG2E = math.log2(math.e)
LN2 = math.log(2.0)

W_A = A_HEADS * A_V
W_B = B_HEADS * HEAD_DIM
W_C = C_HEADS * HEAD_DIM
W_D = D_HEADS * HEAD_DIM
W_BG = B_HEADS_PER_PAIR * HEAD_DIM
MIX_WIDTH = W_A + W_B + W_C + W_D

LAT_USED = A_Q_LORA + A_KV_LORA + A_ROPE
LAT_W = 1536
BQ_OFF = LAT_W
BK_OFF = BQ_OFF + W_B
BV_OFF = BK_OFF + W_B
CQ_OFF = BV_OFF + W_B
CK_OFF = CQ_OFF + W_C
CV_OFF = CK_OFF + C_KV_HEADS * HEAD_DIM
DQ_OFF = CV_OFF + C_KV_HEADS * HEAD_DIM
DK_OFF = DQ_OFF + W_D
DV_OFF = DK_OFF + D_KV_HEADS * HEAD_DIM
GATE_OFF = DV_OFF + D_KV_HEADS * HEAD_DIM
PROJ_W = GATE_OFF + MIX_WIDTH

A_QK = 256
LSE_LANES = 32
BAND_TQ = 128

VMEM_LIMIT = 56 * 1024 * 1024

F32 = jnp.float32
BF16 = jnp.bfloat16

_NT = (((1,), (1,)), ((), ()))
_TN = (((0,), (0,)), ((), ()))


def _params(sem, vmem=VMEM_LIMIT):
    return pltpu.CompilerParams(dimension_semantics=sem, vmem_limit_bytes=vmem)


def _rmsnorm_kernel(x_ref, g_ref, o_ref):
    x = x_ref[...]
    ms = jnp.mean(x * x, axis=-1, keepdims=True)
    o_ref[...] = (x * lax.rsqrt(ms + EPS) * g_ref[...]).astype(o_ref.dtype)


def _rmsnorm(x, g, out_dtype, tm=256):
    m, d = x.shape
    return pl.pallas_call(
        _rmsnorm_kernel,
        out_shape=jax.ShapeDtypeStruct((m, d), out_dtype),
        grid=(m // tm,),
        in_specs=[pl.BlockSpec((tm, d), lambda i: (i, 0)),
                  pl.BlockSpec((1, d), lambda i: (0, 0))],
        out_specs=pl.BlockSpec((tm, d), lambda i: (i, 0)),
        compiler_params=_params(("parallel",)),
        name="rmsnorm",
    )(x, g.reshape(1, d))


def _in_proj_kernel(x_ref, w_ref, o_ref, *, factors):
    tn = w_ref.shape[0]
    col = pl.program_id(1) * tn + lax.broadcasted_iota(jnp.int32, (tn, 1), 0)
    s = jnp.ones((tn, 1), F32)
    for start, stop, factor in factors:
        s = jnp.where((col >= start) & (col < stop), factor, s)
    w = (w_ref[...] * s).astype(BF16)
    o_ref[...] = lax.dot_general(x_ref[...], w, _NT, preferred_element_type=F32).astype(o_ref.dtype)


def _in_proj(h, w_t, layer, tm=1024, tn=768):
    m, k = h.shape
    pad = LAT_W - LAT_USED
    assert LAT_W % tn == 0 and PROJ_W % tn == 0 and pad % 8 == 0
    qk = HEAD_DIM ** -0.5 * LOG2E
    factors = ((LAT_USED, LAT_W, 0.0), (BQ_OFF, BQ_OFF + W_B, qk), (DQ_OFF, DQ_OFF + W_D, qk))
    kern = functools.partial(_in_proj_kernel, factors=factors)

    def w_index(i, j):
        first = j * tn - jnp.where(j < LAT_W // tn, 0, pad)
        return layer, pl.multiple_of(first, 8), 0

    return pl.pallas_call(
        kern,
        out_shape=jax.ShapeDtypeStruct((m, PROJ_W), BF16),
        grid=(m // tm, PROJ_W // tn),
        in_specs=[pl.BlockSpec((tm, k), lambda i, j: (i, 0)),
                  pl.BlockSpec((pl.Squeezed(), pl.Element(tn), pl.Element(k)), w_index)],
        out_specs=pl.BlockSpec((tm, tn), lambda i, j: (i, j)),
        compiler_params=_params(("parallel", "parallel")),
        name="in_proj",
    )(h, w_t)


def _out_proj_kernel(*refs):
    *part_refs, w_ref, x_ref, o_ref = refs
    acc = x_ref[...]
    row = 0
    for p_ref in part_refs:
        k = p_ref.shape[1]
        acc = acc + jnp.dot(p_ref[...], w_ref[0, row:row + k, :], preferred_element_type=F32)
        row += k
    o_ref[...] = acc


def _out_proj(parts, w, layer, x, tm=512, tn=1024):
    m = x.shape[0]
    k, n = w.shape[1:]
    assert sum(p.shape[1] for p in parts) == k
    return pl.pallas_call(
        _out_proj_kernel,
        out_shape=jax.ShapeDtypeStruct((m, n), F32),
        grid=(n // tn, m // tm),
        in_specs=[*[pl.BlockSpec((tm, p.shape[1]), lambda j, i: (i, 0)) for p in parts],
                  pl.BlockSpec((1, k, tn), lambda j, i: (layer, 0, j)),
                  pl.BlockSpec((tm, tn), lambda j, i: (i, j))],
        out_specs=pl.BlockSpec((tm, tn), lambda j, i: (i, j)),
        compiler_params=_params(("parallel", "parallel")),
        name="out_proj",
    )(*parts, w, x)


def _rope_tile(x, cos, sin_signed):
    lane = lax.broadcasted_iota(jnp.int32, x.shape, 1)
    partner = jnp.where((lane % 64) < 32, pltpu.roll(x, 96, 1), pltpu.roll(x, 32, 1))
    return x * cos + partner * sin_signed


def _rope_tables(seq):
    inv = ROPE_THETA ** (-jnp.arange(0, 64, 2, dtype=F32) / 64)
    pos = jnp.arange(seq)

    def ang(p):
        return p.astype(F32)[:, None] * inv[None, :]

    def halves(a):
        c, s = jnp.cos(a), jnp.sin(a)
        return jnp.concatenate([c, c], -1), jnp.concatenate([-s, s], -1)

    ct, st = halves(ang(pos))
    zeros = jnp.zeros_like(ct)
    cos_a = jnp.concatenate([ct, zeros], -1)
    sin_a = jnp.concatenate([st, zeros], -1)
    cr, sr = halves(ang(pos // GRID_W))
    cc, sc = halves(ang(pos % GRID_W))
    cos_c = jnp.concatenate([cr, cc], -1)
    sin_c = jnp.concatenate([sr, sc], -1)
    return cos_a, sin_a, cos_c, sin_c


def _prep_a_kernel(lat_ref, gq_ref, gkv_ref, wuq_ref, wukv_ref, cos_ref, sin_ref, q_ref, k_ref, v_ref):
    lat = lat_ref[...].astype(F32)
    cq = lat[:, 0:A_Q_LORA]
    ckv = lat[:, A_Q_LORA:A_Q_LORA + A_KV_LORA]
    kpe = lat[:, A_Q_LORA + A_KV_LORA:A_Q_LORA + A_KV_LORA + 128]

    def rms(x, g):
        return x * lax.rsqrt(jnp.mean(x * x, axis=-1, keepdims=True) + EPS) * g

    qa = jnp.dot(rms(cq, gq_ref[...]).astype(BF16), wuq_ref[...], preferred_element_type=F32)
    kva = jnp.dot(rms(ckv, gkv_ref[...]).astype(BF16), wukv_ref[...], preferred_element_type=F32)
    cos = cos_ref[...]
    sin = sin_ref[...]
    scale = (A_NOPE + A_ROPE) ** -0.5 * LOG2E
    kpe_r = _rope_tile(kpe, cos, sin).astype(BF16)
    for h in range(A_HEADS):
        c0 = h * A_QK
        q_ref[:, c0:c0 + 128] = (qa[:, c0:c0 + 128] * scale).astype(BF16)
        q_ref[:, c0 + 128:c0 + 256] = (_rope_tile(qa[:, c0 + 128:c0 + 256], cos, sin) * scale).astype(BF16)
        k_ref[:, c0:c0 + 128] = kva[:, c0:c0 + 128].astype(BF16)
        k_ref[:, c0 + 128:c0 + 256] = kpe_r
        v_ref[:, h * A_V:(h + 1) * A_V] = kva[:, c0 + 128:c0 + 256].astype(BF16)


def _prep_a(lat, g_qa, g_kva, w_uq_p, w_ukv, cos_a, sin_a, seq, tm=512):
    m = lat.shape[0]
    nsb = seq // tm
    return pl.pallas_call(
        _prep_a_kernel,
        out_shape=(jax.ShapeDtypeStruct((m, A_HEADS * A_QK), BF16),
                   jax.ShapeDtypeStruct((m, A_HEADS * A_QK), BF16),
                   jax.ShapeDtypeStruct((m, W_A), BF16)),
        grid=(m // tm,),
        in_specs=[pl.BlockSpec((tm, LAT_W), lambda i: (i, 0)),
                  pl.BlockSpec((1, A_Q_LORA), lambda i: (0, 0)),
                  pl.BlockSpec((1, A_KV_LORA), lambda i: (0, 0)),
                  pl.BlockSpec((A_Q_LORA, A_HEADS * A_QK), lambda i: (0, 0)),
                  pl.BlockSpec((A_KV_LORA, A_HEADS * A_QK), lambda i: (0, 0)),
                  pl.BlockSpec((tm, 128), lambda i: (i % nsb, 0)),
                  pl.BlockSpec((tm, 128), lambda i: (i % nsb, 0))],
        out_specs=(pl.BlockSpec((tm, A_HEADS * A_QK), lambda i: (i, 0)),
                   pl.BlockSpec((tm, A_HEADS * A_QK), lambda i: (i, 0)),
                   pl.BlockSpec((tm, W_A), lambda i: (i, 0))),
        compiler_params=_params(("parallel",)),
        name="prep_a",
    )(lat, g_qa.reshape(1, -1), g_kva.reshape(1, -1), w_uq_p, w_ukv, cos_a, sin_a)


def _prep_c_kernel(q_in_ref, k_in_ref, gq_ref, gk_ref, cos_ref, sin_ref, q_ref, k_ref):
    cos = cos_ref[...]
    sin = sin_ref[...]

    def norm_rope(x, g):
        x = x.astype(F32)
        xn = x * lax.rsqrt(jnp.mean(x * x, axis=-1, keepdims=True) + EPS) * g
        return _rope_tile(xn, cos, sin)

    scale = HEAD_DIM ** -0.5 * LOG2E
    for h in range(q_ref.shape[1] // HEAD_DIM):
        sl = slice(h * HEAD_DIM, (h + 1) * HEAD_DIM)
        q_ref[:, sl] = (norm_rope(q_in_ref[:, sl], gq_ref[...]) * scale).astype(BF16)

    @pl.when(pl.program_id(1) == 0)
    def _():
        for h in range(C_KV_HEADS):
            sl = slice(h * HEAD_DIM, (h + 1) * HEAD_DIM)
            k_ref[:, sl] = norm_rope(k_in_ref[:, sl], gk_ref[...]).astype(BF16)


def _prep_c(proj, g_qn, g_kn, cos_c, sin_c, seq, tm=512, qw=512):
    m = proj.shape[0]
    nsb = seq // tm
    kw = C_KV_HEADS * HEAD_DIM
    return pl.pallas_call(
        _prep_c_kernel,
        out_shape=(jax.ShapeDtypeStruct((m, W_C), BF16),
                   jax.ShapeDtypeStruct((m, kw), BF16)),
        grid=(m // tm, W_C // qw),
        in_specs=[pl.BlockSpec((tm, qw), lambda i, c: (i, CQ_OFF // qw + c)),
                  pl.BlockSpec((tm, kw), lambda i, c: (i, CK_OFF // kw)),
                  pl.BlockSpec((1, HEAD_DIM), lambda i, c: (0, 0)),
                  pl.BlockSpec((1, HEAD_DIM), lambda i, c: (0, 0)),
                  pl.BlockSpec((tm, 128), lambda i, c: (i % nsb, 0)),
                  pl.BlockSpec((tm, 128), lambda i, c: (i % nsb, 0))],
        out_specs=(pl.BlockSpec((tm, qw), lambda i, c: (i, c)),
                   pl.BlockSpec((tm, kw), lambda i, c: (i, 0))),
        compiler_params=_params(("parallel", "arbitrary")),
        name="prep_c",
    )(proj, proj, g_qn.reshape(1, -1), g_kn.reshape(1, -1), cos_c, sin_c)


def _silu(g):
    return (0.5 * g) * (1.0 + jnp.tanh(0.5 * g))


def _dense_attn_kernel(q_ref, k_ref, v_ref, g_ref, o_ref, *, groups, dq, dv, chunk, strip):
    tq = q_ref.shape[1]
    s_len = k_ref.shape[1]
    if groups == 1:
        q = q_ref[0]
    else:
        q = jnp.concatenate([q_ref[0, :, g * dq:(g + 1) * dq] for g in range(groups)], axis=0)
    n_strip = groups * tq // strip
    qs = [q[c * strip:(c + 1) * strip, :] for c in range(n_strip)]
    state = [None] * n_strip
    for j in range(s_len // chunk):
        kj = k_ref[0, j * chunk:(j + 1) * chunk, :]
        vj = v_ref[0, j * chunk:(j + 1) * chunk, :]
        for c in range(n_strip):
            st = lax.dot_general(kj, qs[c], _NT, preferred_element_type=F32)
            cmax = jnp.max(st, axis=0, keepdims=True)
            m_new = cmax if state[c] is None else jnp.maximum(state[c][0], cmax)
            p = jnp.exp2(st - m_new)
            psum = jnp.sum(p, axis=0, keepdims=True)
            pv = lax.dot_general(vj, p.astype(BF16), _TN, preferred_element_type=F32)
            if state[c] is None:
                state[c] = (m_new, psum, pv)
            else:
                m, l, acc = state[c]
                alpha = jnp.exp2(m - m_new)
                state[c] = (m_new, alpha * l + psum, alpha * acc + pv)
    o_t = jnp.concatenate([acc / l for _, l, acc in state], axis=1)
    for g in range(groups):
        cols = slice(g * dv, (g + 1) * dv)
        gate = _silu(g_ref[0, :, cols].astype(F32))
        o_ref[0, :, cols] = (o_t[:, g * tq:(g + 1) * tq].T * gate).astype(o_ref.dtype)


def _dense_attn(q, k, v, gate, *, kv_heads, groups, dq, dv, k_col0, v_col0, g_col0, tq, name, chunk=512,
                strip=None):
    b, s, _ = q.shape
    kern = functools.partial(_dense_attn_kernel, groups=groups, dq=dq, dv=dv, chunk=chunk,
                             strip=strip or groups * tq)
    return pl.pallas_call(
        kern,
        out_shape=jax.ShapeDtypeStruct((b, s, kv_heads * groups * dv), BF16),
        grid=(b, kv_heads, s // tq),
        in_specs=[pl.BlockSpec((1, tq, groups * dq), lambda bi, hi, qi: (bi, qi, hi)),
                  pl.BlockSpec((1, s, dq), lambda bi, hi, qi: (bi, 0, k_col0 + hi)),
                  pl.BlockSpec((1, s, dv), lambda bi, hi, qi: (bi, 0, v_col0 + hi)),
                  pl.BlockSpec((1, tq, groups * dv), lambda bi, hi, qi: (bi, qi, g_col0 + hi))],
        out_specs=pl.BlockSpec((1, tq, groups * dv), lambda bi, hi, qi: (bi, qi, hi)),
        compiler_params=_params(("parallel", "parallel", "parallel")),
        name=name,
    )(q, k, v, gate)


def _rel_bucket_np(rel):
    nb = N_BUCKETS // 2
    max_exact = nb // 2
    ret = np.where(rel > 0, nb, 0)
    n = np.abs(rel)
    nf = np.maximum(n, 1).astype(np.float32)
    large = max_exact + (np.log(nf / max_exact) / math.log(REL_MAX_DIST / max_exact)
                         * (nb - max_exact)).astype(np.int32)
    large = np.minimum(large, nb - 1)
    return ret + np.where(n < max_exact, n, large)


def _band_bias(tab, dil, half, tq):
    tw = tq + 2 * half
    n = tq + tw - 1
    heads = tab.shape[1]
    tiles = []
    for delta in (0, -half, -2 * half):
        rel = delta + np.arange(n) - (tq - 1)
        inside = np.abs(rel) <= half
        vals = tab[_rel_bucket_np(dil * rel)].astype(F32) * LOG2E
        diag = jnp.where(inside[:, None], vals, NEG).T
        flat = jnp.tile(diag, (1, tq + 1))[:, :tq * (n + 1)]
        tiles.append(flat.reshape(heads, tq, n + 1)[:, ::-1, :tw])
    return jnp.swapaxes(jnp.stack(tiles, 0), -1, -2)


def _banded_kernel(*refs, head_groups, n_sub, half, length, with_sink, with_gate, with_lse):
    q_ref, k_ref, v_ref, bias_ref = refs[:4]
    pos = 4
    sink_ref = gate_ref = None
    if with_sink:
        sink_ref = refs[pos]
        pos += 1
    if with_gate:
        gate_ref = refs[pos]
        pos += 1
    o_ref = refs[pos]
    lse_ref = refs[pos + 1] if with_lse else None

    tq = BAND_TQ
    tw = tq + 2 * half
    nblk = length // tq
    for sb in range(n_sub):
        blk = pl.program_id(2) * n_sub + sb
        ks = pl.multiple_of(jnp.clip(blk * tq - half, 0, length - tw), 64)
        var = jnp.where(blk == 0, 0, jnp.where(blk == nblk - 1, 2, 1))
        rows = slice(sb * tq, (sb + 1) * tq)
        lse_rows = []
        for kvh, heads in head_groups:
            n = len(heads) * tq
            cols = slice(kvh * HEAD_DIM, (kvh + 1) * HEAD_DIM)
            kw = k_ref[0, 0, pl.ds(ks, tw), cols]
            vw = v_ref[0, 0, pl.ds(ks, tw), cols]
            qs = [q_ref[0, 0, rows, h * HEAD_DIM:(h + 1) * HEAD_DIM] for h in heads]
            q = qs[0] if len(heads) == 1 else jnp.concatenate(qs, axis=0)
            bias = [bias_ref[var, h] for h in heads]
            bias = bias[0] if len(heads) == 1 else jnp.concatenate(bias, axis=1)
            st = lax.dot_general(kw, q, _NT, preferred_element_type=F32) + bias
            m = jnp.max(st, axis=0, keepdims=True)
            if with_sink:
                sink = sink_ref[0]
                m = jnp.maximum(m, sink)
            p = jnp.exp2(st - m)
            l = jnp.sum(p, axis=0, keepdims=True)
            if with_sink:
                l = l + jnp.exp2(sink - m)
            o_t = lax.dot_general(vw, p.astype(BF16), _TN, preferred_element_type=F32) / l
            for gi, h in enumerate(heads):
                hcols = slice(h * HEAD_DIM, (h + 1) * HEAD_DIM)
                o_h = o_t[:, gi * tq:(gi + 1) * tq].T
                if with_gate:
                    o_h = o_h * _silu(gate_ref[0, 0, rows, hcols].astype(F32))
                o_ref[0, 0, rows, hcols] = o_h.astype(o_ref.dtype)
            if with_lse:
                lse = (m + jnp.log2(l)) * LN2
                for gi in range(len(heads)):
                    lse_rows.append(jnp.broadcast_to(lse[:, gi * tq:(gi + 1) * tq], (LSE_LANES, tq)))
        if with_lse:
            lse_ref[0, 0, rows, :] = jnp.concatenate(lse_rows, axis=0).T


def _deinterleave_kernel(x_ref, o_ref, scr_ref, *, dil):
    n = scr_ref.shape[1] // dil
    for c in range(scr_ref.shape[0]):
        lanes = slice(c * 128, (c + 1) * 128)
        scr_ref[c] = x_ref[0, :, lanes].astype(F32)
        for r in range(dil):
            o_ref[0, r, :, lanes] = scr_ref[c, pl.ds(r, n, stride=dil), :].astype(o_ref.dtype)


def _deinterleave_qkv(proj3, j, dil, tm=1024):
    b, s, _ = proj3.shape
    col0 = BQ_OFF // W_BG + j
    step = (BK_OFF - BQ_OFF) // W_BG
    return pl.pallas_call(
        functools.partial(_deinterleave_kernel, dil=dil),
        out_shape=jax.ShapeDtypeStruct((b, dil, s // dil, 3 * W_BG), BF16),
        grid=(b, s // tm, 3),
        in_specs=[pl.BlockSpec((1, tm, W_BG), lambda bi, i, c: (bi, i, col0 + step * c))],
        out_specs=pl.BlockSpec((1, dil, tm // dil, W_BG), lambda bi, i, c: (bi, 0, i, c)),
        scratch_shapes=[pltpu.VMEM((W_BG // 128, tm, 128), F32)],
        compiler_params=_params(("parallel", "parallel", "parallel")),
        name=f"deinterleave_{dil}",
    )(proj3)


def _dilated_group(proj3, bias, j, dil):
    b, s, w = proj3.shape
    length = s // dil
    half = B_PAIRS[j][0] // (2 * dil)
    n_sub = min(4, length // BAND_TQ)
    tqs = n_sub * BAND_TQ
    cq, ck, cv = (BQ_OFF + j * W_BG, BK_OFF + j * W_BG, BV_OFF + j * W_BG)
    if dil == 1:
        src = proj3.reshape(b, 1, s, w)
        bq, bk, bv = cq // W_BG, ck // W_BG, cv // W_BG
    else:
        src = _deinterleave_qkv(proj3, j, dil)
        bq, bk, bv = 0, 1, 2
    groups = tuple((h, (h,)) for h in range(B_HEADS_PER_PAIR))
    kern = functools.partial(_banded_kernel, head_groups=groups, n_sub=n_sub, half=half, length=length,
                             with_sink=False, with_gate=False, with_lse=True)
    o, lse = pl.pallas_call(
        kern,
        out_shape=(jax.ShapeDtypeStruct((b, dil, length, W_BG), BF16),
                   jax.ShapeDtypeStruct((b, dil, length, 128), F32)),
        grid=(b, dil, length // tqs),
        in_specs=[pl.BlockSpec((1, 1, tqs, W_BG), lambda bi, r, i: (bi, r, i, bq)),
                  pl.BlockSpec((1, 1, length, W_BG), lambda bi, r, i: (bi, r, 0, bk)),
                  pl.BlockSpec((1, 1, length, W_BG), lambda bi, r, i: (bi, r, 0, bv)),
                  pl.BlockSpec(bias.shape, lambda bi, r, i: (0, 0, 0, 0))],
        out_specs=(pl.BlockSpec((1, 1, tqs, W_BG), lambda bi, r, i: (bi, r, i, 0)),
                   pl.BlockSpec((1, 1, tqs, 128), lambda bi, r, i: (bi, r, i, 0))),
        compiler_params=_params(("parallel", "parallel", "parallel")),
        name=f"dilated_attn_{dil}",
    )(src, src, src, bias)
    return o, lse


def _window_attn(proj3, bias, sink_rows, n_sub=4):
    b, s, w = proj3.shape
    g = D_HEADS // D_KV_HEADS
    gw = g * HEAD_DIM
    tqs = n_sub * BAND_TQ
    kern = functools.partial(_banded_kernel, head_groups=((0, tuple(range(g))),), n_sub=n_sub, half=D_WINDOW,
                             length=s, with_sink=True, with_gate=True, with_lse=False)
    gate0 = (GATE_OFF + W_A + W_B + W_C) // gw
    src = proj3.reshape(b, 1, s, w)
    o = pl.pallas_call(
        kern,
        out_shape=jax.ShapeDtypeStruct((b, 1, s, W_D), BF16),
        grid=(b, D_KV_HEADS, s // tqs),
        in_specs=[pl.BlockSpec((1, 1, tqs, gw), lambda bi, hi, i: (bi, 0, i, DQ_OFF // gw + hi)),
                  pl.BlockSpec((1, 1, s, HEAD_DIM), lambda bi, hi, i: (bi, 0, 0, DK_OFF // HEAD_DIM + hi)),
                  pl.BlockSpec((1, 1, s, HEAD_DIM), lambda bi, hi, i: (bi, 0, 0, DV_OFF // HEAD_DIM + hi)),
                  pl.BlockSpec((3, g) + bias.shape[2:], lambda bi, hi, i: (0, hi, 0, 0)),
                  pl.BlockSpec((1, 1, g * BAND_TQ), lambda bi, hi, i: (hi, 0, 0)),
                  pl.BlockSpec((1, 1, tqs, gw), lambda bi, hi, i: (bi, 0, i, gate0 + hi))],
        out_specs=pl.BlockSpec((1, 1, tqs, gw), lambda bi, hi, i: (bi, 0, i, hi)),
        compiler_params=_params(("parallel", "parallel", "parallel")),
        name="window_attn",
    )(src, src, src, bias, sink_rows, src)
    return o.reshape(b, s, W_D)


def _mix_b_kernel(g0_ref, g1_ref, g2_ref, ob0_ref, ob1_ref, ob2_ref, l0_ref, l1_ref, l2_ref, out_ref,
                  ob_scr, lse_scr):
    gate_refs = (g0_ref, g1_ref, g2_ref)

    def interleave(src_ref, scr_ref):
        dil, n, w = src_ref.shape[1:]
        tiles = []
        for c in range(w // 128):
            lanes = slice(c * 128, (c + 1) * 128)
            if dil == 1:
                tiles.append(src_ref[0, 0, :, lanes].astype(F32))
                continue
            for r in range(dil):
                scr_ref[c, pl.ds(r, n, stride=dil), :] = src_ref[0, r, :, lanes].astype(F32)
            tiles.append(scr_ref[c])
        return tiles

    l0, l1, l2 = (interleave(l_ref, lse_scr)[0] for l_ref in (l0_ref, l1_ref, l2_ref))
    mx = jnp.maximum(jnp.maximum(l0, l1), l2)
    e0, e1, e2 = jnp.exp(l0 - mx), jnp.exp(l1 - mx), jnp.exp(l2 - mx)
    inv = 1.0 / (e0 + e1 + e2)
    for j, (ob_ref, e) in enumerate(((ob0_ref, e0), (ob1_ref, e1), (ob2_ref, e2))):
        alpha = e * inv
        ob = interleave(ob_ref, ob_scr)
        for h in range(B_HEADS_PER_PAIR):
            hcols = slice(h * HEAD_DIM, (h + 1) * HEAD_DIM)
            a_h = alpha[:, h * LSE_LANES:h * LSE_LANES + 1]
            gate = _silu(gate_refs[j][:, hcols].astype(F32))
            out_ref[:, j * W_BG + h * HEAD_DIM:j * W_BG + (h + 1) * HEAD_DIM] = (ob[h] * a_h * gate).astype(BF16)


def _mix_b(proj, o_b, lse_b, seq, tm=512):
    m = proj.shape[0]
    nsb = seq // tm
    gate = lambda c: pl.BlockSpec((tm, W_BG), lambda i: (i, (GATE_OFF + W_A) // W_BG + c))

    def classes(a):
        dil, w = a.shape[1], a.shape[3]
        return pl.BlockSpec((1, dil, tm // dil, w), lambda i: (i // nsb, 0, i % nsb, 0))

    return pl.pallas_call(
        _mix_b_kernel,
        out_shape=jax.ShapeDtypeStruct((m, W_B), BF16),
        grid=(m // tm,),
        in_specs=[gate(0), gate(1), gate(2), *[classes(a) for a in o_b], *[classes(a) for a in lse_b]],
        out_specs=pl.BlockSpec((tm, W_B), lambda i: (i, 0)),
        scratch_shapes=[pltpu.VMEM((W_BG // 128, tm, 128), F32), pltpu.VMEM((1, tm, 128), F32)],
        compiler_params=_params(("parallel",)),
        name="mix_b",
    )(proj, proj, proj, *o_b, *lse_b)


def _layout_w_uq(w):
    w = w.reshape(A_Q_LORA, A_HEADS, A_NOPE + A_ROPE)
    w = jnp.pad(w, ((0, 0), (0, 0), (0, A_QK - A_NOPE - A_ROPE)))
    return w.reshape(A_Q_LORA, A_HEADS * A_QK).astype(BF16)


def _layer(x2, b, s, layer, g_attn, w_in_t, g_qa, g_kva, w_uq, w_ukv, g_qn, g_kn, sinks, w_out_b, rel_bias, tables):
    cos_a, sin_a, cos_c, sin_c = tables
    h = _rmsnorm(x2, g_attn, BF16)
    proj = _in_proj(h, w_in_t, layer)
    proj3 = proj.reshape(b, s, PROJ_W)

    q_a, k_a, v_a = _prep_a(proj, g_qa, g_kva, _layout_w_uq(w_uq), w_ukv.astype(BF16), cos_a, sin_a, s)
    o_a = _dense_attn(q_a.reshape(b, s, -1), k_a.reshape(b, s, -1), v_a.reshape(b, s, -1), proj3,
                      kv_heads=A_HEADS, groups=1, dq=A_QK, dv=A_V, k_col0=0, v_col0=0,
                      g_col0=GATE_OFF // A_V, tq=2048, name="attn_a")

    o_b, lse_b = [], []
    for j, (win, dil) in enumerate(B_PAIRS):
        tab = rel_bias[:, j * B_HEADS_PER_PAIR:(j + 1) * B_HEADS_PER_PAIR]
        bias = _band_bias(tab, dil, win // (2 * dil), BAND_TQ)
        o, lse = _dilated_group(proj3, bias, j, dil)
        o_b.append(o)
        lse_b.append(lse)

    q_c, k_c = _prep_c(proj, g_qn, g_kn, cos_c, sin_c, s)
    gc = C_HEADS // C_KV_HEADS
    o_c = _dense_attn(q_c.reshape(b, s, -1), k_c.reshape(b, s, -1), proj3, proj3,
                      kv_heads=C_KV_HEADS, groups=gc, dq=HEAD_DIM, dv=HEAD_DIM,
                      k_col0=0, v_col0=CV_OFF // HEAD_DIM, g_col0=(GATE_OFF + W_A + W_B) // (gc * HEAD_DIM),
                      tq=512, name="attn_c")

    g = D_HEADS // D_KV_HEADS
    bias_d = _band_bias(rel_bias[:, B_HEADS:], 1, D_WINDOW, BAND_TQ)
    sink_rows = jnp.repeat(sinks.astype(F32) * LOG2E, BAND_TQ).reshape(D_KV_HEADS, 1, g * BAND_TQ)
    o_d = _window_attn(proj3, bias_d, sink_rows)

    parts = (o_a.reshape(b * s, W_A), _mix_b(proj, o_b, lse_b, s), o_c.reshape(b * s, W_C), o_d.reshape(b * s, W_D))
    return _out_proj(parts, w_out_b, layer, x2)


def kernel(x, g_attn, w_in, g_qa, g_kva, w_uq, w_ukv, g_qn, g_kn, sinks, w_out, rel_bias, g_final):
    b, s, d = x.shape
    depth = w_in.shape[0]
    tables = _rope_tables(s)
    x2 = x.reshape(b * s, d)
    w_in_t = jnp.swapaxes(w_in, 1, 2)
    w_out_b = w_out.astype(BF16)
    for l in range(depth):
        x2 = _layer(x2, b, s, l, g_attn[l], w_in_t, g_qa[l], g_kva[l], w_uq[l], w_ukv[l], g_qn[l], g_kn[l],
                    sinks[l], w_out_b, rel_bias, tables)
    return _rmsnorm(x2, g_final, F32).reshape(b, s, d)
```

```python
import functools
import math

import numpy as np
import jax
import jax.numpy as jnp
from jax import lax
from jax.experimental import pallas as pl
from jax.experimental.pallas import tpu as pltpu

D_MODEL = 4096
HEAD_DIM = 128
A_HEADS = 8
A_Q_LORA = 768
A_KV_LORA = 512
A_NOPE = 128
A_ROPE = 64
A_V = 128
B_PAIRS = ((128, 1), (512, 4), (2048, 16))
B_HEADS_PER_PAIR = 4
B_HEADS = B_HEADS_PER_PAIR * len(B_PAIRS)
C_HEADS = 8
C_KV_HEADS = 2
D_HEADS = 8
D_KV_HEADS = 2
D_WINDOW = 128
GRID_W = 64
ROPE_THETA = 10000.0
N_BUCKETS = 32
REL_MAX_DIST = 1024
EPS = 1e-6
NEG = -1e30
LOG2E = math.log2(math.e)
LN2 = math.log(2.0)

W_A = A_HEADS * A_V
W_B = B_HEADS * HEAD_DIM
W_C = C_HEADS * HEAD_DIM
W_D = D_HEADS * HEAD_DIM
W_BG = B_HEADS_PER_PAIR * HEAD_DIM
MIX_WIDTH = W_A + W_B + W_C + W_D

LAT_USED = A_Q_LORA + A_KV_LORA + A_ROPE
LAT_W = 1536
BQ_OFF = LAT_W
BK_OFF = BQ_OFF + W_B
BV_OFF = BK_OFF + W_B
CQ_OFF = BV_OFF + W_B
CK_OFF = CQ_OFF + W_C
CV_OFF = CK_OFF + C_KV_HEADS * HEAD_DIM
DQ_OFF = CV_OFF + C_KV_HEADS * HEAD_DIM
DK_OFF = DQ_OFF + W_D
DV_OFF = DK_OFF + D_KV_HEADS * HEAD_DIM
GATE_OFF = DV_OFF + D_KV_HEADS * HEAD_DIM
PROJ_W = GATE_OFF + MIX_WIDTH

A_QK = 256
LSE_LANES = 32
BAND_TQ = 128

VMEM_LIMIT = 56 * 1024 * 1024

F32 = jnp.float32
BF16 = jnp.bfloat16

_NT = (((1,), (1,)), ((), ()))
_TN = (((0,), (0,)), ((), ()))


def _params(sem, vmem=VMEM_LIMIT):
    return pltpu.CompilerParams(dimension_semantics=sem, vmem_limit_bytes=vmem)


def _rmsnorm_kernel(x_ref, g_ref, o_ref):
    x = x_ref[...]
    ms = jnp.mean(x * x, axis=-1, keepdims=True)
    o_ref[...] = (x * lax.rsqrt(ms + EPS) * g_ref[...]).astype(o_ref.dtype)


def _rmsnorm(x, g, out_dtype, tm=256):
    m, d = x.shape
    return pl.pallas_call(
        _rmsnorm_kernel,
        out_shape=jax.ShapeDtypeStruct((m, d), out_dtype),
        grid=(m // tm,),
        in_specs=[pl.BlockSpec((tm, d), lambda i: (i, 0)),
                  pl.BlockSpec((1, d), lambda i: (0, 0))],
        out_specs=pl.BlockSpec((tm, d), lambda i: (i, 0)),
        compiler_params=_params(("parallel",)),
        name="rmsnorm",
    )(x, g.reshape(1, d))


def _in_proj_kernel(x_ref, w_ref, o_ref, *, factors):
    tn = w_ref.shape[0]
    col = pl.program_id(1) * tn + lax.broadcasted_iota(jnp.int32, (tn, 1), 0)
    s = jnp.ones((tn, 1), F32)
    for start, stop, factor in factors:
        s = jnp.where((col >= start) & (col < stop), factor, s)
    w = (w_ref[...] * s).astype(BF16)
    o_ref[...] = lax.dot_general(x_ref[...], w, _NT, preferred_element_type=F32).astype(o_ref.dtype)


def _in_proj(h, w_t, layer, tm=1024, tn=768):
    m, k = h.shape
    pad = LAT_W - LAT_USED
    assert LAT_W % tn == 0 and PROJ_W % tn == 0 and pad % 8 == 0
    qk = HEAD_DIM ** -0.5 * LOG2E
    factors = ((LAT_USED, LAT_W, 0.0), (BQ_OFF, BQ_OFF + W_B, qk), (DQ_OFF, DQ_OFF + W_D, qk))
    kern = functools.partial(_in_proj_kernel, factors=factors)

    def w_index(i, j):
        first = j * tn - jnp.where(j < LAT_W // tn, 0, pad)
        return layer, pl.multiple_of(first, 8), 0

    return pl.pallas_call(
        kern,
        out_shape=jax.ShapeDtypeStruct((m, PROJ_W), BF16),
        grid=(m // tm, PROJ_W // tn),
        in_specs=[pl.BlockSpec((tm, k), lambda i, j: (i, 0)),
                  pl.BlockSpec((pl.Squeezed(), pl.Element(tn), pl.Element(k)), w_index)],
        out_specs=pl.BlockSpec((tm, tn), lambda i, j: (i, j)),
        compiler_params=_params(("parallel", "parallel")),
        name="in_proj",
    )(h, w_t)


def _out_proj_kernel(*refs):
    *part_refs, w_ref, x_ref, o_ref = refs
    acc = x_ref[...]
    row = 0
    for p_ref in part_refs:
        k = p_ref.shape[1]
        acc = acc + jnp.dot(p_ref[...], w_ref[0, row:row + k, :], preferred_element_type=F32)
        row += k
    o_ref[...] = acc


def _out_proj(parts, w, layer, x, tm=512, tn=1024):
    m = x.shape[0]
    k, n = w.shape[1:]
    assert sum(p.shape[1] for p in parts) == k
    return pl.pallas_call(
        _out_proj_kernel,
        out_shape=jax.ShapeDtypeStruct((m, n), F32),
        grid=(n // tn, m // tm),
        in_specs=[*[pl.BlockSpec((tm, p.shape[1]), lambda j, i: (i, 0)) for p in parts],
                  pl.BlockSpec((1, k, tn), lambda j, i: (layer, 0, j)),
                  pl.BlockSpec((tm, tn), lambda j, i: (i, j))],
        out_specs=pl.BlockSpec((tm, tn), lambda j, i: (i, j)),
        compiler_params=_params(("parallel", "parallel")),
        name="out_proj",
    )(*parts, w, x)


def _rope_tile(x, cos, sin_signed):
    lane = lax.broadcasted_iota(jnp.int32, x.shape, 1)
    partner = jnp.where((lane % 64) < 32, pltpu.roll(x, 96, 1), pltpu.roll(x, 32, 1))
    return x * cos + partner * sin_signed


def _rope_tables(seq):
    inv = ROPE_THETA ** (-jnp.arange(0, 64, 2, dtype=F32) / 64)
    pos = jnp.arange(seq)

    def ang(p):
        return p.astype(F32)[:, None] * inv[None, :]

    def halves(a):
        c, s = jnp.cos(a), jnp.sin(a)
        return jnp.concatenate([c, c], -1), jnp.concatenate([-s, s], -1)

    ct, st = halves(ang(pos))
    zeros = jnp.zeros_like(ct)
    cos_a = jnp.concatenate([ct, zeros], -1)
    sin_a = jnp.concatenate([st, zeros], -1)
    cr, sr = halves(ang(pos // GRID_W))
    cc, sc = halves(ang(pos % GRID_W))
    cos_c = jnp.concatenate([cr, cc], -1)
    sin_c = jnp.concatenate([sr, sc], -1)
    return cos_a, sin_a, cos_c, sin_c


def _prep_a_kernel(lat_ref, gq_ref, gkv_ref, wuq_ref, wukv_ref, cos_ref, sin_ref, q_ref, k_ref, v_ref):
    lat = lat_ref[...].astype(F32)
    cq = lat[:, 0:A_Q_LORA]
    ckv = lat[:, A_Q_LORA:A_Q_LORA + A_KV_LORA]
    kpe = lat[:, A_Q_LORA + A_KV_LORA:A_Q_LORA + A_KV_LORA + 128]

    def rms(x, g):
        return x * lax.rsqrt(jnp.mean(x * x, axis=-1, keepdims=True) + EPS) * g

    qa = jnp.dot(rms(cq, gq_ref[...]).astype(BF16), wuq_ref[...], preferred_element_type=F32)
    kva = jnp.dot(rms(ckv, gkv_ref[...]).astype(BF16), wukv_ref[...], preferred_element_type=F32)
    cos = cos_ref[...]
    sin = sin_ref[...]
    scale = (A_NOPE + A_ROPE) ** -0.5 * LOG2E
    kpe_r = _rope_tile(kpe, cos, sin).astype(BF16)
    for h in range(A_HEADS):
        c0 = h * A_QK
        q_ref[:, c0:c0 + 128] = (qa[:, c0:c0 + 128] * scale).astype(BF16)
        q_ref[:, c0 + 128:c0 + 256] = (_rope_tile(qa[:, c0 + 128:c0 + 256], cos, sin) * scale).astype(BF16)
        k_ref[:, c0:c0 + 128] = kva[:, c0:c0 + 128].astype(BF16)
        k_ref[:, c0 + 128:c0 + 256] = kpe_r
        v_ref[:, h * A_V:(h + 1) * A_V] = kva[:, c0 + 128:c0 + 256].astype(BF16)


def _prep_a(lat, g_qa, g_kva, w_uq_p, w_ukv, cos_a, sin_a, seq, tm=512):
    m = lat.shape[0]
    nsb = seq // tm
    return pl.pallas_call(
        _prep_a_kernel,
        out_shape=(jax.ShapeDtypeStruct((m, A_HEADS * A_QK), BF16),
                   jax.ShapeDtypeStruct((m, A_HEADS * A_QK), BF16),
                   jax.ShapeDtypeStruct((m, W_A), BF16)),
        grid=(m // tm,),
        in_specs=[pl.BlockSpec((tm, LAT_W), lambda i: (i, 0)),
                  pl.BlockSpec((1, A_Q_LORA), lambda i: (0, 0)),
                  pl.BlockSpec((1, A_KV_LORA), lambda i: (0, 0)),
                  pl.BlockSpec((A_Q_LORA, A_HEADS * A_QK), lambda i: (0, 0)),
                  pl.BlockSpec((A_KV_LORA, A_HEADS * A_QK), lambda i: (0, 0)),
                  pl.BlockSpec((tm, 128), lambda i: (i % nsb, 0)),
                  pl.BlockSpec((tm, 128), lambda i: (i % nsb, 0))],
        out_specs=(pl.BlockSpec((tm, A_HEADS * A_QK), lambda i: (i, 0)),
                   pl.BlockSpec((tm, A_HEADS * A_QK), lambda i: (i, 0)),
                   pl.BlockSpec((tm, W_A), lambda i: (i, 0))),
        compiler_params=_params(("parallel",)),
        name="prep_a",
    )(lat, g_qa.reshape(1, -1), g_kva.reshape(1, -1), w_uq_p, w_ukv, cos_a, sin_a)


def _prep_c_kernel(q_in_ref, k_in_ref, gq_ref, gk_ref, cos_ref, sin_ref, q_ref, k_ref):
    cos = cos_ref[...]
    sin = sin_ref[...]

    def norm_rope(x, g):
        x = x.astype(F32)
        xn = x * lax.rsqrt(jnp.mean(x * x, axis=-1, keepdims=True) + EPS) * g
        return _rope_tile(xn, cos, sin)

    scale = HEAD_DIM ** -0.5 * LOG2E
    for h in range(q_ref.shape[1] // HEAD_DIM):
        sl = slice(h * HEAD_DIM, (h + 1) * HEAD_DIM)
        q_ref[:, sl] = (norm_rope(q_in_ref[:, sl], gq_ref[...]) * scale).astype(BF16)

    @pl.when(pl.program_id(1) == 0)
    def _():
        for h in range(C_KV_HEADS):
            sl = slice(h * HEAD_DIM, (h + 1) * HEAD_DIM)
            k_ref[:, sl] = norm_rope(k_in_ref[:, sl], gk_ref[...]).astype(BF16)


def _prep_c(proj, g_qn, g_kn, cos_c, sin_c, seq, tm=1024, qw=512):
    m = proj.shape[0]
    nsb = seq // tm
    kw = C_KV_HEADS * HEAD_DIM
    return pl.pallas_call(
        _prep_c_kernel,
        out_shape=(jax.ShapeDtypeStruct((m, W_C), BF16),
                   jax.ShapeDtypeStruct((m, kw), BF16)),
        grid=(m // tm, W_C // qw),
        in_specs=[pl.BlockSpec((tm, qw), lambda i, c: (i, CQ_OFF // qw + c)),
                  pl.BlockSpec((tm, kw), lambda i, c: (i, CK_OFF // kw)),
                  pl.BlockSpec((1, HEAD_DIM), lambda i, c: (0, 0)),
                  pl.BlockSpec((1, HEAD_DIM), lambda i, c: (0, 0)),
                  pl.BlockSpec((tm, 128), lambda i, c: (i % nsb, 0)),
                  pl.BlockSpec((tm, 128), lambda i, c: (i % nsb, 0))],
        out_specs=(pl.BlockSpec((tm, qw), lambda i, c: (i, c)),
                   pl.BlockSpec((tm, kw), lambda i, c: (i, 0))),
        compiler_params=_params(("parallel", "arbitrary")),
        name="prep_c",
    )(proj, proj, g_qn.reshape(1, -1), g_kn.reshape(1, -1), cos_c, sin_c)


def _silu(g):
    return (0.5 * g) * (1.0 + jnp.tanh(0.5 * g))


def _dense_attn_kernel(q_ref, k_ref, v_ref, g_ref, o_ref, *, groups, dq, dv, chunk, strip):
    tq = q_ref.shape[1]
    s_len = k_ref.shape[1]
    if groups == 1:
        q = q_ref[0]
    else:
        q = jnp.concatenate([q_ref[0, :, g * dq:(g + 1) * dq] for g in range(groups)], axis=0)
    n_strip = groups * tq // strip
    qs = [q[c * strip:(c + 1) * strip, :] for c in range(n_strip)]
    state = [None] * n_strip
    for j in range(s_len // chunk):
        kj = k_ref[0, j * chunk:(j + 1) * chunk, :]
        vj = v_ref[0, j * chunk:(j + 1) * chunk, :]
        for c in range(n_strip):
            st = lax.dot_general(kj, qs[c], _NT, preferred_element_type=F32)
            cmax = jnp.max(st, axis=0, keepdims=True)
            m_new = cmax if state[c] is None else jnp.maximum(state[c][0], cmax)
            p = jnp.exp2(st - m_new)
            psum = jnp.sum(p, axis=0, keepdims=True)
            pv = lax.dot_general(vj, p.astype(BF16), _TN, preferred_element_type=F32)
            if state[c] is None:
                state[c] = (m_new, psum, pv)
            else:
                m, l, acc = state[c]
                alpha = jnp.exp2(m - m_new)
                state[c] = (m_new, alpha * l + psum, alpha * acc + pv)
    o_t = jnp.concatenate([acc / l for _, l, acc in state], axis=1)
    for g in range(groups):
        cols = slice(g * dv, (g + 1) * dv)
        gate = _silu(g_ref[0, :, cols].astype(F32))
        o_ref[0, :, cols] = (o_t[:, g * tq:(g + 1) * tq].T * gate).astype(o_ref.dtype)


def _dense_attn(q, k, v, gate, *, kv_heads, groups, dq, dv, k_col0, v_col0, g_col0, tq, name, chunk=512,
                strip=None):
    b, s, _ = q.shape
    kern = functools.partial(_dense_attn_kernel, groups=groups, dq=dq, dv=dv, chunk=chunk,
                             strip=strip or groups * tq)
    return pl.pallas_call(
        kern,
        out_shape=jax.ShapeDtypeStruct((b, s, kv_heads * groups * dv), BF16),
        grid=(b, kv_heads, s // tq),
        in_specs=[pl.BlockSpec((1, tq, groups * dq), lambda bi, hi, qi: (bi, qi, hi)),
                  pl.BlockSpec((1, s, dq), lambda bi, hi, qi: (bi, 0, k_col0 + hi)),
                  pl.BlockSpec((1, s, dv), lambda bi, hi, qi: (bi, 0, v_col0 + hi)),
                  pl.BlockSpec((1, tq, groups * dv), lambda bi, hi, qi: (bi, qi, g_col0 + hi))],
        out_specs=pl.BlockSpec((1, tq, groups * dv), lambda bi, hi, qi: (bi, qi, hi)),
        compiler_params=_params(("parallel", "parallel", "parallel")),
        name=name,
    )(q, k, v, gate)


def _rel_bucket_np(rel):
    nb = N_BUCKETS // 2
    max_exact = nb // 2
    ret = np.where(rel > 0, nb, 0)
    n = np.abs(rel)
    nf = np.maximum(n, 1).astype(np.float32)
    large = max_exact + (np.log(nf / max_exact) / math.log(REL_MAX_DIST / max_exact)
                         * (nb - max_exact)).astype(np.int32)
    large = np.minimum(large, nb - 1)
    return ret + np.where(n < max_exact, n, large)


def _band_bias(tab, dil, half, tq):
    tw = tq + 2 * half
    n = tq + tw - 1
    heads = tab.shape[1]
    tiles = []
    for delta in (0, -half, -2 * half):
        rel = delta + np.arange(n) - (tq - 1)
        inside = np.abs(rel) <= half
        vals = tab[_rel_bucket_np(dil * rel)].astype(F32) * LOG2E
        diag = jnp.where(inside[:, None], vals, NEG).T
        flat = jnp.tile(diag, (1, tq + 1))[:, :tq * (n + 1)]
        tiles.append(flat.reshape(heads, tq, n + 1)[:, ::-1, :tw])
    return jnp.swapaxes(jnp.stack(tiles, 0), -1, -2)


def _banded_kernel(*refs, head_groups, n_sub, half, length, with_sink, with_gate, with_lse):
    q_ref, k_ref, v_ref, bias_ref = refs[:4]
    pos = 4
    sink_ref = gate_ref = None
    if with_sink:
        sink_ref = refs[pos]
        pos += 1
    if with_gate:
        gate_ref = refs[pos]
        pos += 1
    o_ref = refs[pos]
    lse_ref = refs[pos + 1] if with_lse else None

    tq = BAND_TQ
    tw = tq + 2 * half
    nblk = length // tq
    for sb in range(n_sub):
        blk = pl.program_id(2) * n_sub + sb
        ks = pl.multiple_of(jnp.clip(blk * tq - half, 0, length - tw), 64)
        var = jnp.where(blk == 0, 0, jnp.where(blk == nblk - 1, 2, 1))
        rows = slice(sb * tq, (sb + 1) * tq)
        lse_rows = []
        for kvh, heads in head_groups:
            n = len(heads) * tq
            cols = slice(kvh * HEAD_DIM, (kvh + 1) * HEAD_DIM)
            kw = k_ref[0, 0, pl.ds(ks, tw), cols]
            vw = v_ref[0, 0, pl.ds(ks, tw), cols]
            qs = [q_ref[0, 0, rows, h * HEAD_DIM:(h + 1) * HEAD_DIM] for h in heads]
            q = qs[0] if len(heads) == 1 else jnp.concatenate(qs, axis=0)
            bias = [bias_ref[var, h] for h in heads]
            bias = bias[0] if len(heads) == 1 else jnp.concatenate(bias, axis=1)
            st = lax.dot_general(kw, q, _NT, preferred_element_type=F32) + bias
            m = jnp.max(st, axis=0, keepdims=True)
            if with_sink:
                sink = sink_ref[0]
                m = jnp.maximum(m, sink)
            p = jnp.exp2(st - m)
            l = jnp.sum(p, axis=0, keepdims=True)
            if with_sink:
                l = l + jnp.exp2(sink - m)
            o_t = lax.dot_general(vw, p.astype(BF16), _TN, preferred_element_type=F32) / l
            for gi, h in enumerate(heads):
                hcols = slice(h * HEAD_DIM, (h + 1) * HEAD_DIM)
                o_h = o_t[:, gi * tq:(gi + 1) * tq].T
                if with_gate:
                    o_h = o_h * _silu(gate_ref[0, 0, rows, hcols].astype(F32))
                o_ref[0, 0, rows, hcols] = o_h.astype(o_ref.dtype)
            if with_lse:
                lse = (m + jnp.log2(l)) * LN2
                for gi in range(len(heads)):
                    lse_rows.append(jnp.broadcast_to(lse[:, gi * tq:(gi + 1) * tq], (LSE_LANES, tq)))
        if with_lse:
            lse_ref[0, 0, rows, :] = jnp.concatenate(lse_rows, axis=0).T


def _deinterleave_kernel(x_ref, o_ref, scr_ref, *, dil):
    n = scr_ref.shape[1] // dil
    for c in range(scr_ref.shape[0]):
        lanes = slice(c * 128, (c + 1) * 128)
        scr_ref[c] = x_ref[0, :, lanes].astype(F32)
        for r in range(dil):
            o_ref[0, r, :, lanes] = scr_ref[c, pl.ds(r, n, stride=dil), :].astype(o_ref.dtype)


def _deinterleave_qkv(proj3, j, dil, tm=1024):
    b, s, _ = proj3.shape
    col0 = BQ_OFF // W_BG + j
    step = (BK_OFF - BQ_OFF) // W_BG
    return pl.pallas_call(
        functools.partial(_deinterleave_kernel, dil=dil),
        out_shape=jax.ShapeDtypeStruct((b, dil, s // dil, 3 * W_BG), BF16),
        grid=(b, s // tm, 3),
        in_specs=[pl.BlockSpec((1, tm, W_BG), lambda bi, i, c: (bi, i, col0 + step * c))],
        out_specs=pl.BlockSpec((1, dil, tm // dil, W_BG), lambda bi, i, c: (bi, 0, i, c)),
        scratch_shapes=[pltpu.VMEM((W_BG // 128, tm, 128), F32)],
        compiler_params=_params(("parallel", "parallel", "parallel")),
        name=f"deinterleave_{dil}",
    )(proj3)


def _dilated_group(proj3, bias, j, dil):
    b, s, w = proj3.shape
    length = s // dil
    half = B_PAIRS[j][0] // (2 * dil)
    n_sub = min(8, length // BAND_TQ)
    tqs = n_sub * BAND_TQ
    cq, ck, cv = (BQ_OFF + j * W_BG, BK_OFF + j * W_BG, BV_OFF + j * W_BG)
    if dil == 1:
        src = proj3.reshape(b, 1, s, w)
        bq, bk, bv = cq // W_BG, ck // W_BG, cv // W_BG
    else:
        src = _deinterleave_qkv(proj3, j, dil)
        bq, bk, bv = 0, 1, 2
    groups = tuple((h, (h,)) for h in range(B_HEADS_PER_PAIR))
    kern = functools.partial(_banded_kernel, head_groups=groups, n_sub=n_sub, half=half, length=length,
                             with_sink=False, with_gate=False, with_lse=True)
    o, lse = pl.pallas_call(
        kern,
        out_shape=(jax.ShapeDtypeStruct((b, dil, length, W_BG), BF16),
                   jax.ShapeDtypeStruct((b, dil, length, 128), F32)),
        grid=(b, dil, length // tqs),
        in_specs=[pl.BlockSpec((1, 1, tqs, W_BG), lambda bi, r, i: (bi, r, i, bq)),
                  pl.BlockSpec((1, 1, length, W_BG), lambda bi, r, i: (bi, r, 0, bk)),
                  pl.BlockSpec((1, 1, length, W_BG), lambda bi, r, i: (bi, r, 0, bv)),
                  pl.BlockSpec(bias.shape, lambda bi, r, i: (0, 0, 0, 0))],
        out_specs=(pl.BlockSpec((1, 1, tqs, W_BG), lambda bi, r, i: (bi, r, i, 0)),
                   pl.BlockSpec((1, 1, tqs, 128), lambda bi, r, i: (bi, r, i, 0))),
        compiler_params=_params(("parallel", "parallel", "parallel")),
        name=f"dilated_attn_{dil}",
    )(src, src, src, bias)
    return o, lse


def _window_attn(proj3, bias, sink_rows, n_sub=8):
    b, s, w = proj3.shape
    g = D_HEADS // D_KV_HEADS
    gw = g * HEAD_DIM
    tqs = n_sub * BAND_TQ
    kern = functools.partial(_banded_kernel, head_groups=((0, tuple(range(g))),), n_sub=n_sub, half=D_WINDOW,
                             length=s, with_sink=True, with_gate=True, with_lse=False)
    gate0 = (GATE_OFF + W_A + W_B + W_C) // gw
    src = proj3.reshape(b, 1, s, w)
    o = pl.pallas_call(
        kern,
        out_shape=jax.ShapeDtypeStruct((b, 1, s, W_D), BF16),
        grid=(b, D_KV_HEADS, s // tqs),
        in_specs=[pl.BlockSpec((1, 1, tqs, gw), lambda bi, hi, i: (bi, 0, i, DQ_OFF // gw + hi)),
                  pl.BlockSpec((1, 1, s, HEAD_DIM), lambda bi, hi, i: (bi, 0, 0, DK_OFF // HEAD_DIM + hi)),
                  pl.BlockSpec((1, 1, s, HEAD_DIM), lambda bi, hi, i: (bi, 0, 0, DV_OFF // HEAD_DIM + hi)),
                  pl.BlockSpec((3, g) + bias.shape[2:], lambda bi, hi, i: (0, hi, 0, 0)),
                  pl.BlockSpec((1, 1, g * BAND_TQ), lambda bi, hi, i: (hi, 0, 0)),
                  pl.BlockSpec((1, 1, tqs, gw), lambda bi, hi, i: (bi, 0, i, gate0 + hi))],
        out_specs=pl.BlockSpec((1, 1, tqs, gw), lambda bi, hi, i: (bi, 0, i, hi)),
        compiler_params=_params(("parallel", "parallel", "parallel")),
        name="window_attn",
    )(src, src, src, bias, sink_rows, src)
    return o.reshape(b, s, W_D)


def _mix_b_kernel(g0_ref, g1_ref, g2_ref, ob0_ref, ob1_ref, ob2_ref, l0_ref, l1_ref, l2_ref, out_ref,
                  ob_scr, lse_scr):
    gate_refs = (g0_ref, g1_ref, g2_ref)

    def interleave(src_ref, scr_ref):
        dil, n, w = src_ref.shape[1:]
        tiles = []
        for c in range(w // 128):
            lanes = slice(c * 128, (c + 1) * 128)
            if dil == 1:
                tiles.append(src_ref[0, 0, :, lanes].astype(F32))
                continue
            for r in range(dil):
                scr_ref[c, pl.ds(r, n, stride=dil), :] = src_ref[0, r, :, lanes].astype(F32)
            tiles.append(scr_ref[c])
        return tiles

    l0, l1, l2 = (interleave(l_ref, lse_scr)[0] for l_ref in (l0_ref, l1_ref, l2_ref))
    mx = jnp.maximum(jnp.maximum(l0, l1), l2)
    e0, e1, e2 = jnp.exp(l0 - mx), jnp.exp(l1 - mx), jnp.exp(l2 - mx)
    inv = 1.0 / (e0 + e1 + e2)
    for j, (ob_ref, e) in enumerate(((ob0_ref, e0), (ob1_ref, e1), (ob2_ref, e2))):
        alpha = e * inv
        ob = interleave(ob_ref, ob_scr)
        for h in range(B_HEADS_PER_PAIR):
            hcols = slice(h * HEAD_DIM, (h + 1) * HEAD_DIM)
            a_h = alpha[:, h * LSE_LANES:h * LSE_LANES + 1]
            gate = _silu(gate_refs[j][:, hcols].astype(F32))
            out_ref[:, j * W_BG + h * HEAD_DIM:j * W_BG + (h + 1) * HEAD_DIM] = (ob[h] * a_h * gate).astype(BF16)


def _mix_b(proj, o_b, lse_b, seq, tm=512):
    m = proj.shape[0]
    nsb = seq // tm
    gate = lambda c: pl.BlockSpec((tm, W_BG), lambda i: (i, (GATE_OFF + W_A) // W_BG + c))

    def classes(a):
        dil, w = a.shape[1], a.shape[3]
        return pl.BlockSpec((1, dil, tm // dil, w), lambda i: (i // nsb, 0, i % nsb, 0))

    return pl.pallas_call(
        _mix_b_kernel,
        out_shape=jax.ShapeDtypeStruct((m, W_B), BF16),
        grid=(m // tm,),
        in_specs=[gate(0), gate(1), gate(2), *[classes(a) for a in o_b], *[classes(a) for a in lse_b]],
        out_specs=pl.BlockSpec((tm, W_B), lambda i: (i, 0)),
        scratch_shapes=[pltpu.VMEM((W_BG // 128, tm, 128), F32), pltpu.VMEM((1, tm, 128), F32)],
        compiler_params=_params(("parallel",)),
        name="mix_b",
    )(proj, proj, proj, *o_b, *lse_b)


def _layout_w_uq(w):
    w = w.reshape(A_Q_LORA, A_HEADS, A_NOPE + A_ROPE)
    w = jnp.pad(w, ((0, 0), (0, 0), (0, A_QK - A_NOPE - A_ROPE)))
    return w.reshape(A_Q_LORA, A_HEADS * A_QK).astype(BF16)


def _layer(x2, b, s, layer, g_attn, w_in_t, g_qa, g_kva, w_uq, w_ukv, g_qn, g_kn, sinks, w_out_b, rel_bias, tables):
    cos_a, sin_a, cos_c, sin_c = tables
    h = _rmsnorm(x2, g_attn, BF16)
    proj = _in_proj(h, w_in_t, layer)
    proj3 = proj.reshape(b, s, PROJ_W)

    q_a, k_a, v_a = _prep_a(proj, g_qa, g_kva, _layout_w_uq(w_uq), w_ukv.astype(BF16), cos_a, sin_a, s)
    o_a = _dense_attn(q_a.reshape(b, s, -1), k_a.reshape(b, s, -1), v_a.reshape(b, s, -1), proj3,
                      kv_heads=A_HEADS, groups=1, dq=A_QK, dv=A_V, k_col0=0, v_col0=0,
                      g_col0=GATE_OFF // A_V, tq=2048, name="attn_a")

    o_b, lse_b = [], []
    for j, (win, dil) in enumerate(B_PAIRS):
        tab = rel_bias[:, j * B_HEADS_PER_PAIR:(j + 1) * B_HEADS_PER_PAIR]
        bias = _band_bias(tab, dil, win // (2 * dil), BAND_TQ)
        o, lse = _dilated_group(proj3, bias, j, dil)
        o_b.append(o)
        lse_b.append(lse)

    q_c, k_c = _prep_c(proj, g_qn, g_kn, cos_c, sin_c, s)
    gc = C_HEADS // C_KV_HEADS
    o_c = _dense_attn(q_c.reshape(b, s, -1), k_c.reshape(b, s, -1), proj3, proj3,
                      kv_heads=C_KV_HEADS, groups=gc, dq=HEAD_DIM, dv=HEAD_DIM,
                      k_col0=0, v_col0=CV_OFF // HEAD_DIM, g_col0=(GATE_OFF + W_A + W_B) // (gc * HEAD_DIM),
                      tq=512, name="attn_c")

    g = D_HEADS // D_KV_HEADS
    bias_d = _band_bias(rel_bias[:, B_HEADS:], 1, D_WINDOW, BAND_TQ)
    sink_rows = jnp.repeat(sinks.astype(F32) * LOG2E, BAND_TQ).reshape(D_KV_HEADS, 1, g * BAND_TQ)
    o_d = _window_attn(proj3, bias_d, sink_rows)

    parts = (o_a.reshape(b * s, W_A), _mix_b(proj, o_b, lse_b, s), o_c.reshape(b * s, W_C), o_d.reshape(b * s, W_D))
    return _out_proj(parts, w_out_b, layer, x2)


def kernel(x, g_attn, w_in, g_qa, g_kva, w_uq, w_ukv, g_qn, g_kn, sinks, w_out, rel_bias, g_final):
    b, s, d = x.shape
    depth = w_in.shape[0]
    tables = _rope_tables(s)
    x2 = x.reshape(b * s, d)
    w_in_t = jnp.swapaxes(w_in, 1, 2)
    w_out_b = w_out.astype(BF16)
    for l in range(depth):
        x2 = _layer(x2, b, s, l, g_attn[l], w_in_t, g_qa[l], g_kva[l], w_uq[l], w_ukv[l], g_qn[l], g_kn[l],
                    sinks[l], w_out_b, rel_bias, tables)
    return _rmsnorm(x2, g_final, F32).reshape(b, s, d)
```

```python
import functools
import math

import numpy as np
import jax
import jax.numpy as jnp
from jax import lax
from jax.experimental import pallas as pl
from jax.experimental.pallas import tpu as pltpu

D_MODEL = 4096
HEAD_DIM = 128
A_HEADS = 8
A_Q_LORA = 768
A_KV_LORA = 512
A_NOPE = 128
A_ROPE = 64
A_V = 128
B_PAIRS = ((128, 1), (512, 4), (2048, 16))
B_HEADS_PER_PAIR = 4
B_HEADS = B_HEADS_PER_PAIR * len(B_PAIRS)
C_HEADS = 8
C_KV_HEADS = 2
D_HEADS = 8
D_KV_HEADS = 2
D_WINDOW = 128
GRID_W = 64
ROPE_THETA = 10000.0
N_BUCKETS = 32
REL_MAX_DIST = 1024
EPS = 1e-6
NEG = -1e30
LOG2E = math.log2(math.e)
LN2 = math.log(2.0)

W_A = A_HEADS * A_V
W_B = B_HEADS * HEAD_DIM
W_C = C_HEADS * HEAD_DIM
W_D = D_HEADS * HEAD_DIM
W_BG = B_HEADS_PER_PAIR * HEAD_DIM
MIX_WIDTH = W_A + W_B + W_C + W_D

LAT_USED = A_Q_LORA + A_KV_LORA + A_ROPE
LAT_W = 1536
BQ_OFF = LAT_W
BK_OFF = BQ_OFF + W_B
BV_OFF = BK_OFF + W_B
CQ_OFF = BV_OFF + W_B
CK_OFF = CQ_OFF + W_C
CV_OFF = CK_OFF + C_KV_HEADS * HEAD_DIM
DQ_OFF = CV_OFF + C_KV_HEADS * HEAD_DIM
DK_OFF = DQ_OFF + W_D
DV_OFF = DK_OFF + D_KV_HEADS * HEAD_DIM
GATE_OFF = DV_OFF + D_KV_HEADS * HEAD_DIM
PROJ_W = GATE_OFF + MIX_WIDTH

A_QK = 256
LSE_LANES = 32
BAND_TQ = 128

VMEM_LIMIT = 56 * 1024 * 1024

F32 = jnp.float32
BF16 = jnp.bfloat16

_NT = (((1,), (1,)), ((), ()))
_TN = (((0,), (0,)), ((), ()))


def _params(sem, vmem=VMEM_LIMIT):
    return pltpu.CompilerParams(dimension_semantics=sem, vmem_limit_bytes=vmem)


def _rmsnorm_kernel(x_ref, g_ref, o_ref):
    x = x_ref[...]
    ms = jnp.mean(x * x, axis=-1, keepdims=True)
    o_ref[...] = (x * lax.rsqrt(ms + EPS) * g_ref[...]).astype(o_ref.dtype)


def _rmsnorm(x, g, out_dtype, tm=256):
    m, d = x.shape
    return pl.pallas_call(
        _rmsnorm_kernel,
        out_shape=jax.ShapeDtypeStruct((m, d), out_dtype),
        grid=(m // tm,),
        in_specs=[pl.BlockSpec((tm, d), lambda i: (i, 0)),
                  pl.BlockSpec((1, d), lambda i: (0, 0))],
        out_specs=pl.BlockSpec((tm, d), lambda i: (i, 0)),
        compiler_params=_params(("parallel",)),
        name="rmsnorm",
    )(x, g.reshape(1, d))


def _in_proj_kernel(x_ref, w_ref, o_ref, *, factors):
    tn = w_ref.shape[0]
    col = pl.program_id(1) * tn + lax.broadcasted_iota(jnp.int32, (tn, 1), 0)
    s = jnp.ones((tn, 1), F32)
    for start, stop, factor in factors:
        s = jnp.where((col >= start) & (col < stop), factor, s)
    w = (w_ref[...] * s).astype(BF16)
    o_ref[...] = lax.dot_general(x_ref[...], w, _NT, preferred_element_type=F32).astype(o_ref.dtype)


def _in_proj(h, w_t, layer, tm=1024, tn=768):
    m, k = h.shape
    pad = LAT_W - LAT_USED
    assert LAT_W % tn == 0 and PROJ_W % tn == 0 and pad % 8 == 0
    qk = HEAD_DIM ** -0.5 * LOG2E
    factors = ((LAT_USED, LAT_W, 0.0), (BQ_OFF, BQ_OFF + W_B, qk), (DQ_OFF, DQ_OFF + W_D, qk))
    kern = functools.partial(_in_proj_kernel, factors=factors)

    def w_index(i, j):
        first = j * tn - jnp.where(j < LAT_W // tn, 0, pad)
        return layer, pl.multiple_of(first, 8), 0

    return pl.pallas_call(
        kern,
        out_shape=jax.ShapeDtypeStruct((m, PROJ_W), BF16),
        grid=(m // tm, PROJ_W // tn),
        in_specs=[pl.BlockSpec((tm, k), lambda i, j: (i, 0)),
                  pl.BlockSpec((pl.Squeezed(), pl.Element(tn), pl.Element(k)), w_index)],
        out_specs=pl.BlockSpec((tm, tn), lambda i, j: (i, j)),
        compiler_params=_params(("parallel", "parallel")),
        name="in_proj",
    )(h, w_t)


def _out_proj_kernel(*refs):
    *part_refs, w_ref, x_ref, o_ref = refs
    acc = x_ref[...]
    row = 0
    for p_ref in part_refs:
        k = p_ref.shape[1]
        acc = acc + jnp.dot(p_ref[...], w_ref[0, row:row + k, :].astype(BF16), preferred_element_type=F32)
        row += k
    o_ref[...] = acc


def _out_proj(parts, w, layer, x, tm=512, tn=512):
    m = x.shape[0]
    k, n = w.shape[1:]
    assert sum(p.shape[1] for p in parts) == k
    return pl.pallas_call(
        _out_proj_kernel,
        out_shape=jax.ShapeDtypeStruct((m, n), F32),
        grid=(n // tn, m // tm),
        in_specs=[*[pl.BlockSpec((tm, p.shape[1]), lambda j, i: (i, 0)) for p in parts],
                  pl.BlockSpec((1, k, tn), lambda j, i: (layer, 0, j)),
                  pl.BlockSpec((tm, tn), lambda j, i: (i, j))],
        out_specs=pl.BlockSpec((tm, tn), lambda j, i: (i, j)),
        compiler_params=_params(("parallel", "parallel")),
        name="out_proj",
    )(*parts, w, x)


def _rope_tile(x, cos, sin_signed):
    lane = lax.broadcasted_iota(jnp.int32, x.shape, 1)
    partner = jnp.where((lane % 64) < 32, pltpu.roll(x, 96, 1), pltpu.roll(x, 32, 1))
    return x * cos + partner * sin_signed


def _rope_tables(seq):
    inv = ROPE_THETA ** (-jnp.arange(0, 64, 2, dtype=F32) / 64)
    pos = jnp.arange(seq)

    def ang(p):
        return p.astype(F32)[:, None] * inv[None, :]

    def halves(a):
        c, s = jnp.cos(a), jnp.sin(a)
        return jnp.concatenate([c, c], -1), jnp.concatenate([-s, s], -1)

    ct, st = halves(ang(pos))
    zeros = jnp.zeros_like(ct)
    cos_a = jnp.concatenate([ct, zeros], -1)
    sin_a = jnp.concatenate([st, zeros], -1)
    cr, sr = halves(ang(pos // GRID_W))
    cc, sc = halves(ang(pos % GRID_W))
    cos_c = jnp.concatenate([cr, cc], -1)
    sin_c = jnp.concatenate([sr, sc], -1)
    return cos_a, sin_a, cos_c, sin_c


def _prep_a_kernel(lat_ref, gq_ref, gkv_ref, wuq_ref, wukv_ref, cos_ref, sin_ref, q_ref, k_ref, v_ref):
    lat = lat_ref[...].astype(F32)
    cq = lat[:, 0:A_Q_LORA]
    ckv = lat[:, A_Q_LORA:A_Q_LORA + A_KV_LORA]
    kpe = lat[:, A_Q_LORA + A_KV_LORA:A_Q_LORA + A_KV_LORA + 128]

    def rms(x, g):
        return x * lax.rsqrt(jnp.mean(x * x, axis=-1, keepdims=True) + EPS) * g

    qa = jnp.dot(rms(cq, gq_ref[...]).astype(BF16), wuq_ref[...], preferred_element_type=F32)
    kva = jnp.dot(rms(ckv, gkv_ref[...]).astype(BF16), wukv_ref[...], preferred_element_type=F32)
    cos = cos_ref[...]
    sin = sin_ref[...]
    scale = (A_NOPE + A_ROPE) ** -0.5 * LOG2E
    kpe_r = _rope_tile(kpe, cos, sin).astype(BF16)
    for h in range(A_HEADS):
        c0 = h * A_QK
        q_ref[:, c0:c0 + 128] = (qa[:, c0:c0 + 128] * scale).astype(BF16)
        q_ref[:, c0 + 128:c0 + 256] = (_rope_tile(qa[:, c0 + 128:c0 + 256], cos, sin) * scale).astype(BF16)
        k_ref[:, c0:c0 + 128] = kva[:, c0:c0 + 128].astype(BF16)
        k_ref[:, c0 + 128:c0 + 256] = kpe_r
        v_ref[:, h * A_V:(h + 1) * A_V] = kva[:, c0 + 128:c0 + 256].astype(BF16)


def _prep_a(lat, g_qa, g_kva, w_uq_p, w_ukv, cos_a, sin_a, seq, tm=512):
    m = lat.shape[0]
    nsb = seq // tm
    return pl.pallas_call(
        _prep_a_kernel,
        out_shape=(jax.ShapeDtypeStruct((m, A_HEADS * A_QK), BF16),
                   jax.ShapeDtypeStruct((m, A_HEADS * A_QK), BF16),
                   jax.ShapeDtypeStruct((m, W_A), BF16)),
        grid=(m // tm,),
        in_specs=[pl.BlockSpec((tm, LAT_W), lambda i: (i, 0)),
                  pl.BlockSpec((1, A_Q_LORA), lambda i: (0, 0)),
                  pl.BlockSpec((1, A_KV_LORA), lambda i: (0, 0)),
                  pl.BlockSpec((A_Q_LORA, A_HEADS * A_QK), lambda i: (0, 0)),
                  pl.BlockSpec((A_KV_LORA, A_HEADS * A_QK), lambda i: (0, 0)),
                  pl.BlockSpec((tm, 128), lambda i: (i % nsb, 0)),
                  pl.BlockSpec((tm, 128), lambda i: (i % nsb, 0))],
        out_specs=(pl.BlockSpec((tm, A_HEADS * A_QK), lambda i: (i, 0)),
                   pl.BlockSpec((tm, A_HEADS * A_QK), lambda i: (i, 0)),
                   pl.BlockSpec((tm, W_A), lambda i: (i, 0))),
        compiler_params=_params(("parallel",)),
        name="prep_a",
    )(lat, g_qa.reshape(1, -1), g_kva.reshape(1, -1), w_uq_p, w_ukv, cos_a, sin_a)


def _prep_c_kernel(q_in_ref, k_in_ref, gq_ref, gk_ref, cos_ref, sin_ref, q_ref, k_ref):
    cos = cos_ref[...]
    sin = sin_ref[...]

    def norm_rope(x, g):
        x = x.astype(F32)
        xn = x * lax.rsqrt(jnp.mean(x * x, axis=-1, keepdims=True) + EPS) * g
        return _rope_tile(xn, cos, sin)

    scale = HEAD_DIM ** -0.5 * LOG2E
    for h in range(q_ref.shape[1] // HEAD_DIM):
        sl = slice(h * HEAD_DIM, (h + 1) * HEAD_DIM)
        q_ref[:, sl] = (norm_rope(q_in_ref[:, sl], gq_ref[...]) * scale).astype(BF16)

    @pl.when(pl.program_id(1) == 0)
    def _():
        for h in range(C_KV_HEADS):
            sl = slice(h * HEAD_DIM, (h + 1) * HEAD_DIM)
            k_ref[:, sl] = norm_rope(k_in_ref[:, sl], gk_ref[...]).astype(BF16)


def _prep_c(proj, g_qn, g_kn, cos_c, sin_c, seq, tm=1024, qw=512):
    m = proj.shape[0]
    nsb = seq // tm
    kw = C_KV_HEADS * HEAD_DIM
    return pl.pallas_call(
        _prep_c_kernel,
        out_shape=(jax.ShapeDtypeStruct((m, W_C), BF16),
                   jax.ShapeDtypeStruct((m, kw), BF16)),
        grid=(m // tm, W_C // qw),
        in_specs=[pl.BlockSpec((tm, qw), lambda i, c: (i, CQ_OFF // qw + c)),
                  pl.BlockSpec((tm, kw), lambda i, c: (i, CK_OFF // kw)),
                  pl.BlockSpec((1, HEAD_DIM), lambda i, c: (0, 0)),
                  pl.BlockSpec((1, HEAD_DIM), lambda i, c: (0, 0)),
                  pl.BlockSpec((tm, 128), lambda i, c: (i % nsb, 0)),
                  pl.BlockSpec((tm, 128), lambda i, c: (i % nsb, 0))],
        out_specs=(pl.BlockSpec((tm, qw), lambda i, c: (i, c)),
                   pl.BlockSpec((tm, kw), lambda i, c: (i, 0))),
        compiler_params=_params(("parallel", "arbitrary")),
        name="prep_c",
    )(proj, proj, g_qn.reshape(1, -1), g_kn.reshape(1, -1), cos_c, sin_c)


def _silu(g):
    return (0.5 * g) * (1.0 + jnp.tanh(0.5 * g))


def _dense_attn_kernel(q_ref, k_ref, v_ref, g_ref, o_ref, *, groups, dq, dv, chunk, strip):
    tq = q_ref.shape[1]
    s_len = k_ref.shape[1]
    if groups == 1:
        q = q_ref[0]
    else:
        q = jnp.concatenate([q_ref[0, :, g * dq:(g + 1) * dq] for g in range(groups)], axis=0)
    n_strip = groups * tq // strip
    qs = [q[c * strip:(c + 1) * strip, :] for c in range(n_strip)]
    state = [None] * n_strip
    for j in range(s_len // chunk):
        kj = k_ref[0, j * chunk:(j + 1) * chunk, :]
        vj = v_ref[0, j * chunk:(j + 1) * chunk, :]
        for c in range(n_strip):
            st = lax.dot_general(kj, qs[c], _NT, preferred_element_type=F32)
            cmax = jnp.max(st, axis=0, keepdims=True)
            m_new = cmax if state[c] is None else jnp.maximum(state[c][0], cmax)
            p = jnp.exp2(st - m_new)
            psum = jnp.sum(p, axis=0, keepdims=True)
            pv = lax.dot_general(vj, p.astype(BF16), _TN, preferred_element_type=F32)
            if state[c] is None:
                state[c] = (m_new, psum, pv)
            else:
                m, l, acc = state[c]
                alpha = jnp.exp2(m - m_new)
                state[c] = (m_new, alpha * l + psum, alpha * acc + pv)
    o_t = jnp.concatenate([acc / l for _, l, acc in state], axis=1)
    for g in range(groups):
        cols = slice(g * dv, (g + 1) * dv)
        gate = _silu(g_ref[0, :, cols].astype(F32))
        o_ref[0, :, cols] = (o_t[:, g * tq:(g + 1) * tq].T * gate).astype(o_ref.dtype)


def _dense_attn(q, k, v, gate, *, kv_heads, groups, dq, dv, k_col0, v_col0, g_col0, tq, name, chunk=512,
                strip=None):
    b, s, _ = q.shape
    kern = functools.partial(_dense_attn_kernel, groups=groups, dq=dq, dv=dv, chunk=chunk,
                             strip=strip or groups * tq)
    return pl.pallas_call(
        kern,
        out_shape=jax.ShapeDtypeStruct((b, s, kv_heads * groups * dv), BF16),
        grid=(b, kv_heads, s // tq),
        in_specs=[pl.BlockSpec((1, tq, groups * dq), lambda bi, hi, qi: (bi, qi, hi)),
                  pl.BlockSpec((1, s, dq), lambda bi, hi, qi: (bi, 0, k_col0 + hi)),
                  pl.BlockSpec((1, s, dv), lambda bi, hi, qi: (bi, 0, v_col0 + hi)),
                  pl.BlockSpec((1, tq, groups * dv), lambda bi, hi, qi: (bi, qi, g_col0 + hi))],
        out_specs=pl.BlockSpec((1, tq, groups * dv), lambda bi, hi, qi: (bi, qi, hi)),
        compiler_params=_params(("parallel", "parallel", "parallel")),
        name=name,
    )(q, k, v, gate)


def _rel_bucket_np(rel):
    nb = N_BUCKETS // 2
    max_exact = nb // 2
    ret = np.where(rel > 0, nb, 0)
    n = np.abs(rel)
    nf = np.maximum(n, 1).astype(np.float32)
    large = max_exact + (np.log(nf / max_exact) / math.log(REL_MAX_DIST / max_exact)
                         * (nb - max_exact)).astype(np.int32)
    large = np.minimum(large, nb - 1)
    return ret + np.where(n < max_exact, n, large)


def _band_bias(tab, dil, half, tq):
    tw = tq + 2 * half
    n = tq + tw - 1
    heads = tab.shape[1]
    tiles = []
    for delta in (0, -half, -2 * half):
        rel = delta + np.arange(n) - (tq - 1)
        inside = np.abs(rel) <= half
        vals = tab[_rel_bucket_np(dil * rel)].astype(F32) * LOG2E
        diag = jnp.where(inside[:, None], vals, NEG).T
        flat = jnp.tile(diag, (1, tq + 1))[:, :tq * (n + 1)]
        tiles.append(flat.reshape(heads, tq, n + 1)[:, ::-1, :tw])
    return jnp.swapaxes(jnp.stack(tiles, 0), -1, -2)


def _banded_kernel(*refs, head_groups, n_sub, half, length, with_sink, with_gate, with_lse):
    q_ref, k_ref, v_ref, bias_ref = refs[:4]
    pos = 4
    sink_ref = gate_ref = None
    if with_sink:
        sink_ref = refs[pos]
        pos += 1
    if with_gate:
        gate_ref = refs[pos]
        pos += 1
    o_ref = refs[pos]
    lse_ref = refs[pos + 1] if with_lse else None

    tq = BAND_TQ
    tw = tq + 2 * half
    nblk = length // tq
    for cls, sb in ((c, i) for c in range(q_ref.shape[1]) for i in range(n_sub)):
        blk = pl.program_id(2) * n_sub + sb
        ks = pl.multiple_of(jnp.clip(blk * tq - half, 0, length - tw), 64)
        var = jnp.where(blk == 0, 0, jnp.where(blk == nblk - 1, 2, 1))
        rows = slice(sb * tq, (sb + 1) * tq)
        lse_rows = []
        for kvh, heads in head_groups:
            n = len(heads) * tq
            cols = slice(kvh * HEAD_DIM, (kvh + 1) * HEAD_DIM)
            kw = k_ref[0, cls, pl.ds(ks, tw), cols]
            vw = v_ref[0, cls, pl.ds(ks, tw), cols]
            qs = [q_ref[0, cls, rows, h * HEAD_DIM:(h + 1) * HEAD_DIM] for h in heads]
            q = qs[0] if len(heads) == 1 else jnp.concatenate(qs, axis=0)
            bias = [bias_ref[var, h] for h in heads]
            bias = bias[0] if len(heads) == 1 else jnp.concatenate(bias, axis=1)
            st = lax.dot_general(kw, q, _NT, preferred_element_type=F32) + bias
            m = jnp.max(st, axis=0, keepdims=True)
            if with_sink:
                sink = sink_ref[0]
                m = jnp.maximum(m, sink)
            p = jnp.exp2(st - m)
            l = jnp.sum(p, axis=0, keepdims=True)
            if with_sink:
                l = l + jnp.exp2(sink - m)
            o_t = lax.dot_general(vw, p.astype(BF16), _TN, preferred_element_type=F32) / l
            for gi, h in enumerate(heads):
                hcols = slice(h * HEAD_DIM, (h + 1) * HEAD_DIM)
                o_h = o_t[:, gi * tq:(gi + 1) * tq].T
                if with_gate:
                    o_h = o_h * _silu(gate_ref[0, cls, rows, hcols].astype(F32))
                o_ref[0, cls, rows, hcols] = o_h.astype(o_ref.dtype)
            if with_lse:
                lse = (m + jnp.log2(l)) * LN2
                for gi in range(len(heads)):
                    lse_rows.append(jnp.broadcast_to(lse[:, gi * tq:(gi + 1) * tq], (LSE_LANES, tq)))
        if with_lse:
            lse_ref[0, cls, rows, :] = jnp.concatenate(lse_rows, axis=0).T


def _deinterleave_kernel(x_ref, o_ref, scr_ref, *, dil):
    n = scr_ref.shape[1] // dil
    for c in range(scr_ref.shape[0]):
        lanes = slice(c * 128, (c + 1) * 128)
        scr_ref[c] = x_ref[0, :, lanes].astype(F32)
        for r in range(dil):
            o_ref[0, r, :, lanes] = scr_ref[c, pl.ds(r, n, stride=dil), :].astype(o_ref.dtype)


def _deinterleave_qkv(proj3, j, dil, tm=1024):
    b, s, _ = proj3.shape
    col0 = BQ_OFF // W_BG + j
    step = (BK_OFF - BQ_OFF) // W_BG
    return pl.pallas_call(
        functools.partial(_deinterleave_kernel, dil=dil),
        out_shape=jax.ShapeDtypeStruct((b, dil, s // dil, 3 * W_BG), BF16),
        grid=(b, s // tm, 3),
        in_specs=[pl.BlockSpec((1, tm, W_BG), lambda bi, i, c: (bi, i, col0 + step * c))],
        out_specs=pl.BlockSpec((1, dil, tm // dil, W_BG), lambda bi, i, c: (bi, 0, i, c)),
        scratch_shapes=[pltpu.VMEM((W_BG // 128, tm, 128), F32)],
        compiler_params=_params(("parallel", "parallel", "parallel")),
        name=f"deinterleave_{dil}",
    )(proj3)


def _dilated_group(proj3, bias, j, dil):
    b, s, w = proj3.shape
    length = s // dil
    half = B_PAIRS[j][0] // (2 * dil)
    n_sub = min(8, length // BAND_TQ)
    n_cls = min(dil, 8 // n_sub)
    tqs = n_sub * BAND_TQ
    cq, ck, cv = (BQ_OFF + j * W_BG, BK_OFF + j * W_BG, BV_OFF + j * W_BG)
    if dil == 1:
        src = proj3.reshape(b, 1, s, w)
        bq, bk, bv = cq // W_BG, ck // W_BG, cv // W_BG
    else:
        src = _deinterleave_qkv(proj3, j, dil)
        bq, bk, bv = 0, 1, 2
    groups = tuple((h, (h,)) for h in range(B_HEADS_PER_PAIR))
    kern = functools.partial(_banded_kernel, head_groups=groups, n_sub=n_sub, half=half, length=length,
                             with_sink=False, with_gate=False, with_lse=True)
    o, lse = pl.pallas_call(
        kern,
        out_shape=(jax.ShapeDtypeStruct((b, dil, length, W_BG), BF16),
                   jax.ShapeDtypeStruct((b, dil, length, 128), F32)),
        grid=(b, dil // n_cls, length // tqs),
        in_specs=[pl.BlockSpec((1, n_cls, tqs, W_BG), lambda bi, r, i: (bi, r, i, bq)),
                  pl.BlockSpec((1, n_cls, length, W_BG), lambda bi, r, i: (bi, r, 0, bk)),
                  pl.BlockSpec((1, n_cls, length, W_BG), lambda bi, r, i: (bi, r, 0, bv)),
                  pl.BlockSpec(bias.shape, lambda bi, r, i: (0, 0, 0, 0))],
        out_specs=(pl.BlockSpec((1, n_cls, tqs, W_BG), lambda bi, r, i: (bi, r, i, 0)),
                   pl.BlockSpec((1, n_cls, tqs, 128), lambda bi, r, i: (bi, r, i, 0))),
        compiler_params=_params(("parallel", "parallel", "parallel")),
        name=f"dilated_attn_{dil}",
    )(src, src, src, bias)
    return o, lse


def _window_attn(proj3, bias, sink_rows, n_sub=8):
    b, s, w = proj3.shape
    g = D_HEADS // D_KV_HEADS
    gw = g * HEAD_DIM
    tqs = n_sub * BAND_TQ
    kern = functools.partial(_banded_kernel, head_groups=((0, tuple(range(g))),), n_sub=n_sub, half=D_WINDOW,
                             length=s, with_sink=True, with_gate=True, with_lse=False)
    gate0 = (GATE_OFF + W_A + W_B + W_C) // gw
    src = proj3.reshape(b, 1, s, w)
    o = pl.pallas_call(
        kern,
        out_shape=jax.ShapeDtypeStruct((b, 1, s, W_D), BF16),
        grid=(b, D_KV_HEADS, s // tqs),
        in_specs=[pl.BlockSpec((1, 1, tqs, gw), lambda bi, hi, i: (bi, 0, i, DQ_OFF // gw + hi)),
                  pl.BlockSpec((1, 1, s, HEAD_DIM), lambda bi, hi, i: (bi, 0, 0, DK_OFF // HEAD_DIM + hi)),
                  pl.BlockSpec((1, 1, s, HEAD_DIM), lambda bi, hi, i: (bi, 0, 0, DV_OFF // HEAD_DIM + hi)),
                  pl.BlockSpec((3, g) + bias.shape[2:], lambda bi, hi, i: (0, hi, 0, 0)),
                  pl.BlockSpec((1, 1, g * BAND_TQ), lambda bi, hi, i: (hi, 0, 0)),
                  pl.BlockSpec((1, 1, tqs, gw), lambda bi, hi, i: (bi, 0, i, gate0 + hi))],
        out_specs=pl.BlockSpec((1, 1, tqs, gw), lambda bi, hi, i: (bi, 0, i, hi)),
        compiler_params=_params(("parallel", "parallel", "parallel")),
        name="window_attn",
    )(src, src, src, bias, sink_rows, src)
    return o.reshape(b, s, W_D)


def _mix_b_kernel(g0_ref, g1_ref, g2_ref, ob0_ref, ob1_ref, ob2_ref, l0_ref, l1_ref, l2_ref, out_ref,
                  ob_scr, lse_scr):
    gate_refs = (g0_ref, g1_ref, g2_ref)

    def interleave(src_ref, scr_ref):
        dil, n, w = src_ref.shape[1:]
        tiles = []
        for c in range(w // 128):
            lanes = slice(c * 128, (c + 1) * 128)
            if dil == 1:
                tiles.append(src_ref[0, 0, :, lanes].astype(F32))
                continue
            for r in range(dil):
                scr_ref[c, pl.ds(r, n, stride=dil), :] = src_ref[0, r, :, lanes].astype(F32)
            tiles.append(scr_ref[c])
        return tiles

    l0, l1, l2 = (interleave(l_ref, lse_scr)[0] for l_ref in (l0_ref, l1_ref, l2_ref))
    mx = jnp.maximum(jnp.maximum(l0, l1), l2)
    e0, e1, e2 = jnp.exp(l0 - mx), jnp.exp(l1 - mx), jnp.exp(l2 - mx)
    inv = 1.0 / (e0 + e1 + e2)
    for j, (ob_ref, e) in enumerate(((ob0_ref, e0), (ob1_ref, e1), (ob2_ref, e2))):
        alpha = e * inv
        ob = interleave(ob_ref, ob_scr)
        for h in range(B_HEADS_PER_PAIR):
            hcols = slice(h * HEAD_DIM, (h + 1) * HEAD_DIM)
            a_h = alpha[:, h * LSE_LANES:h * LSE_LANES + 1]
            gate = _silu(gate_refs[j][:, hcols].astype(F32))
            out_ref[:, j * W_BG + h * HEAD_DIM:j * W_BG + (h + 1) * HEAD_DIM] = (ob[h] * a_h * gate).astype(BF16)


def _mix_b(proj, o_b, lse_b, seq, tm=512):
    m = proj.shape[0]
    nsb = seq // tm
    gate = lambda c: pl.BlockSpec((tm, W_BG), lambda i: (i, (GATE_OFF + W_A) // W_BG + c))

    def classes(a):
        dil, w = a.shape[1], a.shape[3]
        return pl.BlockSpec((1, dil, tm // dil, w), lambda i: (i // nsb, 0, i % nsb, 0))

    return pl.pallas_call(
        _mix_b_kernel,
        out_shape=jax.ShapeDtypeStruct((m, W_B), BF16),
        grid=(m // tm,),
        in_specs=[gate(0), gate(1), gate(2), *[classes(a) for a in o_b], *[classes(a) for a in lse_b]],
        out_specs=pl.BlockSpec((tm, W_B), lambda i: (i, 0)),
        scratch_shapes=[pltpu.VMEM((W_BG // 128, tm, 128), F32), pltpu.VMEM((1, tm, 128), F32)],
        compiler_params=_params(("parallel",)),
        name="mix_b",
    )(proj, proj, proj, *o_b, *lse_b)


def _layout_w_uq(w):
    w = w.reshape(A_Q_LORA, A_HEADS, A_NOPE + A_ROPE)
    w = jnp.pad(w, ((0, 0), (0, 0), (0, A_QK - A_NOPE - A_ROPE)))
    return w.reshape(A_Q_LORA, A_HEADS * A_QK).astype(BF16)


def _layer(x2, b, s, layer, g_attn, w_in_t, g_qa, g_kva, w_uq, w_ukv, g_qn, g_kn, sinks, w_out, rel_bias, tables):
    cos_a, sin_a, cos_c, sin_c = tables
    h = _rmsnorm(x2, g_attn, BF16)
    proj = _in_proj(h, w_in_t, layer)
    proj3 = proj.reshape(b, s, PROJ_W)

    q_a, k_a, v_a = _prep_a(proj, g_qa, g_kva, _layout_w_uq(w_uq), w_ukv.astype(BF16), cos_a, sin_a, s)
    o_a = _dense_attn(q_a.reshape(b, s, -1), k_a.reshape(b, s, -1), v_a.reshape(b, s, -1), proj3,
                      kv_heads=A_HEADS, groups=1, dq=A_QK, dv=A_V, k_col0=0, v_col0=0,
                      g_col0=GATE_OFF // A_V, tq=2048, name="attn_a")

    o_b, lse_b = [], []
    for j, (win, dil) in enumerate(B_PAIRS):
        tab = rel_bias[:, j * B_HEADS_PER_PAIR:(j + 1) * B_HEADS_PER_PAIR]
        bias = _band_bias(tab, dil, win // (2 * dil), BAND_TQ)
        o, lse = _dilated_group(proj3, bias, j, dil)
        o_b.append(o)
        lse_b.append(lse)

    q_c, k_c = _prep_c(proj, g_qn, g_kn, cos_c, sin_c, s)
    gc = C_HEADS // C_KV_HEADS
    o_c = _dense_attn(q_c.reshape(b, s, -1), k_c.reshape(b, s, -1), proj3, proj3,
                      kv_heads=C_KV_HEADS, groups=gc, dq=HEAD_DIM, dv=HEAD_DIM,
                      k_col0=0, v_col0=CV_OFF // HEAD_DIM, g_col0=(GATE_OFF + W_A + W_B) // (gc * HEAD_DIM),
                      tq=512, name="attn_c")

    g = D_HEADS // D_KV_HEADS
    bias_d = _band_bias(rel_bias[:, B_HEADS:], 1, D_WINDOW, BAND_TQ)
    sink_rows = jnp.repeat(sinks.astype(F32) * LOG2E, BAND_TQ).reshape(D_KV_HEADS, 1, g * BAND_TQ)
    o_d = _window_attn(proj3, bias_d, sink_rows)

    parts = (o_a.reshape(b * s, W_A), _mix_b(proj, o_b, lse_b, s), o_c.reshape(b * s, W_C), o_d.reshape(b * s, W_D))
    return _out_proj(parts, w_out, layer, x2)


def kernel(x, g_attn, w_in, g_qa, g_kva, w_uq, w_ukv, g_qn, g_kn, sinks, w_out, rel_bias, g_final):
    b, s, d = x.shape
    depth = w_in.shape[0]
    tables = _rope_tables(s)
    x2 = x.reshape(b * s, d)
    w_in_t = jnp.swapaxes(w_in, 1, 2)
    for l in range(depth):
        x2 = _layer(x2, b, s, l, g_attn[l], w_in_t, g_qa[l], g_kva[l], w_uq[l], w_ukv[l], g_qn[l], g_kn[l],
                    sinks[l], w_out, rel_bias, tables)
    return _rmsnorm(x2, g_final, F32).reshape(b, s, d)
```

```python
import functools
import math

import numpy as np
import jax
import jax.numpy as jnp
from jax import lax
from jax.experimental import pallas as pl
from jax.experimental.pallas import tpu as pltpu

D_MODEL = 4096
HEAD_DIM = 128
A_HEADS = 8
A_Q_LORA = 768
A_KV_LORA = 512
A_NOPE = 128
A_ROPE = 64
A_V = 128
B_PAIRS = ((128, 1), (512, 4), (2048, 16))
B_HEADS_PER_PAIR = 4
B_HEADS = B_HEADS_PER_PAIR * len(B_PAIRS)
C_HEADS = 8
C_KV_HEADS = 2
D_HEADS = 8
D_KV_HEADS = 2
D_WINDOW = 128
GRID_W = 64
ROPE_THETA = 10000.0
N_BUCKETS = 32
REL_MAX_DIST = 1024
EPS = 1e-6
NEG = -1e30
LOG2E = math.log2(math.e)
LN2 = math.log(2.0)

W_A = A_HEADS * A_V
W_B = B_HEADS * HEAD_DIM
W_C = C_HEADS * HEAD_DIM
W_D = D_HEADS * HEAD_DIM
W_BG = B_HEADS_PER_PAIR * HEAD_DIM
MIX_WIDTH = W_A + W_B + W_C + W_D

LAT_USED = A_Q_LORA + A_KV_LORA + A_ROPE
LAT_W = 1536
BQ_OFF = LAT_W
BK_OFF = BQ_OFF + W_B
BV_OFF = BK_OFF + W_B
CQ_OFF = BV_OFF + W_B
CK_OFF = CQ_OFF + W_C
CV_OFF = CK_OFF + C_KV_HEADS * HEAD_DIM
DQ_OFF = CV_OFF + C_KV_HEADS * HEAD_DIM
DK_OFF = DQ_OFF + W_D
DV_OFF = DK_OFF + D_KV_HEADS * HEAD_DIM
GATE_OFF = DV_OFF + D_KV_HEADS * HEAD_DIM
PROJ_W = GATE_OFF + MIX_WIDTH

A_QK = 256
LSE_LANES = 32
BAND_TQ = 128

VMEM_LIMIT = 56 * 1024 * 1024

F32 = jnp.float32
BF16 = jnp.bfloat16

_NT = (((1,), (1,)), ((), ()))
_TN = (((0,), (0,)), ((), ()))


def _params(sem, vmem=VMEM_LIMIT):
    return pltpu.CompilerParams(dimension_semantics=sem, vmem_limit_bytes=vmem)


def _rmsnorm_kernel(x_ref, g_ref, o_ref):
    x = x_ref[...]
    ms = jnp.mean(x * x, axis=-1, keepdims=True)
    o_ref[...] = (x * lax.rsqrt(ms + EPS) * g_ref[...]).astype(o_ref.dtype)


def _rmsnorm(x, g, out_dtype, tm=256):
    m, d = x.shape
    return pl.pallas_call(
        _rmsnorm_kernel,
        out_shape=jax.ShapeDtypeStruct((m, d), out_dtype),
        grid=(m // tm,),
        in_specs=[pl.BlockSpec((tm, d), lambda i: (i, 0)),
                  pl.BlockSpec((1, d), lambda i: (0, 0))],
        out_specs=pl.BlockSpec((tm, d), lambda i: (i, 0)),
        compiler_params=_params(("parallel",)),
        name="rmsnorm",
    )(x, g.reshape(1, d))


def _in_proj_kernel(x_ref, w_ref, o_ref, *, factors):
    tn = w_ref.shape[0]
    col = pl.program_id(1) * tn + lax.broadcasted_iota(jnp.int32, (tn, 1), 0)
    s = jnp.ones((tn, 1), F32)
    for start, stop, factor in factors:
        s = jnp.where((col >= start) & (col < stop), factor, s)
    w = (w_ref[...] * s).astype(BF16)
    o_ref[...] = lax.dot_general(x_ref[...], w, _NT, preferred_element_type=F32).astype(o_ref.dtype)


def _in_proj(h, w_t, layer, tm=1024, tn=768):
    m, k = h.shape
    pad = LAT_W - LAT_USED
    assert LAT_W % tn == 0 and PROJ_W % tn == 0 and pad % 8 == 0
    qk = HEAD_DIM ** -0.5 * LOG2E
    factors = ((LAT_USED, LAT_W, 0.0), (BQ_OFF, BQ_OFF + W_B, qk), (DQ_OFF, DQ_OFF + W_D, qk))
    kern = functools.partial(_in_proj_kernel, factors=factors)

    def w_index(i, j):
        first = j * tn - jnp.where(j < LAT_W // tn, 0, pad)
        return layer, pl.multiple_of(first, 8), 0

    return pl.pallas_call(
        kern,
        out_shape=jax.ShapeDtypeStruct((m, PROJ_W), BF16),
        grid=(m // tm, PROJ_W // tn),
        in_specs=[pl.BlockSpec((tm, k), lambda i, j: (i, 0)),
                  pl.BlockSpec((pl.Squeezed(), pl.Element(tn), pl.Element(k)), w_index)],
        out_specs=pl.BlockSpec((tm, tn), lambda i, j: (i, j)),
        compiler_params=_params(("parallel", "parallel")),
        name="in_proj",
    )(h, w_t)


def _out_proj_kernel(*refs):
    *part_refs, w_ref, x_ref, o_ref = refs
    acc = x_ref[...]
    row = 0
    for p_ref in part_refs:
        k = p_ref.shape[1]
        acc = acc + jnp.dot(p_ref[...], w_ref[0, row:row + k, :], preferred_element_type=F32)
        row += k
    o_ref[...] = acc


def _out_proj(parts, w, layer, x, tm=512, tn=1024):
    m = x.shape[0]
    k, n = w.shape[1:]
    assert sum(p.shape[1] for p in parts) == k
    return pl.pallas_call(
        _out_proj_kernel,
        out_shape=jax.ShapeDtypeStruct((m, n), F32),
        grid=(n // tn, m // tm),
        in_specs=[*[pl.BlockSpec((tm, p.shape[1]), lambda j, i: (i, 0)) for p in parts],
                  pl.BlockSpec((1, k, tn), lambda j, i: (layer, 0, j)),
                  pl.BlockSpec((tm, tn), lambda j, i: (i, j))],
        out_specs=pl.BlockSpec((tm, tn), lambda j, i: (i, j)),
        compiler_params=_params(("parallel", "parallel")),
        name="out_proj",
    )(*parts, w, x)


def _rope_tile(x, cos, sin_signed):
    lane = lax.broadcasted_iota(jnp.int32, x.shape, 1)
    partner = jnp.where((lane % 64) < 32, pltpu.roll(x, 96, 1), pltpu.roll(x, 32, 1))
    return x * cos + partner * sin_signed


def _rope_tables(seq):
    inv = ROPE_THETA ** (-jnp.arange(0, 64, 2, dtype=F32) / 64)
    pos = jnp.arange(seq)

    def ang(p):
        return p.astype(F32)[:, None] * inv[None, :]

    def halves(a):
        c, s = jnp.cos(a), jnp.sin(a)
        return jnp.concatenate([c, c], -1), jnp.concatenate([-s, s], -1)

    ct, st = halves(ang(pos))
    zeros = jnp.zeros_like(ct)
    cos_a = jnp.concatenate([ct, zeros], -1)
    sin_a = jnp.concatenate([st, zeros], -1)
    cr, sr = halves(ang(pos // GRID_W))
    cc, sc = halves(ang(pos % GRID_W))
    cos_c = jnp.concatenate([cr, cc], -1)
    sin_c = jnp.concatenate([sr, sc], -1)
    return cos_a, sin_a, cos_c, sin_c


def _prep_a_kernel(lat_ref, gq_ref, gkv_ref, wuq_ref, wukv_ref, cos_ref, sin_ref, q_ref, k_ref, v_ref):
    lat = lat_ref[...].astype(F32)
    cq = lat[:, 0:A_Q_LORA]
    ckv = lat[:, A_Q_LORA:A_Q_LORA + A_KV_LORA]
    kpe = lat[:, A_Q_LORA + A_KV_LORA:A_Q_LORA + A_KV_LORA + 128]

    def rms(x, g):
        return x * lax.rsqrt(jnp.mean(x * x, axis=-1, keepdims=True) + EPS) * g

    qa = jnp.dot(rms(cq, gq_ref[...]).astype(BF16), wuq_ref[...], preferred_element_type=F32)
    kva = jnp.dot(rms(ckv, gkv_ref[...]).astype(BF16), wukv_ref[...], preferred_element_type=F32)
    cos = cos_ref[...]
    sin = sin_ref[...]
    scale = (A_NOPE + A_ROPE) ** -0.5 * LOG2E
    kpe_r = _rope_tile(kpe, cos, sin).astype(BF16)
    for h in range(A_HEADS):
        c0 = h * A_QK
        q_ref[:, c0:c0 + 128] = (qa[:, c0:c0 + 128] * scale).astype(BF16)
        q_ref[:, c0 + 128:c0 + 256] = (_rope_tile(qa[:, c0 + 128:c0 + 256], cos, sin) * scale).astype(BF16)
        k_ref[:, c0:c0 + 128] = kva[:, c0:c0 + 128].astype(BF16)
        k_ref[:, c0 + 128:c0 + 256] = kpe_r
        v_ref[:, h * A_V:(h + 1) * A_V] = kva[:, c0 + 128:c0 + 256].astype(BF16)


def _prep_a(lat, g_qa, g_kva, w_uq_p, w_ukv, cos_a, sin_a, seq, tm=512):
    m = lat.shape[0]
    nsb = seq // tm
    return pl.pallas_call(
        _prep_a_kernel,
        out_shape=(jax.ShapeDtypeStruct((m, A_HEADS * A_QK), BF16),
                   jax.ShapeDtypeStruct((m, A_HEADS * A_QK), BF16),
                   jax.ShapeDtypeStruct((m, W_A), BF16)),
        grid=(m // tm,),
        in_specs=[pl.BlockSpec((tm, LAT_W), lambda i: (i, 0)),
                  pl.BlockSpec((1, A_Q_LORA), lambda i: (0, 0)),
                  pl.BlockSpec((1, A_KV_LORA), lambda i: (0, 0)),
                  pl.BlockSpec((A_Q_LORA, A_HEADS * A_QK), lambda i: (0, 0)),
                  pl.BlockSpec((A_KV_LORA, A_HEADS * A_QK), lambda i: (0, 0)),
                  pl.BlockSpec((tm, 128), lambda i: (i % nsb, 0)),
                  pl.BlockSpec((tm, 128), lambda i: (i % nsb, 0))],
        out_specs=(pl.BlockSpec((tm, A_HEADS * A_QK), lambda i: (i, 0)),
                   pl.BlockSpec((tm, A_HEADS * A_QK), lambda i: (i, 0)),
                   pl.BlockSpec((tm, W_A), lambda i: (i, 0))),
        compiler_params=_params(("parallel",)),
        name="prep_a",
    )(lat, g_qa.reshape(1, -1), g_kva.reshape(1, -1), w_uq_p, w_ukv, cos_a, sin_a)


def _prep_c_kernel(q_in_ref, k_in_ref, gq_ref, gk_ref, cos_ref, sin_ref, q_ref, k_ref):
    cos = cos_ref[...]
    sin = sin_ref[...]

    def norm_rope(x, g):
        x = x.astype(F32)
        xn = x * lax.rsqrt(jnp.mean(x * x, axis=-1, keepdims=True) + EPS) * g
        return _rope_tile(xn, cos, sin)

    scale = HEAD_DIM ** -0.5 * LOG2E
    for h in range(q_ref.shape[1] // HEAD_DIM):
        sl = slice(h * HEAD_DIM, (h + 1) * HEAD_DIM)
        q_ref[:, sl] = (norm_rope(q_in_ref[:, sl], gq_ref[...]) * scale).astype(BF16)

    @pl.when(pl.program_id(1) == 0)
    def _():
        for h in range(C_KV_HEADS):
            sl = slice(h * HEAD_DIM, (h + 1) * HEAD_DIM)
            k_ref[:, sl] = norm_rope(k_in_ref[:, sl], gk_ref[...]).astype(BF16)


def _prep_c(proj, g_qn, g_kn, cos_c, sin_c, seq, tm=1024, qw=512):
    m = proj.shape[0]
    nsb = seq // tm
    kw = C_KV_HEADS * HEAD_DIM
    return pl.pallas_call(
        _prep_c_kernel,
        out_shape=(jax.ShapeDtypeStruct((m, W_C), BF16),
                   jax.ShapeDtypeStruct((m, kw), BF16)),
        grid=(m // tm, W_C // qw),
        in_specs=[pl.BlockSpec((tm, qw), lambda i, c: (i, CQ_OFF // qw + c)),
                  pl.BlockSpec((tm, kw), lambda i, c: (i, CK_OFF // kw)),
                  pl.BlockSpec((1, HEAD_DIM), lambda i, c: (0, 0)),
                  pl.BlockSpec((1, HEAD_DIM), lambda i, c: (0, 0)),
                  pl.BlockSpec((tm, 128), lambda i, c: (i % nsb, 0)),
                  pl.BlockSpec((tm, 128), lambda i, c: (i % nsb, 0))],
        out_specs=(pl.BlockSpec((tm, qw), lambda i, c: (i, c)),
                   pl.BlockSpec((tm, kw), lambda i, c: (i, 0))),
        compiler_params=_params(("parallel", "arbitrary")),
        name="prep_c",
    )(proj, proj, g_qn.reshape(1, -1), g_kn.reshape(1, -1), cos_c, sin_c)


def _silu(g):
    return (0.5 * g) * (1.0 + jnp.tanh(0.5 * g))


def _dense_attn_kernel(q_ref, k_ref, v_ref, g_ref, o_ref, *, groups, dq, dv, chunk, strip):
    tq = q_ref.shape[1]
    s_len = k_ref.shape[1]
    if groups == 1:
        q = q_ref[0]
    else:
        q = jnp.concatenate([q_ref[0, :, g * dq:(g + 1) * dq] for g in range(groups)], axis=0)
    n_strip = groups * tq // strip
    qs = [q[c * strip:(c + 1) * strip, :] for c in range(n_strip)]
    state = [None] * n_strip
    for j in range(s_len // chunk):
        kj = k_ref[0, j * chunk:(j + 1) * chunk, :]
        vj = v_ref[0, j * chunk:(j + 1) * chunk, :]
        for c in range(n_strip):
            st = lax.dot_general(kj, qs[c], _NT, preferred_element_type=F32)
            cmax = jnp.max(st, axis=0, keepdims=True)
            m_new = cmax if state[c] is None else jnp.maximum(state[c][0], cmax)
            p = jnp.exp2(st - m_new)
            psum = jnp.sum(p, axis=0, keepdims=True)
            pv = lax.dot_general(vj, p.astype(BF16), _TN, preferred_element_type=F32)
            if state[c] is None:
                state[c] = (m_new, psum, pv)
            else:
                m, l, acc = state[c]
                alpha = jnp.exp2(m - m_new)
                state[c] = (m_new, alpha * l + psum, alpha * acc + pv)
    o_t = jnp.concatenate([acc / l for _, l, acc in state], axis=1)
    for g in range(groups):
        cols = slice(g * dv, (g + 1) * dv)
        gate = _silu(g_ref[0, :, cols].astype(F32))
        o_ref[0, :, cols] = (o_t[:, g * tq:(g + 1) * tq].T * gate).astype(o_ref.dtype)


def _dense_attn(q, k, v, gate, *, kv_heads, groups, dq, dv, k_col0, v_col0, g_col0, tq, name, chunk=512,
                strip=None):
    b, s, _ = q.shape
    kern = functools.partial(_dense_attn_kernel, groups=groups, dq=dq, dv=dv, chunk=chunk,
                             strip=strip or groups * tq)
    return pl.pallas_call(
        kern,
        out_shape=jax.ShapeDtypeStruct((b, s, kv_heads * groups * dv), BF16),
        grid=(b, kv_heads, s // tq),
        in_specs=[pl.BlockSpec((1, tq, groups * dq), lambda bi, hi, qi: (bi, qi, hi)),
                  pl.BlockSpec((1, s, dq), lambda bi, hi, qi: (bi, 0, k_col0 + hi)),
                  pl.BlockSpec((1, s, dv), lambda bi, hi, qi: (bi, 0, v_col0 + hi)),
                  pl.BlockSpec((1, tq, groups * dv), lambda bi, hi, qi: (bi, qi, g_col0 + hi))],
        out_specs=pl.BlockSpec((1, tq, groups * dv), lambda bi, hi, qi: (bi, qi, hi)),
        compiler_params=_params(("parallel", "parallel", "parallel")),
        name=name,
    )(q, k, v, gate)


def _rel_bucket_np(rel):
    nb = N_BUCKETS // 2
    max_exact = nb // 2
    ret = np.where(rel > 0, nb, 0)
    n = np.abs(rel)
    nf = np.maximum(n, 1).astype(np.float32)
    large = max_exact + (np.log(nf / max_exact) / math.log(REL_MAX_DIST / max_exact)
                         * (nb - max_exact)).astype(np.int32)
    large = np.minimum(large, nb - 1)
    return ret + np.where(n < max_exact, n, large)


def _band_bias(tab, dil, half, tq):
    tw = tq + 2 * half
    n = tq + tw - 1
    heads = tab.shape[1]
    tiles = []
    for delta in (0, -half, -2 * half):
        rel = delta + np.arange(n) - (tq - 1)
        inside = np.abs(rel) <= half
        vals = tab[_rel_bucket_np(dil * rel)].astype(F32) * LOG2E
        diag = jnp.where(inside[:, None], vals, NEG).T
        flat = jnp.tile(diag, (1, tq + 1))[:, :tq * (n + 1)]
        tiles.append(flat.reshape(heads, tq, n + 1)[:, ::-1, :tw])
    return jnp.swapaxes(jnp.stack(tiles, 0), -1, -2)


def _banded_kernel(*refs, head_groups, n_sub, half, length, with_sink, with_gate, with_lse):
    q_ref, k_ref, v_ref, bias_ref = refs[:4]
    pos = 4
    sink_ref = gate_ref = None
    if with_sink:
        sink_ref = refs[pos]
        pos += 1
    if with_gate:
        gate_ref = refs[pos]
        pos += 1
    o_ref = refs[pos]
    lse_ref = refs[pos + 1] if with_lse else None

    tq = BAND_TQ
    tw = tq + 2 * half
    nblk = length // tq
    for cls, sb in ((c, i) for c in range(q_ref.shape[1]) for i in range(n_sub)):
        blk = pl.program_id(2) * n_sub + sb
        ks = pl.multiple_of(jnp.clip(blk * tq - half, 0, length - tw), 64)
        var = jnp.where(blk == 0, 0, jnp.where(blk == nblk - 1, 2, 1))
        rows = slice(sb * tq, (sb + 1) * tq)
        lse_rows = []
        for kvh, heads in head_groups:
            n = len(heads) * tq
            cols = slice(kvh * HEAD_DIM, (kvh + 1) * HEAD_DIM)
            kw = k_ref[0, cls, pl.ds(ks, tw), cols]
            vw = v_ref[0, cls, pl.ds(ks, tw), cols]
            qs = [q_ref[0, cls, rows, h * HEAD_DIM:(h + 1) * HEAD_DIM] for h in heads]
            q = qs[0] if len(heads) == 1 else jnp.concatenate(qs, axis=0)
            bias = [bias_ref[var, h] for h in heads]
            bias = bias[0] if len(heads) == 1 else jnp.concatenate(bias, axis=1)
            st = lax.dot_general(kw, q, _NT, preferred_element_type=F32) + bias
            m = jnp.max(st, axis=0, keepdims=True)
            if with_sink:
                sink = sink_ref[0]
                m = jnp.maximum(m, sink)
            p = jnp.exp2(st - m)
            l = jnp.sum(p, axis=0, keepdims=True)
            if with_sink:
                l = l + jnp.exp2(sink - m)
            o_t = lax.dot_general(vw, p.astype(BF16), _TN, preferred_element_type=F32) / l
            for gi, h in enumerate(heads):
                hcols = slice(h * HEAD_DIM, (h + 1) * HEAD_DIM)
                o_h = o_t[:, gi * tq:(gi + 1) * tq].T
                if with_gate:
                    o_h = o_h * _silu(gate_ref[0, cls, rows, hcols].astype(F32))
                o_ref[0, cls, rows, hcols] = o_h.astype(o_ref.dtype)
            if with_lse:
                lse = (m + jnp.log2(l)) * LN2
                for gi in range(len(heads)):
                    lse_rows.append(jnp.broadcast_to(lse[:, gi * tq:(gi + 1) * tq], (LSE_LANES, tq)))
        if with_lse:
            lse_ref[0, cls, rows, :] = jnp.concatenate(lse_rows, axis=0).T


def _deinterleave_kernel(q_ref, k_ref, v_ref, o_ref, scr_ref, *, dil):
    n = scr_ref.shape[1] // dil
    tiles = q_ref.shape[2] // 128
    for part, x_ref in enumerate((q_ref, k_ref, v_ref)):
        for c in range(tiles):
            plane = part * tiles + c
            scr_ref[plane] = x_ref[0, :, c * 128:(c + 1) * 128].astype(F32)
            for r in range(dil):
                o_ref[0, r, :, plane * 128:(plane + 1) * 128] = (
                    scr_ref[plane, pl.ds(r, n, stride=dil), :].astype(o_ref.dtype))


def _deinterleave_qkv(proj3, j, dil, tm=1024):
    b, s, _ = proj3.shape
    col0 = BQ_OFF // W_BG + j
    step = (BK_OFF - BQ_OFF) // W_BG
    return pl.pallas_call(
        functools.partial(_deinterleave_kernel, dil=dil),
        out_shape=jax.ShapeDtypeStruct((b, dil, s // dil, 3 * W_BG), BF16),
        grid=(b, s // tm),
        in_specs=[pl.BlockSpec((1, tm, W_BG), lambda bi, i, c=c: (bi, i, col0 + step * c)) for c in range(3)],
        out_specs=pl.BlockSpec((1, dil, tm // dil, 3 * W_BG), lambda bi, i: (bi, 0, i, 0)),
        scratch_shapes=[pltpu.VMEM((3 * W_BG // 128, tm, 128), F32)],
        compiler_params=_params(("parallel", "parallel")),
        name=f"deinterleave_{dil}",
    )(proj3, proj3, proj3)


def _dilated_group(proj3, bias, j, dil):
    b, s, w = proj3.shape
    length = s // dil
    half = B_PAIRS[j][0] // (2 * dil)
    n_sub = min(8, length // BAND_TQ)
    n_cls = min(dil, 8 // n_sub)
    tqs = n_sub * BAND_TQ
    cq, ck, cv = (BQ_OFF + j * W_BG, BK_OFF + j * W_BG, BV_OFF + j * W_BG)
    if dil == 1:
        src = proj3.reshape(b, 1, s, w)
        bq, bk, bv = cq // W_BG, ck // W_BG, cv // W_BG
    else:
        src = _deinterleave_qkv(proj3, j, dil)
        bq, bk, bv = 0, 1, 2
    groups = tuple((h, (h,)) for h in range(B_HEADS_PER_PAIR))
    kern = functools.partial(_banded_kernel, head_groups=groups, n_sub=n_sub, half=half, length=length,
                             with_sink=False, with_gate=False, with_lse=True)
    o, lse = pl.pallas_call(
        kern,
        out_shape=(jax.ShapeDtypeStruct((b, dil, length, W_BG), BF16),
                   jax.ShapeDtypeStruct((b, dil, length, 128), F32)),
        grid=(b, dil // n_cls, length // tqs),
        in_specs=[pl.BlockSpec((1, n_cls, tqs, W_BG), lambda bi, r, i: (bi, r, i, bq)),
                  pl.BlockSpec((1, n_cls, length, W_BG), lambda bi, r, i: (bi, r, 0, bk)),
                  pl.BlockSpec((1, n_cls, length, W_BG), lambda bi, r, i: (bi, r, 0, bv)),
                  pl.BlockSpec(bias.shape, lambda bi, r, i: (0, 0, 0, 0))],
        out_specs=(pl.BlockSpec((1, n_cls, tqs, W_BG), lambda bi, r, i: (bi, r, i, 0)),
                   pl.BlockSpec((1, n_cls, tqs, 128), lambda bi, r, i: (bi, r, i, 0))),
        compiler_params=_params(("parallel", "parallel", "parallel")),
        name=f"dilated_attn_{dil}",
    )(src, src, src, bias)
    return o, lse


def _window_attn(proj3, bias, sink_rows, n_sub=8):
    b, s, w = proj3.shape
    g = D_HEADS // D_KV_HEADS
    gw = g * HEAD_DIM
    tqs = n_sub * BAND_TQ
    kern = functools.partial(_banded_kernel, head_groups=((0, tuple(range(g))),), n_sub=n_sub, half=D_WINDOW,
                             length=s, with_sink=True, with_gate=True, with_lse=False)
    gate0 = (GATE_OFF + W_A + W_B + W_C) // gw
    src = proj3.reshape(b, 1, s, w)
    o = pl.pallas_call(
        kern,
        out_shape=jax.ShapeDtypeStruct((b, 1, s, W_D), BF16),
        grid=(b, D_KV_HEADS, s // tqs),
        in_specs=[pl.BlockSpec((1, 1, tqs, gw), lambda bi, hi, i: (bi, 0, i, DQ_OFF // gw + hi)),
                  pl.BlockSpec((1, 1, s, HEAD_DIM), lambda bi, hi, i: (bi, 0, 0, DK_OFF // HEAD_DIM + hi)),
                  pl.BlockSpec((1, 1, s, HEAD_DIM), lambda bi, hi, i: (bi, 0, 0, DV_OFF // HEAD_DIM + hi)),
                  pl.BlockSpec((3, g) + bias.shape[2:], lambda bi, hi, i: (0, hi, 0, 0)),
                  pl.BlockSpec((1, 1, g * BAND_TQ), lambda bi, hi, i: (hi, 0, 0)),
                  pl.BlockSpec((1, 1, tqs, gw), lambda bi, hi, i: (bi, 0, i, gate0 + hi))],
        out_specs=pl.BlockSpec((1, 1, tqs, gw), lambda bi, hi, i: (bi, 0, i, hi)),
        compiler_params=_params(("parallel", "parallel", "parallel")),
        name="window_attn",
    )(src, src, src, bias, sink_rows, src)
    return o.reshape(b, s, W_D)


def _mix_b_kernel(g0_ref, g1_ref, g2_ref, ob0_ref, ob1_ref, ob2_ref, l0_ref, l1_ref, l2_ref, out_ref,
                  ob_scr, lse_scr):
    gate_refs = (g0_ref, g1_ref, g2_ref)

    def interleave(src_ref, scr_ref):
        dil, n, w = src_ref.shape[1:]
        tiles = []
        for c in range(w // 128):
            lanes = slice(c * 128, (c + 1) * 128)
            if dil == 1:
                tiles.append(src_ref[0, 0, :, lanes].astype(F32))
                continue
            for r in range(dil):
                scr_ref[c, pl.ds(r, n, stride=dil), :] = src_ref[0, r, :, lanes].astype(F32)
            tiles.append(scr_ref[c])
        return tiles

    l0, l1, l2 = (interleave(l_ref, lse_scr)[0] for l_ref in (l0_ref, l1_ref, l2_ref))
    mx = jnp.maximum(jnp.maximum(l0, l1), l2)
    e0, e1, e2 = jnp.exp(l0 - mx), jnp.exp(l1 - mx), jnp.exp(l2 - mx)
    inv = 1.0 / (e0 + e1 + e2)
    for j, (ob_ref, e) in enumerate(((ob0_ref, e0), (ob1_ref, e1), (ob2_ref, e2))):
        alpha = e * inv
        ob = interleave(ob_ref, ob_scr)
        for h in range(B_HEADS_PER_PAIR):
            hcols = slice(h * HEAD_DIM, (h + 1) * HEAD_DIM)
            a_h = alpha[:, h * LSE_LANES:h * LSE_LANES + 1]
            gate = _silu(gate_refs[j][:, hcols].astype(F32))
            out_ref[:, j * W_BG + h * HEAD_DIM:j * W_BG + (h + 1) * HEAD_DIM] = (ob[h] * a_h * gate).astype(BF16)


def _mix_b(proj, o_b, lse_b, seq, tm=512):
    m = proj.shape[0]
    nsb = seq // tm
    gate = lambda c: pl.BlockSpec((tm, W_BG), lambda i: (i, (GATE_OFF + W_A) // W_BG + c))

    def classes(a):
        dil, w = a.shape[1], a.shape[3]
        return pl.BlockSpec((1, dil, tm // dil, w), lambda i: (i // nsb, 0, i % nsb, 0))

    return pl.pallas_call(
        _mix_b_kernel,
        out_shape=jax.ShapeDtypeStruct((m, W_B), BF16),
        grid=(m // tm,),
        in_specs=[gate(0), gate(1), gate(2), *[classes(a) for a in o_b], *[classes(a) for a in lse_b]],
        out_specs=pl.BlockSpec((tm, W_B), lambda i: (i, 0)),
        scratch_shapes=[pltpu.VMEM((W_BG // 128, tm, 128), F32), pltpu.VMEM((1, tm, 128), F32)],
        compiler_params=_params(("parallel",)),
        name="mix_b",
    )(proj, proj, proj, *o_b, *lse_b)


def _layout_w_uq(w):
    w = w.reshape(A_Q_LORA, A_HEADS, A_NOPE + A_ROPE)
    w = jnp.pad(w, ((0, 0), (0, 0), (0, A_QK - A_NOPE - A_ROPE)))
    return w.reshape(A_Q_LORA, A_HEADS * A_QK).astype(BF16)


def _layer(x2, b, s, layer, g_attn, w_in_t, g_qa, g_kva, w_uq, w_ukv, g_qn, g_kn, sinks, w_out_b, rel_bias, tables):
    cos_a, sin_a, cos_c, sin_c = tables
    h = _rmsnorm(x2, g_attn, BF16)
    proj = _in_proj(h, w_in_t, layer)
    proj3 = proj.reshape(b, s, PROJ_W)

    q_a, k_a, v_a = _prep_a(proj, g_qa, g_kva, _layout_w_uq(w_uq), w_ukv.astype(BF16), cos_a, sin_a, s)
    o_a = _dense_attn(q_a.reshape(b, s, -1), k_a.reshape(b, s, -1), v_a.reshape(b, s, -1), proj3,
                      kv_heads=A_HEADS, groups=1, dq=A_QK, dv=A_V, k_col0=0, v_col0=0,
                      g_col0=GATE_OFF // A_V, tq=2048, name="attn_a")

    o_b, lse_b = [], []
    for j, (win, dil) in enumerate(B_PAIRS):
        tab = rel_bias[:, j * B_HEADS_PER_PAIR:(j + 1) * B_HEADS_PER_PAIR]
        bias = _band_bias(tab, dil, win // (2 * dil), BAND_TQ)
        o, lse = _dilated_group(proj3, bias, j, dil)
        o_b.append(o)
        lse_b.append(lse)

    q_c, k_c = _prep_c(proj, g_qn, g_kn, cos_c, sin_c, s)
    gc = C_HEADS // C_KV_HEADS
    o_c = _dense_attn(q_c.reshape(b, s, -1), k_c.reshape(b, s, -1), proj3, proj3,
                      kv_heads=C_KV_HEADS, groups=gc, dq=HEAD_DIM, dv=HEAD_DIM,
                      k_col0=0, v_col0=CV_OFF // HEAD_DIM, g_col0=(GATE_OFF + W_A + W_B) // (gc * HEAD_DIM),
                      tq=512, name="attn_c")

    g = D_HEADS // D_KV_HEADS
    bias_d = _band_bias(rel_bias[:, B_HEADS:], 1, D_WINDOW, BAND_TQ)
    sink_rows = jnp.repeat(sinks.astype(F32) * LOG2E, BAND_TQ).reshape(D_KV_HEADS, 1, g * BAND_TQ)
    o_d = _window_attn(proj3, bias_d, sink_rows)

    parts = (o_a.reshape(b * s, W_A), _mix_b(proj, o_b, lse_b, s), o_c.reshape(b * s, W_C), o_d.reshape(b * s, W_D))
    return _out_proj(parts, w_out_b, layer, x2)


def kernel(x, g_attn, w_in, g_qa, g_kva, w_uq, w_ukv, g_qn, g_kn, sinks, w_out, rel_bias, g_final):
    b, s, d = x.shape
    depth = w_in.shape[0]
    tables = _rope_tables(s)
    x2 = x.reshape(b * s, d)
    w_in_t = jnp.swapaxes(w_in, 1, 2)
    w_out_b = w_out.astype(BF16)
    for l in range(depth):
        x2 = _layer(x2, b, s, l, g_attn[l], w_in_t, g_qa[l], g_kva[l], w_uq[l], w_ukv[l], g_qn[l], g_kn[l],
                    sinks[l], w_out_b, rel_bias, tables)
    return _rmsnorm(x2, g_final, F32).reshape(b, s, d)
```
